```python
import math
import jax, jax.numpy as jnp
from jax import lax
import numpy as np

D_MODEL = 2048
BATCH = 1
SEQ = 8192
DEPTH = 4

N_MIXERS = 3
N_RWKV = (DEPTH + 2) // 3
N_RET = (DEPTH + 1) // 3
N_MLSTM = DEPTH // 3
PLE_DIM = 256
D_FF = 5632
NORM_EPS = 1e-6

RWKV_HEAD = 64
RWKV_HEADS = D_MODEL // RWKV_HEAD
RWKV_DECAY_LORA = 96
RWKV_AAA_LORA = 96
RWKV_MV_LORA = 64
RWKV_GATE_LORA = 256
RWKV_LNX_EPS = 64e-5

RET_HEADS = 8
RET_QK_TOT = D_MODEL
RET_V_TOT = 2 * D_MODEL
RET_QK = RET_QK_TOT // RET_HEADS
RET_V = RET_V_TOT // RET_HEADS
RET_CHUNK = 128
RET_GN_EPS = 1e-6
ROPE_BASE = 10000.0

ML_HEADS = 8
ML_QK_TOT = D_MODEL // 2
ML_V_TOT = D_MODEL
ML_QK = ML_QK_TOT // ML_HEADS
ML_V = ML_V_TOT // ML_HEADS
ML_PROJ = 2 * ML_QK_TOT + 2 * ML_V_TOT + 2 * ML_HEADS
ML_CHUNK = 64
ML_IGATE_CAP = 15.0
ML_NORM_EPS = 1e-6

kernel_name = "hybrid_rwkv7_retnet_mlstm_macaron"


def _rmsnorm(x, gain):
    xf = x.astype(jnp.float32)
    y = xf * lax.rsqrt(jnp.mean(xf * xf, axis=-1, keepdims=True) + NORM_EPS)
    return (y * gain.astype(jnp.float32)).astype(x.dtype)


def _group_norm(x, eps):
    mu = jnp.mean(x, axis=-1, keepdims=True)
    xc = x - mu
    return xc * lax.rsqrt(jnp.mean(xc * xc, axis=-1, keepdims=True) + eps)


def _head_rms(x, eps):
    return x * lax.rsqrt(jnp.mean(x * x, axis=-1, keepdims=True) + eps)


def _swiglu(x, w_in, w_out):
    gate, up = jnp.split(x @ w_in, 2, axis=-1)
    return (jax.nn.silu(gate) * up) @ w_out


def _token_shift(x):
    return jnp.pad(x[:, :-1], ((0, 0), (1, 0), (0, 0)))


def _chunk(z, L):
    Bsz, T = z.shape[:2]
    z = z.reshape(Bsz, T // L, L, *z.shape[2:])
    return jnp.moveaxis(jnp.moveaxis(z, 1, 0), 2, 3)


def _unchunk(z):
    z = jnp.moveaxis(jnp.moveaxis(z, 3, 2), 0, 1)
    return z.reshape(z.shape[0], -1, *z.shape[3:])


def _rope(x, positions):
    half = x.shape[-1] // 2
    freqs = ROPE_BASE ** (-jnp.arange(half, dtype=jnp.float32) / half)
    ang = positions.astype(jnp.float32)[..., None] * freqs
    cos, sin = jnp.cos(ang)[:, :, None, :], jnp.sin(ang)[:, :, None, :]
    x1, x2 = x[..., :half], x[..., half:]
    return jnp.concatenate([x1 * cos - x2 * sin, x1 * sin + x2 * cos], axis=-1)


def _rwkv7_scan(r, w, k, v, a, b):
    Bsz, T, H, N = r.shape

    def step(S, xs):
        r_t, w_t, k_t, v_t, a_t, b_t = xs
        sa = jnp.einsum('bhij,bhj->bhi', S, a_t)
        S = (S * w_t[:, :, None, :] + sa[..., None] * b_t[:, :, None, :]
             + v_t[..., None] * k_t[:, :, None, :])
        return S, jnp.einsum('bhij,bhj->bhi', S, r_t)

    xs = tuple(jnp.moveaxis(z, 1, 0) for z in (r, w, k, v, a, b))
    S0 = jnp.zeros((Bsz, H, N, N), jnp.float32)
    _, y = lax.scan(step, S0, xs)
    return jnp.moveaxis(y, 0, 1)


def _rwkv7_time_mix(u, lerp, w0, w1, w2, a0, a1, a2, g1, g2, k_k, k_a, r_k,
                    w_rkv, w_o, lnx_g, lnx_b, v_first, v_res):
    Bsz, T, C = u.shape
    H, N = RWKV_HEADS, RWKV_HEAD
    xx = _token_shift(u) - u
    xr, xw, xk, xv, xa, xg = (u + xx * lerp[j] for j in range(6))
    r = xr @ w_rkv[0]
    k = xk @ w_rkv[1]
    v = xv @ w_rkv[2]
    w = -jax.nn.softplus(-(w0 + jnp.tanh(xw @ w1) @ w2)) - 0.5
    if v_res is not None:
        v0, v1, v2 = v_res
        v = v + (v_first - v) * jax.nn.sigmoid(v0 + (xv @ v1) @ v2)
    a = jax.nn.sigmoid(a0 + (xa @ a1) @ a2)
    g = jax.nn.sigmoid(xg @ g1) @ g2

    def heads(z):
        return z.astype(jnp.float32).reshape(Bsz, T, H, N)

    kk = heads(k * k_k)
    kk = kk / jnp.maximum(jnp.linalg.norm(kk, axis=-1, keepdims=True), 1e-12)
    k = k * (1 + (a - 1) * k_a)
    rh, kh, vh, ah = heads(r), heads(k), heads(v), heads(a)
    decay = jnp.exp(-jnp.exp(heads(w)))
    y = _rwkv7_scan(rh, decay, kh, vh, -kk, kk * ah)
    y = _group_norm(y, RWKV_LNX_EPS).reshape(Bsz, T, C) * lnx_g + lnx_b
    bonus = jnp.sum(rh * kh * r_k, axis=-1, keepdims=True) * vh
    y = y + bonus.reshape(Bsz, T, C)
    return (y.astype(u.dtype) * g) @ w_o, v


def _retention_chunkwise(q, k, v):
    L = RET_CHUNK
    Bsz, T, H, dk = q.shape
    dv = v.shape[-1]
    log_gamma = jnp.log(1.0 - 2.0 ** (-5.0 - jnp.arange(H, dtype=jnp.float32)))
    idx = jnp.arange(L, dtype=jnp.float32)
    rel = idx[:, None] - idx[None, :]
    intra = jnp.where(rel >= 0, jnp.exp(jnp.maximum(rel, 0.0) * log_gamma[:, None, None]), 0.0)
    xi = jnp.exp((idx + 1.0) * log_gamma[:, None])[:, :, None]
    zeta = jnp.exp((L - 1.0 - idx) * log_gamma[:, None])[:, :, None]
    chunk_decay = jnp.exp(L * log_gamma)[:, None, None]

    def step(R, xs):
        qc, kc, vc = xs
        s = jnp.einsum('bhld,bhmd->bhlm', qc, kc) * intra
        o = (jnp.einsum('bhlm,bhmv->bhlv', s, vc)
             + jnp.einsum('bhld,bhdv->bhlv', qc, R) * xi)
        R = chunk_decay * R + jnp.einsum('bhmd,bhmv->bhdv', kc * zeta, vc)
        return R, o

    R0 = jnp.zeros((Bsz, H, dk, dv), jnp.float32)
    _, o = lax.scan(step, R0, (_chunk(q, L), _chunk(k, L), _chunk(v, L)))
    return _unchunk(o)


def _retention(u, positions, w_in, ln_g, w_o):
    Bsz, T, C = u.shape
    H = RET_HEADS
    f32 = jnp.float32
    q, k, v, g = jnp.split(u @ w_in, [RET_QK_TOT, 2 * RET_QK_TOT, 2 * RET_QK_TOT + RET_V_TOT], axis=-1)
    q = _rope(q.astype(f32).reshape(Bsz, T, H, RET_QK), positions)
    k = _rope(k.astype(f32).reshape(Bsz, T, H, RET_QK), positions) * (RET_QK ** -0.5)
    v = v.astype(f32).reshape(Bsz, T, H, RET_V)
    o = _group_norm(_retention_chunkwise(q, k, v), RET_GN_EPS).reshape(Bsz, T, RET_V_TOT) * ln_g
    return (jax.nn.silu(g) * o.astype(u.dtype)) @ w_o


def _mlstm_chunkwise(q, k, v, log_i, log_f):
    L = ML_CHUNK
    Bsz, T, H, dqk = q.shape
    dv = v.shape[-1]
    causal = jnp.tril(jnp.ones((L, L), dtype=bool))

    def step(carry, xs):
        C, n, m = carry
        qc, kc, vc, ic, fc = xs
        b = jnp.cumsum(fc, axis=-1)
        log_inter = b + m[..., None]
        log_intra = jnp.where(causal, b[..., :, None] - b[..., None, :] + ic[..., None, :], -jnp.inf)
        m_t = jnp.maximum(log_inter, jnp.max(log_intra, axis=-1))
        w_inter = jnp.exp(log_inter - m_t)
        w_intra = jnp.exp(log_intra - m_t[..., None])
        s = jnp.einsum('bhtd,bhsd->bhts', qc, kc) * w_intra
        num = (jnp.einsum('bhts,bhsv->bhtv', s, vc)
               + w_inter[..., None] * jnp.einsum('bhtd,bhdv->bhtv', qc, C))
        den = jnp.sum(s, axis=-1) + w_inter * jnp.einsum('bhtd,bhd->bht', qc, n)
        h = num / jnp.maximum(jnp.abs(den), jnp.exp(-m_t))[..., None]
        b_last = b[..., -1]
        log_s = b_last[..., None] - b + ic
        m_new = jnp.maximum(b_last + m, jnp.max(log_s, axis=-1))
        ws = jnp.exp(log_s - m_new[..., None])
        carry_decay = jnp.exp(b_last + m - m_new)
        C = carry_decay[..., None, None] * C + jnp.einsum('bhs,bhsd,bhsv->bhdv', ws, kc, vc)
        n = carry_decay[..., None] * n + jnp.einsum('bhs,bhsd->bhd', ws, kc)
        return (C, n, m_new), h

    init = (jnp.zeros((Bsz, H, dqk, dv), jnp.float32),
            jnp.zeros((Bsz, H, dqk), jnp.float32),
            jnp.zeros((Bsz, H), jnp.float32))
    xs = (_chunk(q, L), _chunk(k, L), _chunk(v, L), _chunk(log_i, L), _chunk(log_f, L))
    _, h = lax.scan(step, init, xs)
    return _unchunk(h)


def _mlstm(u, w_in, gate_b, mh_g, w_o):
    Bsz, T, C = u.shape
    H = ML_HEADS
    f32 = jnp.float32
    o1 = 2 * ML_QK_TOT + ML_V_TOT
    o2 = o1 + ML_V_TOT
    q, k, v, og, gi, gf = jnp.split(u @ w_in, [ML_QK_TOT, 2 * ML_QK_TOT, o1, o2, o2 + H], axis=-1)
    q = q.astype(f32).reshape(Bsz, T, H, ML_QK)
    k = k.astype(f32).reshape(Bsz, T, H, ML_QK) * (ML_QK ** -0.5)
    v = v.astype(f32).reshape(Bsz, T, H, ML_V)
    log_i = ML_IGATE_CAP * jnp.tanh((gi.astype(f32) + gate_b[0]) / ML_IGATE_CAP)
    log_f = jax.nn.log_sigmoid(gf.astype(f32) + gate_b[1])
    h = _mlstm_chunkwise(q, k, v, log_i, log_f)
    h = _head_rms(h, ML_NORM_EPS).reshape(Bsz, T, C) * mh_g
    return (jax.nn.sigmoid(og) * h.astype(u.dtype)) @ w_o


def setup_inputs(seed: int = 0) -> dict:
    key = jax.random.key(seed)
    ks = iter(jax.random.split(key, 64))
    f32 = jnp.float32

    def nrm(shape, scale=1.0):
        return scale * jax.random.normal(next(ks), shape, f32)

    def dense(shape, fan_in, scale=1.0):
        return nrm(shape, scale * fan_in ** -0.5)

    def unif(shape, lo, hi):
        return jax.random.uniform(next(ks), shape, f32, lo, hi)

    D = D_MODEL
    inputs = {
        "x": nrm((BATCH, SEQ, D)),
        "p": nrm((DEPTH, BATCH, SEQ, PLE_DIM)),
        "positions": jnp.broadcast_to(jnp.arange(SEQ, dtype=jnp.int32), (BATCH, SEQ)),
        "norm_g": 1.0 + nrm((DEPTH, 4, D), 0.05),
        "final_g": 1.0 + nrm((D,), 0.05),
        "ffn_in": dense((DEPTH, 2, D, 2 * D_FF), D),
        "ffn_out": dense((DEPTH, 2, D_FF, D), D_FF),
        "ple_proj": dense((DEPTH, PLE_DIM, D), PLE_DIM),
        "ple_gate": dense((DEPTH, D, D), D),
        "rwkv_lerp": unif((N_RWKV, 6, D), 0.0, 1.0),
        "rwkv_w0": unif((N_RWKV, D), -6.0, -1.0),
        "rwkv_w1": dense((N_RWKV, D, RWKV_DECAY_LORA), D),
        "rwkv_w2": dense((N_RWKV, RWKV_DECAY_LORA, D), RWKV_DECAY_LORA, 0.1),
        "rwkv_a0": nrm((N_RWKV, D), 0.1),
        "rwkv_a1": dense((N_RWKV, D, RWKV_AAA_LORA), D),
        "rwkv_a2": dense((N_RWKV, RWKV_AAA_LORA, D), RWKV_AAA_LORA, 0.1),
        "rwkv_g1": dense((N_RWKV, D, RWKV_GATE_LORA), D),
        "rwkv_g2": dense((N_RWKV, RWKV_GATE_LORA, D), RWKV_GATE_LORA),
        "rwkv_kk": 0.85 + nrm((N_RWKV, D), 0.05),
        "rwkv_ka": 1.0 + nrm((N_RWKV, D), 0.05),
        "rwkv_rk": nrm((N_RWKV, RWKV_HEADS, RWKV_HEAD), 0.1),
        "rwkv_w_rkv": dense((N_RWKV, 3, D, D), D),
        "rwkv_w_o": dense((N_RWKV, D, D), D),
        "rwkv_lnx_g": 1.0 + nrm((N_RWKV, D), 0.05),
        "rwkv_lnx_b": nrm((N_RWKV, D), 0.02),
        "rwkv_v0": 1.0 + nrm((N_RWKV - 1, D), 0.1),
        "rwkv_v1": dense((N_RWKV - 1, D, RWKV_MV_LORA), D),
        "rwkv_v2": dense((N_RWKV - 1, RWKV_MV_LORA, D), RWKV_MV_LORA, 0.1),
        "ret_w_in": dense((N_RET, D, 2 * RET_QK_TOT + 2 * RET_V_TOT), D),
        "ret_ln_g": 1.0 + nrm((N_RET, RET_V_TOT), 0.05),
        "ret_w_o": dense((N_RET, RET_V_TOT, D), RET_V_TOT),
        "ml_w_in": dense((N_MLSTM, D, ML_PROJ), D),
        "ml_gate_b": jnp.stack([nrm((N_MLSTM, ML_HEADS), 0.1),
                                unif((N_MLSTM, ML_HEADS), 3.0, 6.0)], axis=1),
        "ml_mh_g": 1.0 + nrm((N_MLSTM, ML_V_TOT), 0.05),
        "ml_w_o": dense((N_MLSTM, ML_V_TOT, D), ML_V_TOT),
    }
    return inputs


def reference(x, p, positions, norm_g, final_g, ffn_in, ffn_out, ple_proj, ple_gate,
              rwkv_lerp, rwkv_w0, rwkv_w1, rwkv_w2, rwkv_a0, rwkv_a1, rwkv_a2,
              rwkv_g1, rwkv_g2, rwkv_kk, rwkv_ka, rwkv_rk, rwkv_w_rkv, rwkv_w_o,
              rwkv_lnx_g, rwkv_lnx_b, rwkv_v0, rwkv_v1, rwkv_v2,
              ret_w_in, ret_ln_g, ret_w_o,
              ml_w_in, ml_gate_b, ml_mh_g, ml_w_o):
    h = x
    v_first = None
    for i in range(DEPTH):
        kind, j = i % N_MIXERS, i // N_MIXERS
        h = h + 0.5 * _swiglu(_rmsnorm(h, norm_g[i, 0]), ffn_in[i, 0], ffn_out[i, 0])
        u = _rmsnorm(h, norm_g[i, 1])
        if kind == 0:
            v_res = (rwkv_v0[j - 1], rwkv_v1[j - 1], rwkv_v2[j - 1]) if j > 0 else None
            mix, v_layer = _rwkv7_time_mix(
                u, rwkv_lerp[j], rwkv_w0[j], rwkv_w1[j], rwkv_w2[j], rwkv_a0[j], rwkv_a1[j],
                rwkv_a2[j], rwkv_g1[j], rwkv_g2[j], rwkv_kk[j], rwkv_ka[j], rwkv_rk[j],
                rwkv_w_rkv[j], rwkv_w_o[j], rwkv_lnx_g[j], rwkv_lnx_b[j], v_first, v_res)
            if j == 0:
                v_first = v_layer
        elif kind == 1:
            mix = _retention(u, positions, ret_w_in[j], ret_ln_g[j], ret_w_o[j])
        else:
            mix = _mlstm(u, ml_w_in[j], ml_gate_b[j], ml_mh_g[j], ml_w_o[j])
        h = h + mix
        h = h + 0.5 * _swiglu(_rmsnorm(h, norm_g[i, 2]), ffn_in[i, 1], ffn_out[i, 1])
        gate = jax.nn.sigmoid(_rmsnorm(h, norm_g[i, 3]) @ ple_gate[i])
        h = h + (p[i] @ ple_proj[i]) * gate
    return _rmsnorm(h, final_g)
```

```python
import functools

import jax
import jax.numpy as jnp
from jax import lax
from jax.experimental import pallas as pl
from jax.experimental.pallas import tpu as pltpu

F32 = jnp.float32
BF16 = jnp.bfloat16

NORM_EPS = 1e-6
RWKV_HEAD = 64
RWKV_LNX_EPS = 64e-5
RWKV_CHUNK = 64
RET_HEADS = 8
RET_CHUNK = 128
RET_GN_EPS = 1e-6
ROPE_BASE = 10000.0
ML_HEADS = 8
ML_CHUNK = 64
ML_IGATE_CAP = 15.0
ML_NORM_EPS = 1e-6

LANES = 128
VMEM_LIMIT_BYTES = 56 * 1024 * 1024
HI = lax.Precision.HIGHEST


def _params(*sem):
    return pltpu.CompilerParams(dimension_semantics=sem, vmem_limit_bytes=VMEM_LIMIT_BYTES)


def _rms(x, gain):
    return x * lax.rsqrt(jnp.mean(x * x, axis=-1, keepdims=True) + NORM_EPS) * gain


def _dot(a, b):
    return jnp.dot(a, b, preferred_element_type=F32)


def _dot_nt(a, b):
    return lax.dot_general(a, b, (((1,), (1,)), ((), ())), preferred_element_type=F32)


def _dot_tn(a, b):
    return lax.dot_general(a, b, (((0,), (0,)), ((), ())), preferred_element_type=F32)


def _ffn_kernel(nf, x_ref, g_ref, wg_ref, wu_ref, wo_ref, o_ref, xn_ref):
    f = pl.program_id(1)

    @pl.when(f == 0)
    def _():
        xn_ref[...] = _rms(x_ref[...], g_ref[...]).astype(BF16)

    xn = xn_ref[...]
    gate = _dot(xn, wg_ref[...])
    up = _dot(xn, wu_ref[...])
    act = (gate * jax.nn.sigmoid(gate) * up).astype(BF16)
    part = _dot(act, wo_ref[...])

    @pl.when(f == 0)
    def _():
        o_ref[...] = part

    @pl.when(f > 0)
    def _():
        o_ref[...] += part

    @pl.when(f == nf - 1)
    def _():
        o_ref[...] = x_ref[...] + 0.5 * o_ref[...]


def _ffn(h, gain, w_in, w_out, tm=512, tf=512):
    T, D = h.shape
    F = w_out.shape[0]
    nf = F // tf
    return pl.pallas_call(
        functools.partial(_ffn_kernel, nf),
        grid=(T // tm, nf),
        in_specs=[
            pl.BlockSpec((tm, D), lambda i, f: (i, 0)),
            pl.BlockSpec((1, D), lambda i, f: (0, 0)),
            pl.BlockSpec((D, tf), lambda i, f: (0, f)),
            pl.BlockSpec((D, tf), lambda i, f: (0, nf + f)),
            pl.BlockSpec((tf, D), lambda i, f: (f, 0)),
        ],
        out_specs=pl.BlockSpec((tm, D), lambda i, f: (i, 0)),
        out_shape=jax.ShapeDtypeStruct((T, D), F32),
        scratch_shapes=[pltpu.VMEM((tm, D), BF16)],
        compiler_params=_params("parallel", "arbitrary"),
        name="ffn",
    )(h, gain.reshape(1, D), w_in, w_in, w_out)


def _norm_matmul_kernel(x_ref, g_ref, w_ref, o_ref, xn_ref):
    @pl.when(pl.program_id(1) == 0)
    def _():
        xn_ref[...] = _rms(x_ref[...], g_ref[...]).astype(BF16)

    o_ref[...] = _dot(xn_ref[...], w_ref[...])


def _norm_matmul(h, gain, w, tm=512, tn=1024):
    T, D = h.shape
    N = w.shape[1]
    return pl.pallas_call(
        _norm_matmul_kernel,
        grid=(T // tm, N // tn),
        in_specs=[
            pl.BlockSpec((tm, D), lambda i, j: (i, 0)),
            pl.BlockSpec((1, D), lambda i, j: (0, 0)),
            pl.BlockSpec((D, tn), lambda i, j: (0, j)),
        ],
        out_specs=pl.BlockSpec((tm, tn), lambda i, j: (i, j)),
        out_shape=jax.ShapeDtypeStruct((T, N), F32),
        scratch_shapes=[pltpu.VMEM((tm, D), BF16)],
        compiler_params=_params("parallel", "arbitrary"),
        name="norm_matmul",
    )(h, gain.reshape(1, D), w)


def _matmul_residual_kernel(h_ref, a_ref, w_ref, o_ref):
    o_ref[...] = h_ref[...] + _dot(a_ref[...], w_ref[...])


def _matmul_residual(h, a, w, tm=512, tn=1024):
    T, D = h.shape
    K = a.shape[1]
    return pl.pallas_call(
        _matmul_residual_kernel,
        grid=(D // tn, T // tm),
        in_specs=[
            pl.BlockSpec((tm, tn), lambda j, i: (i, j)),
            pl.BlockSpec((tm, K), lambda j, i: (i, 0)),
            pl.BlockSpec((K, tn), lambda j, i: (0, j)),
        ],
        out_specs=pl.BlockSpec((tm, tn), lambda j, i: (i, j)),
        out_shape=jax.ShapeDtypeStruct((T, D), F32),
        compiler_params=_params("parallel", "parallel"),
        name="matmul_residual",
    )(h, a, w)


def _ple_kernel(final, h_ref, g_ref, p_ref, wp_ref, wg_ref, fg_ref, o_ref):
    h = h_ref[...]
    gate = jax.nn.sigmoid(_dot(_rms(h, g_ref[...]).astype(BF16), wg_ref[...]))
    out = h + _dot(p_ref[...].astype(BF16), wp_ref[...]) * gate
    if final:
        out = _rms(out, fg_ref[...])
    o_ref[...] = out


def _ple(h, gain, p, w_proj, w_gate, final_gain, final, tm=512):
    T, D = h.shape
    P = p.shape[1]
    return pl.pallas_call(
        functools.partial(_ple_kernel, final),
        grid=(T // tm,),
        in_specs=[
            pl.BlockSpec((tm, D), lambda i: (i, 0)),
            pl.BlockSpec((1, D), lambda i: (0, 0)),
            pl.BlockSpec((tm, P), lambda i: (i, 0)),
            pl.BlockSpec((P, D), lambda i: (0, 0)),
            pl.BlockSpec((D, D), lambda i: (0, 0)),
            pl.BlockSpec((1, D), lambda i: (0, 0)),
        ],
        out_specs=pl.BlockSpec((tm, D), lambda i: (i, 0)),
        out_shape=jax.ShapeDtypeStruct((T, D), F32),
        compiler_params=_params("parallel"),
        name="ple",
    )(h, gain.reshape(1, D), p, w_proj, w_gate, final_gain.reshape(1, D))


def _rwkv_proj_kernel(has_vres, *refs):
    if has_vres:
        (h_ref, halo_ref, g_ref, lerp_ref, wr_ref, wk_ref, wv_ref, w1_ref, w2_ref, a1_ref, a2_ref,
         g1_ref, g2_ref, w0_ref, a0_ref, v1_ref, v2_ref, v0_ref, vf_ref,
         r_out, k_out, v_out, a_out, lw_out, g_out,
         xr_s, xk_s, xv_s, hw_s, ha_s, hg_s, hv_s) = refs
    else:
        (h_ref, halo_ref, g_ref, lerp_ref, wr_ref, wk_ref, wv_ref, w1_ref, w2_ref, a1_ref, a2_ref,
         g1_ref, g2_ref, w0_ref, a0_ref,
         r_out, k_out, v_out, a_out, lw_out, g_out,
         xr_s, xk_s, xv_s, hw_s, ha_s, hg_s) = refs
    i = pl.program_id(0)

    @pl.when(pl.program_id(1) == 0)
    def _():
        gain = g_ref[...]
        u = _rms(h_ref[...], gain)
        halo = _rms(halo_ref[...], gain)
        first = jnp.where(i > 0, halo[7:8, :], 0.0)
        row = lax.broadcasted_iota(jnp.int32, u.shape, 0)
        u_prev = jnp.where(row == 0, first, pltpu.roll(u, 1, 0))
        xx = u_prev - u
        lerp = lerp_ref[...]
        xr_s[...] = (u + xx * lerp[0:1]).astype(BF16)
        xw = (u + xx * lerp[1:2]).astype(BF16)
        xk_s[...] = (u + xx * lerp[2:3]).astype(BF16)
        xv = (u + xx * lerp[3:4]).astype(BF16)
        xv_s[...] = xv
        xa = (u + xx * lerp[4:5]).astype(BF16)
        xg = (u + xx * lerp[5:6]).astype(BF16)
        hw_s[...] = jnp.tanh(_dot(xw, w1_ref[...])).astype(BF16)
        ha_s[...] = _dot(xa, a1_ref[...]).astype(BF16)
        hg_s[...] = jax.nn.sigmoid(_dot(xg, g1_ref[...])).astype(BF16)
        if has_vres:
            hv_s[...] = _dot(xv, v1_ref[...]).astype(BF16)

    r_out[...] = _dot(xr_s[...], wr_ref[...])
    k_out[...] = _dot(xk_s[...], wk_ref[...])
    v = _dot(xv_s[...], wv_ref[...])
    if has_vres:
        mix = jax.nn.sigmoid(v0_ref[...] + _dot(hv_s[...], v2_ref[...]))
        v = v + (vf_ref[...] - v) * mix
    v_out[...] = v
    z = -(w0_ref[...] + _dot(hw_s[...], w2_ref[...]))
    softplus = jnp.maximum(z, 0.0) + jnp.log1p(jnp.exp(-jnp.abs(z)))
    lw_out[...] = -jnp.exp(-softplus - 0.5)
    a_out[...] = jax.nn.sigmoid(a0_ref[...] + _dot(ha_s[...], a2_ref[...]))
    g_out[...] = _dot(hg_s[...], g2_ref[...])


def _pad_cols(w, n):
    return jnp.pad(w, ((0, 0), (0, n - w.shape[1])))


def _pad_rows(w, n):
    return jnp.pad(w, ((0, n - w.shape[0]), (0, 0)))


def _rwkv_proj(h, gain, lerp, w_rkv, w1, w2, a1, a2, g1, g2, w0, a0, v_res, v_first, tm=512, tn=512):
    T, D = h.shape
    has_vres = v_res is not None
    lo = LANES
    row = lambda x: x.reshape(1, D)
    full = lambda shape: pl.BlockSpec(shape, lambda i, j: (0, 0))
    col = lambda k: pl.BlockSpec((k, tn), lambda i, j: (0, j))
    tile = pl.BlockSpec((tm, tn), lambda i, j: (i, j))
    rowtile = pl.BlockSpec((1, tn), lambda i, j: (0, j))
    hb = tm // 8
    gd = g1.shape[1]
    args = [h, h, row(gain), _pad_rows(lerp, 8),
            w_rkv[0], w_rkv[1], w_rkv[2],
            _pad_cols(w1, lo), _pad_rows(w2, lo), _pad_cols(a1, lo), _pad_rows(a2, lo),
            g1, g2, row(w0), row(a0)]
    in_specs = [
        pl.BlockSpec((tm, D), lambda i, j: (i, 0)),
        pl.BlockSpec((8, D), lambda i, j: (jnp.maximum(i * hb - 1, 0), 0)),
        full((1, D)), full((8, D)),
        col(D), col(D), col(D),
        full((D, lo)), col(lo), full((D, lo)), col(lo),
        full((D, gd)), col(gd), rowtile, rowtile,
    ]
    scratch = [pltpu.VMEM((tm, D), BF16)] * 3 + [
        pltpu.VMEM((tm, lo), BF16), pltpu.VMEM((tm, lo), BF16), pltpu.VMEM((tm, gd), BF16)]
    if has_vres:
        v0, v1, v2 = v_res
        args += [_pad_cols(v1, lo), _pad_rows(v2, lo), row(v0), v_first]
        in_specs += [full((D, lo)), col(lo), rowtile, tile]
        scratch += [pltpu.VMEM((tm, lo), BF16)]
    return pl.pallas_call(
        functools.partial(_rwkv_proj_kernel, has_vres),
        grid=(T // tm, D // tn),
        in_specs=in_specs,
        out_specs=[tile] * 6,
        out_shape=[jax.ShapeDtypeStruct((T, D), F32)] * 6,
        scratch_shapes=scratch,
        compiler_params=_params("parallel", "arbitrary"),
        name="rwkv_proj",
    )(*args)


def _rwkv_scan_kernel(npairs, r_ref, k_ref, v_ref, a_ref, lw_ref, g_ref,
                      kk_ref, ka_ref, rk_ref, lg_ref, lb_ref, o_ref, s_ref):
    L = RWKV_CHUNK
    N = RWKV_HEAD
    W = 2 * N

    @pl.when(pl.program_id(1) == 0)
    def _():
        s_ref[...] = jnp.zeros_like(s_ref)

    t_i = lax.broadcasted_iota(jnp.int32, (L, W), 0)
    lane = lax.broadcasted_iota(jnp.int32, (L, W), 1)
    s_i = lane % N
    head0 = lane < N
    strict = s_i < t_i
    incl = s_i <= t_i
    same16 = (t_i // 16) == (s_i // 16)
    same32 = (t_i // 32) == (s_i // 32)
    eye = jnp.where(s_i == t_i, 1.0, 0.0)
    brow = lax.broadcasted_iota(jnp.int32, (W, W), 0)
    blane = lax.broadcasted_iota(jnp.int32, (W, W), 1)
    bdiag = (brow // N) == (blane // N)

    def blk(x):
        return jnp.where(bdiag, jnp.concatenate([x, x], axis=0), 0.0)

    def segsum(x):
        s0 = jnp.sum(jnp.where(head0, x, 0.0), axis=1, keepdims=True)
        s1 = jnp.sum(jnp.where(head0, 0.0, x), axis=1, keepdims=True)
        return jnp.where(head0, s0, s1)

    def mm_hi(x, y):
        return jnp.dot(x, blk(y), precision=HI, preferred_element_type=F32)

    tri_r = lax.broadcasted_iota(jnp.int32, (L, L), 0)
    tri_c = lax.broadcasted_iota(jnp.int32, (L, L), 1)
    tri = jnp.where(tri_c <= tri_r, 1.0, 0.0)
    c_all = jnp.dot(tri, lw_ref[...], precision=HI, preferred_element_type=F32)

    for p in range(npairs):
        sl = slice(p * W, (p + 1) * W)
        r = r_ref[:, sl]
        k = k_ref[:, sl]
        v = v_ref[:, sl]
        a = a_ref[:, sl]
        lw = lw_ref[:, sl]
        c = c_all[:, sl]

        kk = k * kk_ref[:, sl]
        kk = kk / jnp.maximum(jnp.sqrt(segsum(kk * kk)), 1e-12)
        kmod = k * (1.0 + (a - 1.0) * ka_ref[:, sl])
        alpha = -kk
        beta = kk * a

        c_mid = c[L // 2 - 1:L // 2, :]
        c_last = c[L - 1:L, :]
        e = c - c_mid
        ex_m = jnp.exp(-e)
        r_t = r * jnp.exp(e)
        al_t = alpha * jnp.exp(e - lw)
        be_t = beta * ex_m
        k_t = kmod * ex_m
        r_abs = r * jnp.exp(c)
        al_abs = alpha * jnp.exp(c - lw)
        to_end = jnp.exp(c_last - c)
        be_hat = beta * to_end
        k_hat = kmod * to_end
        p_last = jnp.exp(c_last)

        lhs = jnp.concatenate([al_t, r_t], axis=0).astype(BF16)
        rhs = jnp.concatenate([blk(be_t), blk(k_t)], axis=0).astype(BF16)
        amat = _dot_nt(lhs, rhs)
        n_ab = jnp.where(strict, amat[:L, :W], 0.0)
        a_ak = jnp.where(strict, amat[:L, W:], 0.0)
        a_rb = jnp.where(incl, amat[L:, :W], 0.0)
        a_rk = jnp.where(incl, amat[L:, W:], 0.0)

        nd = jnp.where(same16, n_ab, 0.0)
        tinv = eye + nd
        pw = mm_hi(nd, nd)
        tinv = tinv + mm_hi(tinv, pw)
        pw = mm_hi(pw, pw)
        tinv = tinv + mm_hi(tinv, pw)
        pw = mm_hi(pw, pw)
        tinv = tinv + mm_hi(tinv, pw)
        off = jnp.where(same32 & jnp.logical_not(same16), n_ab, 0.0)
        tinv = tinv + mm_hi(tinv, mm_hi(off, tinv))
        off = jnp.where(same32, 0.0, n_ab)
        tinv = tinv + mm_hi(tinv, mm_hi(off, tinv))

        akv = _dot(a_ak.astype(BF16), blk(v).astype(BF16))
        ta = _dot(tinv.astype(BF16),
                  jnp.concatenate([blk(al_abs), blk(akv)], axis=1).astype(BF16))
        a_hat = ta[:, :W]
        u_v = ta[:, W:]
        rb = _dot(a_rb.astype(BF16),
                  jnp.concatenate([blk(a_hat), blk(u_v)], axis=1).astype(BF16))
        r_hat = r_abs + rb[:, :W]
        y0 = rb[:, W:] + _dot(a_rk.astype(BF16), blk(v).astype(BF16))

        state = s_ref[p]
        state_b = state.astype(BF16)
        y = _dot_nt(r_hat.astype(BF16), state_b) + y0
        w_su = _dot_nt(state_b, a_hat.astype(BF16))
        xt = jnp.concatenate([u_v, v], axis=0).T
        zl = jnp.concatenate([w_su + xt[:, :L], xt[:, L:]], axis=1).astype(BF16)
        z = _dot(zl, jnp.concatenate([be_hat, k_hat], axis=0).astype(BF16))
        s_ref[p] = state * p_last + jnp.where(bdiag, z, 0.0)

        mu = segsum(y) * (1.0 / N)
        yc = y - mu
        var = segsum(yc * yc) * (1.0 / N)
        out = yc * lax.rsqrt(var + RWKV_LNX_EPS) * lg_ref[:, sl] + lb_ref[:, sl]
        out = out + segsum(r * kmod * rk_ref[:, sl]) * v
        o_ref[:, sl] = (out * g_ref[:, sl]).astype(BF16)


def _rwkv_scan(r, k, v, a, lw, g, k_k, k_a, r_k, lnx_g, lnx_b, npairs=4):
    T, D = r.shape
    L = RWKV_CHUNK
    wl = npairs * 2 * RWKV_HEAD
    tile = pl.BlockSpec((L, wl), lambda hg, c: (c, hg))
    prow = pl.BlockSpec((1, wl), lambda hg, c: (0, hg))
    row = lambda x: x.reshape(1, D)
    return pl.pallas_call(
        functools.partial(_rwkv_scan_kernel, npairs),
        grid=(D // wl, T // L),
        in_specs=[tile] * 6 + [prow] * 5,
        out_specs=tile,
        out_shape=jax.ShapeDtypeStruct((T, D), BF16),
        scratch_shapes=[pltpu.VMEM((npairs, 2 * RWKV_HEAD, 2 * RWKV_HEAD), F32)],
        compiler_params=_params("parallel", "arbitrary"),
        name="rwkv_scan",
    )(r, k, v, a, lw, g, row(k_k), row(k_a), row(r_k), row(lnx_g), row(lnx_b))


def _rope_table_kernel(pos_ref, cos_ref, sin_ref):
    half = cos_ref.shape[1]
    idx = lax.broadcasted_iota(jnp.int32, (1, half), 1).astype(F32)
    freqs = jnp.exp(idx * (-jnp.log(ROPE_BASE) / half))
    ang = pos_ref[...].astype(F32) * freqs
    cos_ref[...] = jnp.cos(ang)
    sin_ref[...] = jnp.sin(ang)


def _rope_table(positions, half, tm=1024):
    T = positions.shape[0]
    return pl.pallas_call(
        _rope_table_kernel,
        grid=(T // tm,),
        in_specs=[pl.BlockSpec((tm, 1), lambda i: (i, 0))],
        out_specs=[pl.BlockSpec((tm, half), lambda i: (i, 0))] * 2,
        out_shape=[jax.ShapeDtypeStruct((T, half), F32)] * 2,
        compiler_params=_params("parallel"),
        name="rope_table",
    )(positions.reshape(T, 1))


def _retention_kernel(q_ref, k_ref, v_ref, g_ref, cos_ref, sin_ref, lg_ref, ln_ref, o_ref, st_ref):
    L = RET_CHUNK
    dk = q_ref.shape[1]
    half = dk // 2

    @pl.when(pl.program_id(1) == 0)
    def _():
        st_ref[...] = jnp.zeros_like(st_ref)

    cos = cos_ref[...]
    sin = sin_ref[...]

    def rope(x):
        x1 = x[:, :half]
        x2 = x[:, half:]
        return jnp.concatenate([x1 * cos - x2 * sin, x1 * sin + x2 * cos], axis=1)

    log_gamma = lg_ref[0]
    row = lax.broadcasted_iota(jnp.int32, (L, L), 0)
    col = lax.broadcasted_iota(jnp.int32, (L, L), 1)
    rel = (row - col).astype(F32)
    intra = jnp.where(rel >= 0, jnp.exp(jnp.maximum(rel, 0.0) * log_gamma), 0.0)
    tcol = lax.broadcasted_iota(jnp.int32, (L, 1), 0).astype(F32)
    lg1 = log_gamma[:, 0:1]
    xi = jnp.exp((tcol + 1.0) * lg1)
    zeta = jnp.exp((L - 1.0 - tcol) * lg1)
    chunk_decay = jnp.exp(L * lg1)

    q = rope(q_ref[...])
    k = rope(k_ref[...]) * (dk ** -0.5)
    v = v_ref[...]
    qb = q.astype(BF16)
    vb = v.astype(BF16)
    s = _dot_nt(qb, k.astype(BF16)) * intra
    state = st_ref[...]
    o = _dot(s.astype(BF16), vb) + _dot(qb, state.astype(BF16)) * xi
    st_ref[...] = chunk_decay * state + _dot_tn((k * zeta).astype(BF16), vb)

    mu = jnp.mean(o, axis=-1, keepdims=True)
    oc = o - mu
    on = oc * lax.rsqrt(jnp.mean(oc * oc, axis=-1, keepdims=True) + RET_GN_EPS) * ln_ref[...]
    g = g_ref[...]
    o_ref[...] = (g * jax.nn.sigmoid(g) * on).astype(BF16)


def _retention_core(proj, cos, sin, ln_g):
    T = proj.shape[0]
    H = RET_HEADS
    L = RET_CHUNK
    dk = cos.shape[1] * 2
    dv = ln_g.shape[0] // H
    qk_tot = H * dk
    v_tot = H * dv
    kb = qk_tot // dk
    vb = (2 * qk_tot) // dv
    gb = (2 * qk_tot + v_tot) // dv
    hs = jnp.arange(H, dtype=F32)
    log_gamma = jnp.broadcast_to(jnp.log(1.0 - 2.0 ** (-5.0 - hs))[:, None, None], (H, 1, L))
    return pl.pallas_call(
        _retention_kernel,
        grid=(H, T // L),
        in_specs=[
            pl.BlockSpec((L, dk), lambda h, c: (c, h)),
            pl.BlockSpec((L, dk), lambda h, c: (c, kb + h)),
            pl.BlockSpec((L, dv), lambda h, c: (c, vb + h)),
            pl.BlockSpec((L, dv), lambda h, c: (c, gb + h)),
            pl.BlockSpec((L, dk // 2), lambda h, c: (c, 0)),
            pl.BlockSpec((L, dk // 2), lambda h, c: (c, 0)),
            pl.BlockSpec((1, 1, L), lambda h, c: (h, 0, 0)),
            pl.BlockSpec((1, dv), lambda h, c: (0, h)),
        ],
        out_specs=pl.BlockSpec((L, dv), lambda h, c: (c, h)),
        out_shape=jax.ShapeDtypeStruct((T, v_tot), BF16),
        scratch_shapes=[pltpu.VMEM((dk, dv), F32)],
        compiler_params=_params("parallel", "arbitrary"),
        name="retention",
    )(proj, proj, proj, proj, cos, sin, log_gamma, ln_g.reshape(1, v_tot))


def _mlstm_kernel(q_ref, k_ref, v_ref, og_ref, gates_ref, gb_ref, mh_ref, o_ref, c_ref, n_ref, m_ref):
    L = ML_CHUNK
    H = ML_HEADS
    dqk = q_ref.shape[1]
    h = pl.program_id(0)

    @pl.when(pl.program_id(1) == 0)
    def _():
        c_ref[...] = jnp.zeros_like(c_ref)
        n_ref[...] = jnp.zeros_like(n_ref)
        m_ref[...] = jnp.zeros_like(m_ref)

    gates = gates_ref[...] + gb_ref[...]
    glane = lax.broadcasted_iota(jnp.int32, gates.shape, 1)
    gi = jnp.sum(jnp.where(glane == h, gates, 0.0), axis=1, keepdims=True)
    gf = jnp.sum(jnp.where(glane == h + H, gates, 0.0), axis=1, keepdims=True)
    ic = ML_IGATE_CAP * jnp.tanh(gi / ML_IGATE_CAP)
    fc = jnp.minimum(gf, 0.0) - jnp.log1p(jnp.exp(-jnp.abs(gf)))

    row = lax.broadcasted_iota(jnp.int32, (L, L), 0)
    col = lax.broadcasted_iota(jnp.int32, (L, L), 1)
    causal = col <= row
    diag = col == row
    f_mat = jnp.broadcast_to(fc, (L, L))
    i_row = jnp.sum(jnp.where(diag, jnp.broadcast_to(ic, (L, L)), 0.0), axis=0, keepdims=True)
    f_row = jnp.sum(jnp.where(diag, f_mat, 0.0), axis=0, keepdims=True)
    b_col = jnp.sum(jnp.where(causal, jnp.broadcast_to(f_row, (L, L)), 0.0), axis=1, keepdims=True)
    b_row = jnp.sum(jnp.where(row <= col, f_mat, 0.0), axis=0, keepdims=True)
    b_last = b_row[:, L - 1:L]

    m_prev = m_ref[...]
    log_inter = b_col + m_prev
    log_intra = jnp.where(causal, b_col - b_row + i_row, -jnp.inf)
    m_t = jnp.maximum(log_inter, jnp.max(log_intra, axis=1, keepdims=True))
    w_inter = jnp.exp(log_inter - m_t)
    w_intra = jnp.exp(log_intra - m_t)

    q = q_ref[...]
    k = k_ref[...] * (dqk ** -0.5)
    v = v_ref[...]
    qb = q.astype(BF16)
    vb = v.astype(BF16)
    s = _dot_nt(qb, k.astype(BF16)) * w_intra
    c_state = c_ref[...]
    n_state = n_ref[...]
    num = _dot(s.astype(BF16), vb) + w_inter * _dot(qb, c_state.astype(BF16))
    den = jnp.sum(s, axis=1, keepdims=True) + w_inter * jnp.sum(q * n_state, axis=1, keepdims=True)
    hid = num / jnp.maximum(jnp.abs(den), jnp.exp(-m_t))

    log_s = b_last - b_col + ic
    m_new = jnp.maximum(b_last + m_prev, jnp.max(log_s, axis=0, keepdims=True))
    ws = jnp.exp(log_s - m_new)
    carry = jnp.exp(b_last + m_prev - m_new)
    kw = k * ws
    c_ref[...] = carry * c_state + _dot_tn(kw.astype(BF16), vb)
    n_ref[...] = carry * n_state + jnp.sum(kw, axis=0, keepdims=True)
    m_ref[...] = m_new

    hn = hid * lax.rsqrt(jnp.mean(hid * hid, axis=-1, keepdims=True) + ML_NORM_EPS) * mh_ref[...]
    o_ref[...] = (jax.nn.sigmoid(og_ref[...]) * hn).astype(BF16)


def _mlstm_core(proj, gates, gate_b, mh_g):
    T = proj.shape[0]
    H = ML_HEADS
    L = ML_CHUNK
    v_tot = mh_g.shape[0]
    dv = v_tot // H
    qk_tot = (proj.shape[1] - 2 * v_tot) // 2
    dqk = qk_tot // H
    kb = qk_tot // dqk
    vb = (2 * qk_tot) // dv
    ob = (2 * qk_tot + v_tot) // dv
    return pl.pallas_call(
        _mlstm_kernel,
        grid=(H, T // L),
        in_specs=[
            pl.BlockSpec((L, dqk), lambda h, c: (c, h)),
            pl.BlockSpec((L, dqk), lambda h, c: (c, kb + h)),
            pl.BlockSpec((L, dv), lambda h, c: (c, vb + h)),
            pl.BlockSpec((L, dv), lambda h, c: (c, ob + h)),
            pl.BlockSpec((L, 2 * H), lambda h, c: (c, 0)),
            pl.BlockSpec((1, 2 * H), lambda h, c: (0, 0)),
            pl.BlockSpec((1, dv), lambda h, c: (0, h)),
        ],
        out_specs=pl.BlockSpec((L, dv), lambda h, c: (c, h)),
        out_shape=jax.ShapeDtypeStruct((T, v_tot), BF16),
        scratch_shapes=[pltpu.VMEM((dqk, dv), F32), pltpu.VMEM((1, dqk), F32), pltpu.VMEM((1, 1), F32)],
        compiler_params=_params("parallel", "arbitrary"),
        name="mlstm",
    )(proj, proj, proj, proj, gates, gate_b.reshape(1, 2 * H), mh_g.reshape(1, v_tot))


def _gate_proj_kernel(x_ref, g_ref, w_ref, o_ref):
    o_ref[...] = _dot(_rms(x_ref[...], g_ref[...]).astype(BF16), w_ref[...])


def _gate_proj(h, gain, w, tm=512):
    T, D = h.shape
    n = w.shape[1]
    return pl.pallas_call(
        _gate_proj_kernel,
        grid=(T // tm,),
        in_specs=[
            pl.BlockSpec((tm, D), lambda i: (i, 0)),
            pl.BlockSpec((1, D), lambda i: (0, 0)),
            pl.BlockSpec((D, n), lambda i: (0, 0)),
        ],
        out_specs=pl.BlockSpec((tm, n), lambda i: (i, 0)),
        out_shape=jax.ShapeDtypeStruct((T, n), F32),
        compiler_params=_params("parallel"),
        name="gate_proj",
    )(h, gain.reshape(1, D), w)


def kernel(x, p, positions, norm_g, final_g, ffn_in, ffn_out, ple_proj, ple_gate, rwkv_lerp, rwkv_w0, rwkv_w1, rwkv_w2, rwkv_a0, rwkv_a1, rwkv_a2, rwkv_g1, rwkv_g2, rwkv_kk, rwkv_ka, rwkv_rk, rwkv_w_rkv, rwkv_w_o, rwkv_lnx_g, rwkv_lnx_b, rwkv_v0, rwkv_v1, rwkv_v2, ret_w_in, ret_ln_g, ret_w_o, ml_w_in, ml_gate_b, ml_mh_g, ml_w_o):
    B, T, D = x.shape
    depth = norm_g.shape[0]
    bf = lambda w: w.astype(BF16)
    outs = []
    for b in range(B):
        h = x[b]
        v_first = None
        for i in range(depth):
            kind, j = i % 3, i // 3
            h = _ffn(h, norm_g[i, 0], bf(ffn_in[i, 0]), bf(ffn_out[i, 0]))
            if kind == 0:
                v_res = None
                if j > 0:
                    v_res = (rwkv_v0[j - 1], bf(rwkv_v1[j - 1]), bf(rwkv_v2[j - 1]))
                r, k, v, a, lw, g = _rwkv_proj(
                    h, norm_g[i, 1], rwkv_lerp[j], bf(rwkv_w_rkv[j]), bf(rwkv_w1[j]), bf(rwkv_w2[j]),
                    bf(rwkv_a1[j]), bf(rwkv_a2[j]), bf(rwkv_g1[j]), bf(rwkv_g2[j]),
                    rwkv_w0[j], rwkv_a0[j], v_res, v_first)
                if j == 0:
                    v_first = v
                mix_in = _rwkv_scan(r, k, v, a, lw, g, rwkv_kk[j], rwkv_ka[j], rwkv_rk[j].reshape(D),
                                    rwkv_lnx_g[j], rwkv_lnx_b[j])
                w_o = rwkv_w_o[j]
            elif kind == 1:
                proj = _norm_matmul(h, norm_g[i, 1], bf(ret_w_in[j]))
                half = ret_w_in.shape[2] // 6 // RET_HEADS // 2
                cos, sin = _rope_table(positions[b], half)
                mix_in = _retention_core(proj, cos, sin, ret_ln_g[j])
                w_o = ret_w_o[j]
            else:
                n_main = ml_w_in.shape[2] - 2 * ML_HEADS
                proj = _norm_matmul(h, norm_g[i, 1], bf(ml_w_in[j][:, :n_main]))
                gates = _gate_proj(h, norm_g[i, 1], bf(ml_w_in[j][:, n_main:]))
                mix_in = _mlstm_core(proj, gates, ml_gate_b[j], ml_mh_g[j])
                w_o = ml_w_o[j]
            h = _matmul_residual(h, mix_in, bf(w_o))
            h = _ffn(h, norm_g[i, 2], bf(ffn_in[i, 1]), bf(ffn_out[i, 1]))
            h = _ple(h, norm_g[i, 3], p[i, b], bf(ple_proj[i]), bf(ple_gate[i]), final_g, i == depth - 1)
        outs.append(h)
    return jnp.stack(outs, axis=0)
```

```python
import functools

import jax
import jax.numpy as jnp
from jax import lax
from jax.experimental import pallas as pl
from jax.experimental.pallas import tpu as pltpu

F32 = jnp.float32
BF16 = jnp.bfloat16

NORM_EPS = 1e-6
RWKV_HEAD = 64
RWKV_LNX_EPS = 64e-5
RWKV_CHUNK = 64
RET_HEADS = 8
RET_CHUNK = 128
RET_GN_EPS = 1e-6
ROPE_BASE = 10000.0
ML_HEADS = 8
ML_CHUNK = 64
ML_IGATE_CAP = 15.0
ML_NORM_EPS = 1e-6

LANES = 128
VMEM_LIMIT_BYTES = 56 * 1024 * 1024
HI = lax.Precision.HIGHEST


def _params(*sem):
    return pltpu.CompilerParams(dimension_semantics=sem, vmem_limit_bytes=VMEM_LIMIT_BYTES)


def _rms(x, gain):
    return x * lax.rsqrt(jnp.mean(x * x, axis=-1, keepdims=True) + NORM_EPS) * gain


def _dot(a, b):
    return jnp.dot(a, b, preferred_element_type=F32)


def _dot_nt(a, b):
    return lax.dot_general(a, b, (((1,), (1,)), ((), ())), preferred_element_type=F32)


def _dot_tn(a, b):
    return lax.dot_general(a, b, (((0,), (0,)), ((), ())), preferred_element_type=F32)


def _ffn_kernel(nf, x_ref, g_ref, wg_ref, wu_ref, wo_ref, o_ref, xn_ref):
    f = pl.program_id(1)

    @pl.when(f == 0)
    def _():
        xn_ref[...] = _rms(x_ref[...], g_ref[...]).astype(BF16)
        o_ref[...] = jnp.zeros_like(o_ref)

    xn = xn_ref[...]
    gate = _dot(xn, wg_ref[...])
    up = _dot(xn, wu_ref[...])
    act = (gate * jax.nn.sigmoid(gate) * up).astype(BF16)
    o_ref[...] += _dot(act, wo_ref[...])

    @pl.when(f == nf - 1)
    def _():
        o_ref[...] = x_ref[...] + 0.5 * o_ref[...]


def _ffn(h, gain, w_in, w_out, tm=1024, tf=512):
    T, D = h.shape
    F = w_out.shape[0]
    nf = F // tf
    return pl.pallas_call(
        functools.partial(_ffn_kernel, nf),
        grid=(T // tm, nf),
        in_specs=[
            pl.BlockSpec((tm, D), lambda i, f: (i, 0), pipeline_mode=pl.Buffered(1)),
            pl.BlockSpec((1, D), lambda i, f: (0, 0)),
            pl.BlockSpec((D, tf), lambda i, f: (0, f)),
            pl.BlockSpec((D, tf), lambda i, f: (0, nf + f)),
            pl.BlockSpec((tf, D), lambda i, f: (f, 0)),
        ],
        out_specs=pl.BlockSpec((tm, D), lambda i, f: (i, 0)),
        out_shape=jax.ShapeDtypeStruct((T, D), F32),
        scratch_shapes=[pltpu.VMEM((tm, D), BF16)],
        compiler_params=_params("parallel", "arbitrary"),
        name="ffn",
    )(h, gain.reshape(1, D), w_in, w_in, w_out)


def _norm_matmul_kernel(x_ref, g_ref, w_ref, o_ref, xn_ref):
    @pl.when(pl.program_id(1) == 0)
    def _():
        xn_ref[...] = _rms(x_ref[...], g_ref[...]).astype(BF16)

    o_ref[...] = _dot(xn_ref[...], w_ref[...])


def _norm_matmul(h, gain, w, tm=512, tn=1024):
    T, D = h.shape
    N = w.shape[1]
    return pl.pallas_call(
        _norm_matmul_kernel,
        grid=(T // tm, N // tn),
        in_specs=[
            pl.BlockSpec((tm, D), lambda i, j: (i, 0)),
            pl.BlockSpec((1, D), lambda i, j: (0, 0)),
            pl.BlockSpec((D, tn), lambda i, j: (0, j)),
        ],
        out_specs=pl.BlockSpec((tm, tn), lambda i, j: (i, j)),
        out_shape=jax.ShapeDtypeStruct((T, N), F32),
        scratch_shapes=[pltpu.VMEM((tm, D), BF16)],
        compiler_params=_params("parallel", "arbitrary"),
        name="norm_matmul",
    )(h, gain.reshape(1, D), w)


def _matmul_residual_kernel(h_ref, a_ref, w_ref, o_ref):
    o_ref[...] = h_ref[...] + _dot(a_ref[...], w_ref[...])


def _matmul_residual(h, a, w, tm=512, tn=1024):
    T, D = h.shape
    K = a.shape[1]
    return pl.pallas_call(
        _matmul_residual_kernel,
        grid=(D // tn, T // tm),
        in_specs=[
            pl.BlockSpec((tm, tn), lambda j, i: (i, j)),
            pl.BlockSpec((tm, K), lambda j, i: (i, 0)),
            pl.BlockSpec((K, tn), lambda j, i: (0, j)),
        ],
        out_specs=pl.BlockSpec((tm, tn), lambda j, i: (i, j)),
        out_shape=jax.ShapeDtypeStruct((T, D), F32),
        compiler_params=_params("parallel", "parallel"),
        name="matmul_residual",
    )(h, a, w)


def _ple_kernel(final, h_ref, g_ref, p_ref, wp_ref, wg_ref, fg_ref, o_ref):
    h = h_ref[...]
    gate = jax.nn.sigmoid(_dot(_rms(h, g_ref[...]).astype(BF16), wg_ref[...]))
    out = h + _dot(p_ref[...].astype(BF16), wp_ref[...]) * gate
    if final:
        out = _rms(out, fg_ref[...])
    o_ref[...] = out


def _ple(h, gain, p, w_proj, w_gate, final_gain, final, tm=512):
    T, D = h.shape
    P = p.shape[1]
    return pl.pallas_call(
        functools.partial(_ple_kernel, final),
        grid=(T // tm,),
        in_specs=[
            pl.BlockSpec((tm, D), lambda i: (i, 0)),
            pl.BlockSpec((1, D), lambda i: (0, 0)),
            pl.BlockSpec((tm, P), lambda i: (i, 0)),
            pl.BlockSpec((P, D), lambda i: (0, 0)),
            pl.BlockSpec((D, D), lambda i: (0, 0)),
            pl.BlockSpec((1, D), lambda i: (0, 0)),
        ],
        out_specs=pl.BlockSpec((tm, D), lambda i: (i, 0)),
        out_shape=jax.ShapeDtypeStruct((T, D), F32),
        compiler_params=_params("parallel"),
        name="ple",
    )(h, gain.reshape(1, D), p, w_proj, w_gate, final_gain.reshape(1, D))


def _rwkv_proj_kernel(has_vres, *refs):
    if has_vres:
        (h_ref, halo_ref, g_ref, lerp_ref, wr_ref, wk_ref, wv_ref, w1_ref, w2_ref, a1_ref, a2_ref,
         g1_ref, g2_ref, w0_ref, a0_ref, v1_ref, v2_ref, v0_ref, vf_ref,
         r_out, k_out, v_out, a_out, lw_out, g_out,
         xr_s, xk_s, xv_s, hw_s, ha_s, hg_s, hv_s) = refs
    else:
        (h_ref, halo_ref, g_ref, lerp_ref, wr_ref, wk_ref, wv_ref, w1_ref, w2_ref, a1_ref, a2_ref,
         g1_ref, g2_ref, w0_ref, a0_ref,
         r_out, k_out, v_out, a_out, lw_out, g_out,
         xr_s, xk_s, xv_s, hw_s, ha_s, hg_s) = refs
    i = pl.program_id(0)

    @pl.when(pl.program_id(1) == 0)
    def _():
        gain = g_ref[...]
        u = _rms(h_ref[...], gain)
        halo = _rms(halo_ref[...], gain)
        first = jnp.where(i > 0, halo[7:8, :], 0.0)
        row = lax.broadcasted_iota(jnp.int32, u.shape, 0)
        u_prev = jnp.where(row == 0, first, pltpu.roll(u, 1, 0))
        xx = u_prev - u
        lerp = lerp_ref[...]
        xr_s[...] = (u + xx * lerp[0:1]).astype(BF16)
        xw = (u + xx * lerp[1:2]).astype(BF16)
        xk_s[...] = (u + xx * lerp[2:3]).astype(BF16)
        xv = (u + xx * lerp[3:4]).astype(BF16)
        xv_s[...] = xv
        xa = (u + xx * lerp[4:5]).astype(BF16)
        xg = (u + xx * lerp[5:6]).astype(BF16)
        hw_s[...] = jnp.tanh(_dot(xw, w1_ref[...])).astype(BF16)
        ha_s[...] = _dot(xa, a1_ref[...]).astype(BF16)
        hg_s[...] = jax.nn.sigmoid(_dot(xg, g1_ref[...])).astype(BF16)
        if has_vres:
            hv_s[...] = _dot(xv, v1_ref[...]).astype(BF16)

    r_out[...] = _dot(xr_s[...], wr_ref[...])
    k_out[...] = _dot(xk_s[...], wk_ref[...])
    v = _dot(xv_s[...], wv_ref[...])
    if has_vres:
        mix = jax.nn.sigmoid(v0_ref[...] + _dot(hv_s[...], v2_ref[...]))
        v = v + (vf_ref[...] - v) * mix
    v_out[...] = v
    z = -(w0_ref[...] + _dot(hw_s[...], w2_ref[...]))
    softplus = jnp.maximum(z, 0.0) + jnp.log1p(jnp.exp(-jnp.abs(z)))
    lw_out[...] = -jnp.exp(-softplus - 0.5)
    a_out[...] = jax.nn.sigmoid(a0_ref[...] + _dot(ha_s[...], a2_ref[...]))
    g_out[...] = _dot(hg_s[...], g2_ref[...])


def _pad_cols(w, n):
    return jnp.pad(w, ((0, 0), (0, n - w.shape[1])))


def _pad_rows(w, n):
    return jnp.pad(w, ((0, n - w.shape[0]), (0, 0)))


def _rwkv_proj(h, gain, lerp, w_rkv, w1, w2, a1, a2, g1, g2, w0, a0, v_res, v_first, tm=512, tn=512):
    T, D = h.shape
    has_vres = v_res is not None
    lo = LANES
    row = lambda x: x.reshape(1, D)
    full = lambda shape: pl.BlockSpec(shape, lambda i, j: (0, 0))
    col = lambda k: pl.BlockSpec((k, tn), lambda i, j: (0, j))
    tile = pl.BlockSpec((tm, tn), lambda i, j: (i, j))
    rowtile = pl.BlockSpec((1, tn), lambda i, j: (0, j))
    hb = tm // 8
    gd = g1.shape[1]
    args = [h, h, row(gain), _pad_rows(lerp, 8),
            w_rkv[0], w_rkv[1], w_rkv[2],
            _pad_cols(w1, lo), _pad_rows(w2, lo), _pad_cols(a1, lo), _pad_rows(a2, lo),
            g1, g2, row(w0), row(a0)]
    in_specs = [
        pl.BlockSpec((tm, D), lambda i, j: (i, 0)),
        pl.BlockSpec((8, D), lambda i, j: (jnp.maximum(i * hb - 1, 0), 0)),
        full((1, D)), full((8, D)),
        col(D), col(D), col(D),
        full((D, lo)), col(lo), full((D, lo)), col(lo),
        full((D, gd)), col(gd), rowtile, rowtile,
    ]
    scratch = [pltpu.VMEM((tm, D), BF16)] * 3 + [
        pltpu.VMEM((tm, lo), BF16), pltpu.VMEM((tm, lo), BF16), pltpu.VMEM((tm, gd), BF16)]
    if has_vres:
        v0, v1, v2 = v_res
        args += [_pad_cols(v1, lo), _pad_rows(v2, lo), row(v0), v_first]
        in_specs += [full((D, lo)), col(lo), rowtile, tile]
        scratch += [pltpu.VMEM((tm, lo), BF16)]
    return pl.pallas_call(
        functools.partial(_rwkv_proj_kernel, has_vres),
        grid=(T // tm, D // tn),
        in_specs=in_specs,
        out_specs=[tile] * 6,
        out_shape=[jax.ShapeDtypeStruct((T, D), F32)] * 6,
        scratch_shapes=scratch,
        compiler_params=_params("parallel", "arbitrary"),
        name="rwkv_proj",
    )(*args)


def _rwkv_scan_kernel(npairs, r_ref, k_ref, v_ref, a_ref, lw_ref, g_ref,
                      kk_ref, ka_ref, rk_ref, lg_ref, lb_ref, o_ref, s_ref):
    L = RWKV_CHUNK
    N = RWKV_HEAD
    W = 2 * N

    @pl.when(pl.program_id(1) == 0)
    def _():
        s_ref[...] = jnp.zeros_like(s_ref)

    t_i = lax.broadcasted_iota(jnp.int32, (L, W), 0)
    lane = lax.broadcasted_iota(jnp.int32, (L, W), 1)
    s_i = lane % N
    head0 = lane < N
    strict = s_i < t_i
    incl = s_i <= t_i
    same16 = (t_i // 16) == (s_i // 16)
    same32 = (t_i // 32) == (s_i // 32)
    eye = jnp.where(s_i == t_i, 1.0, 0.0)
    brow = lax.broadcasted_iota(jnp.int32, (W, W), 0)
    blane = lax.broadcasted_iota(jnp.int32, (W, W), 1)
    bdiag = (brow // N) == (blane // N)

    def blk(x):
        return jnp.where(bdiag, jnp.concatenate([x, x], axis=0), 0.0)

    def segsum(x):
        s0 = jnp.sum(jnp.where(head0, x, 0.0), axis=1, keepdims=True)
        s1 = jnp.sum(jnp.where(head0, 0.0, x), axis=1, keepdims=True)
        return jnp.where(head0, s0, s1)

    def blk_b(x):
        return blk(x).astype(BF16)

    def split(x):
        hi = x.astype(BF16)
        return hi, (x - hi.astype(F32)).astype(BF16)

    def mm_base(x, y):
        xh, xl = split(x)
        yh, yl = split(blk(y))
        return _dot(jnp.concatenate([xh, xh, xl], axis=1), jnp.concatenate([yh, yl, yh], axis=0))

    def mm(x, y):
        return _dot(x.astype(BF16), blk_b(y))

    tri_r = lax.broadcasted_iota(jnp.int32, (L, L), 0)
    tri_c = lax.broadcasted_iota(jnp.int32, (L, L), 1)
    tri = jnp.where(tri_c <= tri_r, 1.0, 0.0)
    c_all = jnp.dot(tri, lw_ref[...], precision=HI, preferred_element_type=F32)

    pairs = range(npairs)
    sls = [slice(p * W, (p + 1) * W) for p in pairs]

    def prep(sl):
        r = r_ref[:, sl]
        k = k_ref[:, sl]
        a = a_ref[:, sl]
        lw = lw_ref[:, sl]
        c = c_all[:, sl]
        kk = k * kk_ref[:, sl]
        kk = kk / jnp.maximum(jnp.sqrt(segsum(kk * kk)), 1e-12)
        kmod = k * (1.0 + (a - 1.0) * ka_ref[:, sl])
        alpha = -kk
        beta = kk * a
        c_mid = c[L // 2 - 1:L // 2, :]
        c_last = c[L - 1:L, :]
        e = c - c_mid
        ex_m = jnp.exp(-e)
        lhs = jnp.concatenate([alpha * jnp.exp(e - lw), r * jnp.exp(e)], axis=0).astype(BF16)
        rhs = jnp.concatenate([blk_b(beta * ex_m), blk_b(kmod * ex_m)], axis=0)
        to_end = jnp.exp(c_last - c)
        upd = jnp.concatenate([beta * to_end, kmod * to_end], axis=0).astype(BF16)
        return dict(lhs=lhs, rhs=rhs, upd=upd, r_abs=r * jnp.exp(c), al_abs=alpha * jnp.exp(c - lw),
                    p_last=jnp.exp(c_last), bonus=segsum(r * kmod * rk_ref[:, sl]))

    pre = [prep(sl) for sl in sls]
    amat = [_dot_nt(q["lhs"], q["rhs"]) for q in pre]
    n_ab = [jnp.where(strict, m[:L, :W], 0.0) for m in amat]
    a_ak = [jnp.where(strict, m[:L, W:], 0.0).astype(BF16) for m in amat]
    a_rb = [jnp.where(incl, m[L:, :W], 0.0).astype(BF16) for m in amat]
    a_rk = [jnp.where(incl, m[L:, W:], 0.0).astype(BF16) for m in amat]
    v_blk = [blk_b(v_ref[:, sl]) for sl in sls]
    akv = [_dot(x, y) for x, y in zip(a_ak, v_blk)]

    nd = [jnp.where(same16, n, 0.0) for n in n_ab]
    tinv = [eye + n for n in nd]
    pw = [mm_base(n, n) for n in nd]
    for _ in range(2):
        both = [mm_base(jnp.concatenate([t, q], axis=0), q) for t, q in zip(tinv, pw)]
        tinv = [t + b[:L] for t, b in zip(tinv, both)]
        pw = [b[L:] for b in both]
    tinv = [t + mm_base(t, q) for t, q in zip(tinv, pw)]
    for off_mask in (same32 & jnp.logical_not(same16), jnp.logical_not(same32)):
        inner = [mm(jnp.where(off_mask, n, 0.0), t) for n, t in zip(n_ab, tinv)]
        tinv = [t + mm(t, x) for t, x in zip(tinv, inner)]

    ta = [_dot(t.astype(BF16), jnp.concatenate([blk_b(q["al_abs"]), blk_b(x)], axis=1))
          for t, q, x in zip(tinv, pre, akv)]
    rb = [_dot(x, jnp.concatenate([blk_b(t[:, :W]), blk_b(t[:, W:])], axis=1)) for x, t in zip(a_rb, ta)]
    y0 = [b[:, W:] + _dot(x, vb) for b, x, vb in zip(rb, a_rk, v_blk)]
    r_hat = [(q["r_abs"] + b[:, :W]).astype(BF16) for q, b in zip(pre, rb)]

    states = [s_ref[p] for p in pairs]
    states_b = [s.astype(BF16) for s in states]
    ys = [_dot_nt(x, s) + y for x, s, y in zip(r_hat, states_b, y0)]
    w_su = [_dot_nt(s, t[:, :W].astype(BF16)) for s, t in zip(states_b, ta)]
    xt = [jnp.concatenate([t[:, W:], v_ref[:, sl]], axis=0).T for t, sl in zip(ta, sls)]
    zl = [jnp.concatenate([w + x[:, :L], x[:, L:]], axis=1).astype(BF16) for w, x in zip(w_su, xt)]
    z = [_dot(x, q["upd"]) for x, q in zip(zl, pre)]
    for p in pairs:
        s_ref[p] = states[p] * pre[p]["p_last"] + jnp.where(bdiag, z[p], 0.0)

    for p in pairs:
        sl = sls[p]
        y = ys[p]
        mu = segsum(y) * (1.0 / N)
        yc = y - mu
        var = segsum(yc * yc) * (1.0 / N)
        out = yc * lax.rsqrt(var + RWKV_LNX_EPS) * lg_ref[:, sl] + lb_ref[:, sl]
        out = out + pre[p]["bonus"] * v_ref[:, sl]
        o_ref[:, sl] = (out * g_ref[:, sl]).astype(BF16)


def _rwkv_scan(r, k, v, a, lw, g, k_k, k_a, r_k, lnx_g, lnx_b, npairs=8):
    T, D = r.shape
    L = RWKV_CHUNK
    wl = npairs * 2 * RWKV_HEAD
    tile = pl.BlockSpec((L, wl), lambda hg, c: (c, hg))
    prow = pl.BlockSpec((1, wl), lambda hg, c: (0, hg))
    row = lambda x: x.reshape(1, D)
    return pl.pallas_call(
        functools.partial(_rwkv_scan_kernel, npairs),
        grid=(D // wl, T // L),
        in_specs=[tile] * 6 + [prow] * 5,
        out_specs=tile,
        out_shape=jax.ShapeDtypeStruct((T, D), BF16),
        scratch_shapes=[pltpu.VMEM((npairs, 2 * RWKV_HEAD, 2 * RWKV_HEAD), F32)],
        compiler_params=_params("parallel", "arbitrary"),
        name="rwkv_scan",
    )(r, k, v, a, lw, g, row(k_k), row(k_a), row(r_k), row(lnx_g), row(lnx_b))


def _rope_table_kernel(pos_ref, cos_ref, sin_ref):
    half = cos_ref.shape[1]
    idx = lax.broadcasted_iota(jnp.int32, (1, half), 1).astype(F32)
    freqs = jnp.exp(idx * (-jnp.log(ROPE_BASE) / half))
    ang = pos_ref[...].astype(F32) * freqs
    cos_ref[...] = jnp.cos(ang)
    sin_ref[...] = jnp.sin(ang)


def _rope_table(positions, half, tm=1024):
    T = positions.shape[0]
    return pl.pallas_call(
        _rope_table_kernel,
        grid=(T // tm,),
        in_specs=[pl.BlockSpec((tm, 1), lambda i: (i, 0))],
        out_specs=[pl.BlockSpec((tm, half), lambda i: (i, 0))] * 2,
        out_shape=[jax.ShapeDtypeStruct((T, half), F32)] * 2,
        compiler_params=_params("parallel"),
        name="rope_table",
    )(positions.reshape(T, 1))


def _retention_kernel(q_ref, k_ref, v_ref, g_ref, cos_ref, sin_ref, lg_ref, ln_ref, o_ref, st_ref):
    L = RET_CHUNK
    H = RET_HEADS
    dk = q_ref.shape[1] // H
    dv = v_ref.shape[1] // H
    half = dk // 2

    @pl.when(pl.program_id(0) == 0)
    def _():
        st_ref[...] = jnp.zeros_like(st_ref)

    cos = cos_ref[...]
    sin = sin_ref[...]

    def rope(ref, h):
        x1 = ref[:, h * dk:h * dk + half]
        x2 = ref[:, h * dk + half:(h + 1) * dk]
        return jnp.concatenate([x1 * cos - x2 * sin, x1 * sin + x2 * cos], axis=1)

    row = lax.broadcasted_iota(jnp.int32, (L, L), 0)
    col = lax.broadcasted_iota(jnp.int32, (L, L), 1)
    rel = (row - col).astype(F32)
    tcol = lax.broadcasted_iota(jnp.int32, (L, 1), 0).astype(F32)

    heads = range(H)
    lgs = [lg_ref[h] for h in heads]
    intra = [jnp.where(rel >= 0, jnp.exp(jnp.maximum(rel, 0.0) * lg), 0.0) for lg in lgs]
    lg1 = [lg[:, 0:1] for lg in lgs]
    qb = [rope(q_ref, h).astype(BF16) for h in heads]
    ks = [rope(k_ref, h) * (dk ** -0.5) for h in heads]
    vb = [v_ref[:, h * dv:(h + 1) * dv].astype(BF16) for h in heads]
    s = [(_dot_nt(q, k.astype(BF16)) * m).astype(BF16) for q, k, m in zip(qb, ks, intra)]
    states = [st_ref[h] for h in heads]
    o = [_dot(x, v) + _dot(q, st.astype(BF16)) * jnp.exp((tcol + 1.0) * lg)
         for x, v, q, st, lg in zip(s, vb, qb, states, lg1)]
    kz = [(k * jnp.exp((L - 1.0 - tcol) * lg)).astype(BF16) for k, lg in zip(ks, lg1)]
    upd = [_dot_tn(k, v) for k, v in zip(kz, vb)]
    for h in heads:
        st_ref[h] = jnp.exp(L * lg1[h]) * states[h] + upd[h]

    for h in heads:
        sl = slice(h * dv, (h + 1) * dv)
        mu = jnp.mean(o[h], axis=-1, keepdims=True)
        oc = o[h] - mu
        on = oc * lax.rsqrt(jnp.mean(oc * oc, axis=-1, keepdims=True) + RET_GN_EPS) * ln_ref[:, sl]
        g = g_ref[:, sl]
        o_ref[:, sl] = (g * jax.nn.sigmoid(g) * on).astype(BF16)


def _retention_core(proj, cos, sin, ln_g):
    T = proj.shape[0]
    H = RET_HEADS
    L = RET_CHUNK
    dk = cos.shape[1] * 2
    v_tot = ln_g.shape[0]
    dv = v_tot // H
    qk_tot = H * dk
    hs = jnp.arange(H, dtype=F32)
    log_gamma = jnp.broadcast_to(jnp.log(1.0 - 2.0 ** (-5.0 - hs))[:, None, None], (H, 1, L))
    return pl.pallas_call(
        _retention_kernel,
        grid=(T // L,),
        in_specs=[
            pl.BlockSpec((L, qk_tot), lambda c: (c, 0)),
            pl.BlockSpec((L, qk_tot), lambda c: (c, 1)),
            pl.BlockSpec((L, v_tot), lambda c: (c, (2 * qk_tot) // v_tot)),
            pl.BlockSpec((L, v_tot), lambda c: (c, (2 * qk_tot) // v_tot + 1)),
            pl.BlockSpec((L, dk // 2), lambda c: (c, 0)),
            pl.BlockSpec((L, dk // 2), lambda c: (c, 0)),
            pl.BlockSpec((H, 1, L), lambda c: (0, 0, 0)),
            pl.BlockSpec((1, v_tot), lambda c: (0, 0)),
        ],
        out_specs=pl.BlockSpec((L, v_tot), lambda c: (c, 0)),
        out_shape=jax.ShapeDtypeStruct((T, v_tot), BF16),
        scratch_shapes=[pltpu.VMEM((H, dk, dv), F32)],
        compiler_params=_params("arbitrary"),
        name="retention",
    )(proj, proj, proj, proj, cos, sin, log_gamma, ln_g.reshape(1, v_tot))


def _mlstm_kernel(q_ref, k_ref, v_ref, og_ref, gates_ref, gb_ref, mh_ref, o_ref, c_ref, n_ref, m_ref):
    L = ML_CHUNK
    H = ML_HEADS
    dqk = q_ref.shape[1] // H
    dv = v_ref.shape[1] // H

    @pl.when(pl.program_id(0) == 0)
    def _():
        c_ref[...] = jnp.zeros_like(c_ref)
        n_ref[...] = jnp.zeros_like(n_ref)
        m_ref[...] = jnp.zeros_like(m_ref)

    gates = gates_ref[...] + gb_ref[...]
    glane = lax.broadcasted_iota(jnp.int32, gates.shape, 1)
    row = lax.broadcasted_iota(jnp.int32, (L, L), 0)
    col = lax.broadcasted_iota(jnp.int32, (L, L), 1)
    causal = col <= row
    diag = col == row

    def gate_terms(h):
        gi = jnp.sum(jnp.where(glane == h, gates, 0.0), axis=1, keepdims=True)
        gf = jnp.sum(jnp.where(glane == h + H, gates, 0.0), axis=1, keepdims=True)
        ic = ML_IGATE_CAP * jnp.tanh(gi / ML_IGATE_CAP)
        fc = jnp.minimum(gf, 0.0) - jnp.log1p(jnp.exp(-jnp.abs(gf)))
        f_mat = jnp.broadcast_to(fc, (L, L))
        i_row = jnp.sum(jnp.where(diag, jnp.broadcast_to(ic, (L, L)), 0.0), axis=0, keepdims=True)
        f_row = jnp.sum(jnp.where(diag, f_mat, 0.0), axis=0, keepdims=True)
        b_col = jnp.sum(jnp.where(causal, jnp.broadcast_to(f_row, (L, L)), 0.0), axis=1, keepdims=True)
        b_row = jnp.sum(jnp.where(row <= col, f_mat, 0.0), axis=0, keepdims=True)
        b_last = b_row[:, L - 1:L]
        m_prev = m_ref[h]
        log_inter = b_col + m_prev
        log_intra = jnp.where(causal, b_col - b_row + i_row, -jnp.inf)
        m_t = jnp.maximum(log_inter, jnp.max(log_intra, axis=1, keepdims=True))
        log_s = b_last - b_col + ic
        m_new = jnp.maximum(b_last + m_prev, jnp.max(log_s, axis=0, keepdims=True))
        return dict(m_t=m_t, w_inter=jnp.exp(log_inter - m_t), w_intra=jnp.exp(log_intra - m_t),
                    m_new=m_new, ws=jnp.exp(log_s - m_new), carry=jnp.exp(b_last + m_prev - m_new))

    heads = range(H)
    gt = [gate_terms(h) for h in heads]
    qs = [q_ref[:, h * dqk:(h + 1) * dqk] for h in heads]
    ks = [k_ref[:, h * dqk:(h + 1) * dqk] * (dqk ** -0.5) for h in heads]
    qb = [q.astype(BF16) for q in qs]
    vb = [v_ref[:, h * dv:(h + 1) * dv].astype(BF16) for h in heads]
    s = [_dot_nt(q, k.astype(BF16)) * t["w_intra"] for q, k, t in zip(qb, ks, gt)]
    c_state = [c_ref[h] for h in heads]
    n_state = [n_ref[h] for h in heads]
    num = [_dot(x.astype(BF16), v) + t["w_inter"] * _dot(q, c.astype(BF16))
           for x, v, t, q, c in zip(s, vb, gt, qb, c_state)]
    den = [jnp.sum(x, axis=1, keepdims=True) + t["w_inter"] * jnp.sum(q * n, axis=1, keepdims=True)
           for x, t, q, n in zip(s, gt, qs, n_state)]
    kw = [k * t["ws"] for k, t in zip(ks, gt)]
    upd = [_dot_tn(k.astype(BF16), v) for k, v in zip(kw, vb)]
    for h in heads:
        c_ref[h] = gt[h]["carry"] * c_state[h] + upd[h]
        n_ref[h] = gt[h]["carry"] * n_state[h] + jnp.sum(kw[h], axis=0, keepdims=True)
        m_ref[h] = gt[h]["m_new"]

    for h in heads:
        sl = slice(h * dv, (h + 1) * dv)
        hid = num[h] / jnp.maximum(jnp.abs(den[h]), jnp.exp(-gt[h]["m_t"]))
        hn = hid * lax.rsqrt(jnp.mean(hid * hid, axis=-1, keepdims=True) + ML_NORM_EPS) * mh_ref[:, sl]
        o_ref[:, sl] = (jax.nn.sigmoid(og_ref[:, sl]) * hn).astype(BF16)


def _mlstm_core(proj, gates, gate_b, mh_g):
    T = proj.shape[0]
    H = ML_HEADS
    L = ML_CHUNK
    v_tot = mh_g.shape[0]
    dv = v_tot // H
    qk_tot = (proj.shape[1] - 2 * v_tot) // 2
    dqk = qk_tot // H
    return pl.pallas_call(
        _mlstm_kernel,
        grid=(T // L,),
        in_specs=[
            pl.BlockSpec((L, qk_tot), lambda c: (c, 0)),
            pl.BlockSpec((L, qk_tot), lambda c: (c, 1)),
            pl.BlockSpec((L, v_tot), lambda c: (c, (2 * qk_tot) // v_tot)),
            pl.BlockSpec((L, v_tot), lambda c: (c, (2 * qk_tot) // v_tot + 1)),
            pl.BlockSpec((L, 2 * H), lambda c: (c, 0)),
            pl.BlockSpec((1, 2 * H), lambda c: (0, 0)),
            pl.BlockSpec((1, v_tot), lambda c: (0, 0)),
        ],
        out_specs=pl.BlockSpec((L, v_tot), lambda c: (c, 0)),
        out_shape=jax.ShapeDtypeStruct((T, v_tot), BF16),
        scratch_shapes=[pltpu.VMEM((H, dqk, dv), F32), pltpu.VMEM((H, 1, dqk), F32),
                        pltpu.VMEM((H, 1, 1), F32)],
        compiler_params=_params("arbitrary"),
        name="mlstm",
    )(proj, proj, proj, proj, gates, gate_b.reshape(1, 2 * H), mh_g.reshape(1, v_tot))


def _gate_proj_kernel(x_ref, g_ref, w_ref, o_ref):
    o_ref[...] = _dot(_rms(x_ref[...], g_ref[...]).astype(BF16), w_ref[...])


def _gate_proj(h, gain, w, tm=512):
    T, D = h.shape
    n = w.shape[1]
    return pl.pallas_call(
        _gate_proj_kernel,
        grid=(T // tm,),
        in_specs=[
            pl.BlockSpec((tm, D), lambda i: (i, 0)),
            pl.BlockSpec((1, D), lambda i: (0, 0)),
            pl.BlockSpec((D, n), lambda i: (0, 0)),
        ],
        out_specs=pl.BlockSpec((tm, n), lambda i: (i, 0)),
        out_shape=jax.ShapeDtypeStruct((T, n), F32),
        compiler_params=_params("parallel"),
        name="gate_proj",
    )(h, gain.reshape(1, D), w)


def kernel(x, p, positions, norm_g, final_g, ffn_in, ffn_out, ple_proj, ple_gate, rwkv_lerp, rwkv_w0, rwkv_w1, rwkv_w2, rwkv_a0, rwkv_a1, rwkv_a2, rwkv_g1, rwkv_g2, rwkv_kk, rwkv_ka, rwkv_rk, rwkv_w_rkv, rwkv_w_o, rwkv_lnx_g, rwkv_lnx_b, rwkv_v0, rwkv_v1, rwkv_v2, ret_w_in, ret_ln_g, ret_w_o, ml_w_in, ml_gate_b, ml_mh_g, ml_w_o):
    B, T, D = x.shape
    depth = norm_g.shape[0]
    bf = lambda w: w.astype(BF16)
    outs = []
    for b in range(B):
        h = x[b]
        v_first = None
        for i in range(depth):
            kind, j = i % 3, i // 3
            h = _ffn(h, norm_g[i, 0], bf(ffn_in[i, 0]), bf(ffn_out[i, 0]))
            if kind == 0:
                v_res = None
                if j > 0:
                    v_res = (rwkv_v0[j - 1], bf(rwkv_v1[j - 1]), bf(rwkv_v2[j - 1]))
                r, k, v, a, lw, g = _rwkv_proj(
                    h, norm_g[i, 1], rwkv_lerp[j], bf(rwkv_w_rkv[j]), bf(rwkv_w1[j]), bf(rwkv_w2[j]),
                    bf(rwkv_a1[j]), bf(rwkv_a2[j]), bf(rwkv_g1[j]), bf(rwkv_g2[j]),
                    rwkv_w0[j], rwkv_a0[j], v_res, v_first)
                if j == 0:
                    v_first = v
                mix_in = _rwkv_scan(r, k, v, a, lw, g, rwkv_kk[j], rwkv_ka[j], rwkv_rk[j].reshape(D),
                                    rwkv_lnx_g[j], rwkv_lnx_b[j])
                w_o = rwkv_w_o[j]
            elif kind == 1:
                proj = _norm_matmul(h, norm_g[i, 1], bf(ret_w_in[j]))
                half = ret_w_in.shape[2] // 6 // RET_HEADS // 2
                cos, sin = _rope_table(positions[b], half)
                mix_in = _retention_core(proj, cos, sin, ret_ln_g[j])
                w_o = ret_w_o[j]
            else:
                n_main = ml_w_in.shape[2] - 2 * ML_HEADS
                proj = _norm_matmul(h, norm_g[i, 1], bf(ml_w_in[j][:, :n_main]))
                gates = _gate_proj(h, norm_g[i, 1], bf(ml_w_in[j][:, n_main:]))
                mix_in = _mlstm_core(proj, gates, ml_gate_b[j], ml_mh_g[j])
                w_o = ml_w_o[j]
            h = _matmul_residual(h, mix_in, bf(w_o))
            h = _ffn(h, norm_g[i, 2], bf(ffn_in[i, 1]), bf(ffn_out[i, 1]))
            h = _ple(h, norm_g[i, 3], p[i, b], bf(ple_proj[i]), bf(ple_gate[i]), final_g, i == depth - 1)
        outs.append(h)
    return jnp.stack(outs, axis=0)
```

```python
import functools

import jax
import jax.numpy as jnp
from jax import lax
from jax.experimental import pallas as pl
from jax.experimental.pallas import tpu as pltpu

F32 = jnp.float32
BF16 = jnp.bfloat16

NORM_EPS = 1e-6
RWKV_HEAD = 64
RWKV_LNX_EPS = 64e-5
RWKV_CHUNK = 64
RET_HEADS = 8
RET_CHUNK = 128
RET_GN_EPS = 1e-6
ROPE_BASE = 10000.0
ML_HEADS = 8
ML_CHUNK = 64
ML_IGATE_CAP = 15.0
ML_NORM_EPS = 1e-6

LANES = 128
VMEM_LIMIT_BYTES = 56 * 1024 * 1024
HI = lax.Precision.HIGHEST


def _params(*sem):
    return pltpu.CompilerParams(dimension_semantics=sem, vmem_limit_bytes=VMEM_LIMIT_BYTES)


def _rms(x, gain):
    return x * lax.rsqrt(jnp.mean(x * x, axis=-1, keepdims=True) + NORM_EPS) * gain


def _wspec(block, lead, tail):
    return pl.BlockSpec((None,) * len(lead) + tuple(block), lambda *g: tuple(lead) + tuple(tail(*g)))


def _dot(a, b):
    return jnp.dot(a, b, preferred_element_type=F32)


def _dot_nt(a, b):
    return lax.dot_general(a, b, (((1,), (1,)), ((), ())), preferred_element_type=F32)


def _dot_tn(a, b):
    return lax.dot_general(a, b, (((0,), (0,)), ((), ())), preferred_element_type=F32)


def _ffn_kernel(nf, x_ref, g_ref, wg_ref, wu_ref, wo_ref, o_ref, xn_ref):
    f = pl.program_id(1)

    @pl.when(f == 0)
    def _():
        xn_ref[...] = _rms(x_ref[...], g_ref[...]).astype(BF16)
        o_ref[...] = jnp.zeros_like(o_ref)

    xn = xn_ref[...]
    gate = _dot(xn, wg_ref[...])
    up = _dot(xn, wu_ref[...])
    act = (gate * jax.nn.sigmoid(gate) * up).astype(BF16)
    o_ref[...] += _dot(act, wo_ref[...])

    @pl.when(f == nf - 1)
    def _():
        o_ref[...] = x_ref[...] + 0.5 * o_ref[...]


def _ffn(h, gain, w_in, w_out, lead, tm=1024, tf=512):
    T, D = h.shape
    F = w_out.shape[-2]
    nf = F // tf
    return pl.pallas_call(
        functools.partial(_ffn_kernel, nf),
        grid=(T // tm, nf),
        in_specs=[
            pl.BlockSpec((tm, D), lambda i, f: (i, 0)),
            pl.BlockSpec((1, D), lambda i, f: (0, 0)),
            _wspec((D, tf), lead, lambda i, f: (0, f)),
            _wspec((D, tf), lead, lambda i, f: (0, nf + f)),
            _wspec((tf, D), lead, lambda i, f: (f, 0)),
        ],
        out_specs=pl.BlockSpec((tm, D), lambda i, f: (i, 0)),
        out_shape=jax.ShapeDtypeStruct((T, D), F32),
        scratch_shapes=[pltpu.VMEM((tm, D), BF16)],
        compiler_params=_params("parallel", "arbitrary"),
        name="ffn",
    )(h, gain.reshape(1, D), w_in, w_in, w_out)


def _norm_matmul_kernel(x_ref, g_ref, w_ref, o_ref, xn_ref):
    @pl.when(pl.program_id(1) == 0)
    def _():
        xn_ref[...] = _rms(x_ref[...], g_ref[...]).astype(BF16)

    o_ref[...] = _dot(xn_ref[...], w_ref[...])


def _norm_matmul(h, gain, w, lead, N, tm=1024, tn=1024):
    T, D = h.shape
    return pl.pallas_call(
        _norm_matmul_kernel,
        grid=(T // tm, N // tn),
        in_specs=[
            pl.BlockSpec((tm, D), lambda i, j: (i, 0)),
            pl.BlockSpec((1, D), lambda i, j: (0, 0)),
            _wspec((D, tn), lead, lambda i, j: (0, j)),
        ],
        out_specs=pl.BlockSpec((tm, tn), lambda i, j: (i, j)),
        out_shape=jax.ShapeDtypeStruct((T, N), F32),
        scratch_shapes=[pltpu.VMEM((tm, D), BF16)],
        compiler_params=_params("parallel", "arbitrary"),
        name="norm_matmul",
    )(h, gain.reshape(1, D), w)


def _matmul_residual_kernel(h_ref, a_ref, w_ref, o_ref):
    o_ref[...] = h_ref[...] + _dot(a_ref[...], w_ref[...])


def _matmul_residual(h, a, w, lead, tm=512, tn=1024):
    T, D = h.shape
    K = a.shape[1]
    return pl.pallas_call(
        _matmul_residual_kernel,
        grid=(D // tn, T // tm),
        in_specs=[
            pl.BlockSpec((tm, tn), lambda j, i: (i, j)),
            pl.BlockSpec((tm, K), lambda j, i: (i, 0)),
            _wspec((K, tn), lead, lambda j, i: (0, j)),
        ],
        out_specs=pl.BlockSpec((tm, tn), lambda j, i: (i, j)),
        out_shape=jax.ShapeDtypeStruct((T, D), F32),
        compiler_params=_params("parallel", "parallel"),
        name="matmul_residual",
    )(h, a, w)


def _ple_kernel(final, h_ref, g_ref, p_ref, wp_ref, wg_ref, fg_ref, o_ref):
    h = h_ref[...]
    gate = jax.nn.sigmoid(_dot(_rms(h, g_ref[...]).astype(BF16), wg_ref[...]))
    out = h + _dot(p_ref[...].astype(BF16), wp_ref[...]) * gate
    if final:
        out = _rms(out, fg_ref[...])
    o_ref[...] = out


def _ple(h, gain, p, w_proj, w_gate, lead, final_gain, final, tm=512):
    T, D = h.shape
    P = p.shape[1]
    return pl.pallas_call(
        functools.partial(_ple_kernel, final),
        grid=(T // tm,),
        in_specs=[
            pl.BlockSpec((tm, D), lambda i: (i, 0)),
            pl.BlockSpec((1, D), lambda i: (0, 0)),
            pl.BlockSpec((tm, P), lambda i: (i, 0)),
            _wspec((P, D), lead, lambda i: (0, 0)),
            _wspec((D, D), lead, lambda i: (0, 0)),
            pl.BlockSpec((1, D), lambda i: (0, 0)),
        ],
        out_specs=pl.BlockSpec((tm, D), lambda i: (i, 0)),
        out_shape=jax.ShapeDtypeStruct((T, D), F32),
        compiler_params=_params("parallel"),
        name="ple",
    )(h, gain.reshape(1, D), p, w_proj, w_gate, final_gain.reshape(1, D))


def _rwkv_proj_kernel(has_vres, *refs):
    if has_vres:
        (h_ref, halo_ref, g_ref, lerp_ref, wr_ref, wk_ref, wv_ref, w1_ref, w2_ref, a1_ref, a2_ref,
         g1_ref, g2_ref, w0_ref, a0_ref, v1_ref, v2_ref, v0_ref, vf_ref,
         r_out, k_out, v_out, a_out, lw_out, g_out,
         xr_s, xk_s, xv_s, hw_s, ha_s, hg_s, hv_s) = refs
    else:
        (h_ref, halo_ref, g_ref, lerp_ref, wr_ref, wk_ref, wv_ref, w1_ref, w2_ref, a1_ref, a2_ref,
         g1_ref, g2_ref, w0_ref, a0_ref,
         r_out, k_out, v_out, a_out, lw_out, g_out,
         xr_s, xk_s, xv_s, hw_s, ha_s, hg_s) = refs
    i = pl.program_id(0)

    @pl.when(pl.program_id(1) == 0)
    def _():
        gain = g_ref[...]
        u = _rms(h_ref[...], gain)
        halo = _rms(halo_ref[...], gain)
        first = jnp.where(i > 0, halo[7:8, :], 0.0)
        row = lax.broadcasted_iota(jnp.int32, u.shape, 0)
        u_prev = jnp.where(row == 0, first, pltpu.roll(u, 1, 0))
        xx = u_prev - u
        lerp = lerp_ref[...]
        xr_s[...] = (u + xx * lerp[0:1]).astype(BF16)
        xw = (u + xx * lerp[1:2]).astype(BF16)
        xk_s[...] = (u + xx * lerp[2:3]).astype(BF16)
        xv = (u + xx * lerp[3:4]).astype(BF16)
        xv_s[...] = xv
        xa = (u + xx * lerp[4:5]).astype(BF16)
        xg = (u + xx * lerp[5:6]).astype(BF16)
        hw_s[...] = jnp.tanh(_dot(xw, w1_ref[...])).astype(BF16)
        ha_s[...] = _dot(xa, a1_ref[...]).astype(BF16)
        hg_s[...] = jax.nn.sigmoid(_dot(xg, g1_ref[...])).astype(BF16)
        if has_vres:
            hv_s[...] = _dot(xv, v1_ref[...]).astype(BF16)

    r_out[...] = _dot(xr_s[...], wr_ref[...])
    k_out[...] = _dot(xk_s[...], wk_ref[...])
    v = _dot(xv_s[...], wv_ref[...])
    if has_vres:
        mix = jax.nn.sigmoid(v0_ref[...] + _dot(hv_s[...], v2_ref[...]))
        v = v + (vf_ref[...] - v) * mix
    v_out[...] = v
    z = -(w0_ref[...] + _dot(hw_s[...], w2_ref[...]))
    softplus = jnp.maximum(z, 0.0) + jnp.log1p(jnp.exp(-jnp.abs(z)))
    lw_out[...] = -jnp.exp(-softplus - 0.5)
    a_out[...] = jax.nn.sigmoid(a0_ref[...] + _dot(ha_s[...], a2_ref[...]))
    g_out[...] = _dot(hg_s[...], g2_ref[...])


def _pad_cols(w, n):
    return jnp.pad(w, ((0, 0), (0, n - w.shape[1])))


def _pad_rows(w, n):
    return jnp.pad(w, ((0, n - w.shape[0]), (0, 0)))


def _rwkv_proj(h, gain, lerp, w_rkv, lead, w1, w2, a1, a2, g1, g2, w0, a0, v_res, v_first, tm=512, tn=512):
    T, D = h.shape
    has_vres = v_res is not None
    lo = LANES
    row = lambda x: x.reshape(1, D)
    full = lambda shape: pl.BlockSpec(shape, lambda i, j: (0, 0))
    col = lambda k: pl.BlockSpec((k, tn), lambda i, j: (0, j))
    tile = pl.BlockSpec((tm, tn), lambda i, j: (i, j))
    rowtile = pl.BlockSpec((1, tn), lambda i, j: (0, j))
    hb = tm // 8
    gd = g1.shape[1]
    args = [h, h, row(gain), _pad_rows(lerp, 8),
            w_rkv, w_rkv, w_rkv,
            _pad_cols(w1, lo), _pad_rows(w2, lo), _pad_cols(a1, lo), _pad_rows(a2, lo),
            g1, g2, row(w0), row(a0)]
    in_specs = [
        pl.BlockSpec((tm, D), lambda i, j: (i, 0)),
        pl.BlockSpec((8, D), lambda i, j: (jnp.maximum(i * hb - 1, 0), 0)),
        full((1, D)), full((8, D)),
        _wspec((D, tn), lead + (0,), lambda i, j: (0, j)),
        _wspec((D, tn), lead + (1,), lambda i, j: (0, j)),
        _wspec((D, tn), lead + (2,), lambda i, j: (0, j)),
        full((D, lo)), col(lo), full((D, lo)), col(lo),
        full((D, gd)), col(gd), rowtile, rowtile,
    ]
    scratch = [pltpu.VMEM((tm, D), BF16)] * 3 + [
        pltpu.VMEM((tm, lo), BF16), pltpu.VMEM((tm, lo), BF16), pltpu.VMEM((tm, gd), BF16)]
    if has_vres:
        v0, v1, v2 = v_res
        args += [_pad_cols(v1, lo), _pad_rows(v2, lo), row(v0), v_first]
        in_specs += [full((D, lo)), col(lo), rowtile, tile]
        scratch += [pltpu.VMEM((tm, lo), BF16)]
    return pl.pallas_call(
        functools.partial(_rwkv_proj_kernel, has_vres),
        grid=(T // tm, D // tn),
        in_specs=in_specs,
        out_specs=[tile] * 6,
        out_shape=[jax.ShapeDtypeStruct((T, D), F32)] * 6,
        scratch_shapes=scratch,
        compiler_params=_params("parallel", "arbitrary"),
        name="rwkv_proj",
    )(*args)


def _rwkv_scan_kernel(npairs, r_ref, k_ref, v_ref, a_ref, lw_ref, g_ref,
                      kk_ref, ka_ref, rk_ref, lg_ref, lb_ref, o_ref, s_ref):
    L = RWKV_CHUNK
    N = RWKV_HEAD
    W = 2 * N

    @pl.when(pl.program_id(1) == 0)
    def _():
        s_ref[...] = jnp.zeros_like(s_ref)

    t_i = lax.broadcasted_iota(jnp.int32, (L, W), 0)
    lane = lax.broadcasted_iota(jnp.int32, (L, W), 1)
    s_i = lane % N
    head0 = lane < N
    strict = s_i < t_i
    incl = s_i <= t_i
    same16 = (t_i // 16) == (s_i // 16)
    same32 = (t_i // 32) == (s_i // 32)
    eye = jnp.where(s_i == t_i, 1.0, 0.0)
    brow = lax.broadcasted_iota(jnp.int32, (W, W), 0)
    blane = lax.broadcasted_iota(jnp.int32, (W, W), 1)
    bdiag = (brow // N) == (blane // N)

    def blk(x):
        return jnp.where(bdiag, jnp.concatenate([x, x], axis=0), 0.0)

    def segsum(x):
        s0 = jnp.sum(jnp.where(head0, x, 0.0), axis=1, keepdims=True)
        s1 = jnp.sum(jnp.where(head0, 0.0, x), axis=1, keepdims=True)
        return jnp.where(head0, s0, s1)

    def blk_b(x):
        return blk(x).astype(BF16)

    def split(x):
        hi = x.astype(BF16)
        return hi, (x - hi.astype(F32)).astype(BF16)

    def mm_base(x, y):
        xh, xl = split(x)
        yh, yl = split(blk(y))
        return _dot(jnp.concatenate([xh, xh, xl], axis=1), jnp.concatenate([yh, yl, yh], axis=0))

    def mm(x, y):
        return _dot(x.astype(BF16), blk_b(y))

    tri_r = lax.broadcasted_iota(jnp.int32, (L, L), 0)
    tri_c = lax.broadcasted_iota(jnp.int32, (L, L), 1)
    tri = jnp.where(tri_c <= tri_r, 1.0, 0.0)
    c_all = jnp.dot(tri, lw_ref[...], precision=HI, preferred_element_type=F32)

    pairs = range(npairs)
    sls = [slice(p * W, (p + 1) * W) for p in pairs]

    def prep(sl):
        r = r_ref[:, sl]
        k = k_ref[:, sl]
        a = a_ref[:, sl]
        lw = lw_ref[:, sl]
        c = c_all[:, sl]
        kk = k * kk_ref[:, sl]
        kk = kk / jnp.maximum(jnp.sqrt(segsum(kk * kk)), 1e-12)
        kmod = k * (1.0 + (a - 1.0) * ka_ref[:, sl])
        alpha = -kk
        beta = kk * a
        c_mid = c[L // 2 - 1:L // 2, :]
        c_last = c[L - 1:L, :]
        e = c - c_mid
        ex_m = jnp.exp(-e)
        lhs = jnp.concatenate([alpha * jnp.exp(e - lw), r * jnp.exp(e)], axis=0).astype(BF16)
        rhs = jnp.concatenate([blk_b(beta * ex_m), blk_b(kmod * ex_m)], axis=0)
        to_end = jnp.exp(c_last - c)
        upd = jnp.concatenate([beta * to_end, kmod * to_end], axis=0).astype(BF16)
        return dict(lhs=lhs, rhs=rhs, upd=upd, r_abs=r * jnp.exp(c), al_abs=alpha * jnp.exp(c - lw),
                    p_last=jnp.exp(c_last), bonus=segsum(r * kmod * rk_ref[:, sl]))

    pre = [prep(sl) for sl in sls]
    amat = [_dot_nt(q["lhs"], q["rhs"]) for q in pre]
    n_ab = [jnp.where(strict, m[:L, :W], 0.0) for m in amat]
    a_ak = [jnp.where(strict, m[:L, W:], 0.0).astype(BF16) for m in amat]
    a_rb = [jnp.where(incl, m[L:, :W], 0.0).astype(BF16) for m in amat]
    a_rk = [jnp.where(incl, m[L:, W:], 0.0).astype(BF16) for m in amat]
    v_blk = [blk_b(v_ref[:, sl]) for sl in sls]
    akv = [_dot(x, y) for x, y in zip(a_ak, v_blk)]

    nd = [jnp.where(same16, n, 0.0) for n in n_ab]
    tinv = [eye + n for n in nd]
    pw = [mm_base(n, n) for n in nd]
    for _ in range(2):
        both = [mm_base(jnp.concatenate([t, q], axis=0), q) for t, q in zip(tinv, pw)]
        tinv = [t + b[:L] for t, b in zip(tinv, both)]
        pw = [b[L:] for b in both]
    tinv = [t + mm_base(t, q) for t, q in zip(tinv, pw)]
    for off_mask in (same32 & jnp.logical_not(same16), jnp.logical_not(same32)):
        inner = [mm(jnp.where(off_mask, n, 0.0), t) for n, t in zip(n_ab, tinv)]
        tinv = [t + mm(t, x) for t, x in zip(tinv, inner)]

    ta = [_dot(t.astype(BF16), jnp.concatenate([blk_b(q["al_abs"]), blk_b(x)], axis=1))
          for t, q, x in zip(tinv, pre, akv)]
    rb = [_dot(x, jnp.concatenate([blk_b(t[:, :W]), blk_b(t[:, W:])], axis=1)) for x, t in zip(a_rb, ta)]
    y0 = [b[:, W:] + _dot(x, vb) for b, x, vb in zip(rb, a_rk, v_blk)]
    r_hat = [(q["r_abs"] + b[:, :W]).astype(BF16) for q, b in zip(pre, rb)]

    states = [s_ref[p] for p in pairs]
    states_b = [s.astype(BF16) for s in states]
    ys = [_dot_nt(x, s) + y for x, s, y in zip(r_hat, states_b, y0)]
    w_su = [_dot_nt(s, t[:, :W].astype(BF16)) for s, t in zip(states_b, ta)]
    xt = [jnp.concatenate([t[:, W:], v_ref[:, sl]], axis=0).T for t, sl in zip(ta, sls)]
    zl = [jnp.concatenate([w + x[:, :L], x[:, L:]], axis=1).astype(BF16) for w, x in zip(w_su, xt)]
    z = [_dot(x, q["upd"]) for x, q in zip(zl, pre)]
    for p in pairs:
        s_ref[p] = states[p] * pre[p]["p_last"] + jnp.where(bdiag, z[p], 0.0)

    for p in pairs:
        sl = sls[p]
        y = ys[p]
        mu = segsum(y) * (1.0 / N)
        yc = y - mu
        var = segsum(yc * yc) * (1.0 / N)
        out = yc * lax.rsqrt(var + RWKV_LNX_EPS) * lg_ref[:, sl] + lb_ref[:, sl]
        out = out + pre[p]["bonus"] * v_ref[:, sl]
        o_ref[:, sl] = (out * g_ref[:, sl]).astype(BF16)


def _rwkv_scan(r, k, v, a, lw, g, k_k, k_a, r_k, lnx_g, lnx_b, npairs=8):
    T, D = r.shape
    L = RWKV_CHUNK
    wl = npairs * 2 * RWKV_HEAD
    tile = pl.BlockSpec((L, wl), lambda hg, c: (c, hg))
    prow = pl.BlockSpec((1, wl), lambda hg, c: (0, hg))
    row = lambda x: x.reshape(1, D)
    return pl.pallas_call(
        functools.partial(_rwkv_scan_kernel, npairs),
        grid=(D // wl, T // L),
        in_specs=[tile] * 6 + [prow] * 5,
        out_specs=tile,
        out_shape=jax.ShapeDtypeStruct((T, D), BF16),
        scratch_shapes=[pltpu.VMEM((npairs, 2 * RWKV_HEAD, 2 * RWKV_HEAD), F32)],
        compiler_params=_params("parallel", "arbitrary"),
        name="rwkv_scan",
    )(r, k, v, a, lw, g, row(k_k), row(k_a), row(r_k), row(lnx_g), row(lnx_b))


def _rope_table_kernel(pos_ref, cos_ref, sin_ref):
    half = cos_ref.shape[1]
    idx = lax.broadcasted_iota(jnp.int32, (1, half), 1).astype(F32)
    freqs = jnp.exp(idx * (-jnp.log(ROPE_BASE) / half))
    ang = pos_ref[...].astype(F32) * freqs
    cos_ref[...] = jnp.cos(ang)
    sin_ref[...] = jnp.sin(ang)


def _rope_table(positions, half, tm=1024):
    T = positions.shape[0]
    return pl.pallas_call(
        _rope_table_kernel,
        grid=(T // tm,),
        in_specs=[pl.BlockSpec((tm, 1), lambda i: (i, 0))],
        out_specs=[pl.BlockSpec((tm, half), lambda i: (i, 0))] * 2,
        out_shape=[jax.ShapeDtypeStruct((T, half), F32)] * 2,
        compiler_params=_params("parallel"),
        name="rope_table",
    )(positions.reshape(T, 1))


def _retention_kernel(q_ref, k_ref, v_ref, g_ref, cos_ref, sin_ref, lg_ref, ln_ref, o_ref, st_ref):
    L = RET_CHUNK
    H = RET_HEADS
    dk = q_ref.shape[1] // H
    dv = v_ref.shape[1] // H
    half = dk // 2

    @pl.when(pl.program_id(0) == 0)
    def _():
        st_ref[...] = jnp.zeros_like(st_ref)

    cos = cos_ref[...]
    sin = sin_ref[...]

    def rope(ref, h):
        x1 = ref[:, h * dk:h * dk + half]
        x2 = ref[:, h * dk + half:(h + 1) * dk]
        return jnp.concatenate([x1 * cos - x2 * sin, x1 * sin + x2 * cos], axis=1)

    row = lax.broadcasted_iota(jnp.int32, (L, L), 0)
    col = lax.broadcasted_iota(jnp.int32, (L, L), 1)
    rel = (row - col).astype(F32)
    tcol = lax.broadcasted_iota(jnp.int32, (L, 1), 0).astype(F32)

    heads = range(H)
    lgs = [lg_ref[h] for h in heads]
    intra = [jnp.where(rel >= 0, jnp.exp(jnp.maximum(rel, 0.0) * lg), 0.0) for lg in lgs]
    lg1 = [lg[:, 0:1] for lg in lgs]
    qb = [rope(q_ref, h).astype(BF16) for h in heads]
    ks = [rope(k_ref, h) * (dk ** -0.5) for h in heads]
    vb = [v_ref[:, h * dv:(h + 1) * dv].astype(BF16) for h in heads]
    s = [(_dot_nt(q, k.astype(BF16)) * m).astype(BF16) for q, k, m in zip(qb, ks, intra)]
    states = [st_ref[h] for h in heads]
    o = [_dot(x, v) + _dot(q, st.astype(BF16)) * jnp.exp((tcol + 1.0) * lg)
         for x, v, q, st, lg in zip(s, vb, qb, states, lg1)]
    kz = [(k * jnp.exp((L - 1.0 - tcol) * lg)).astype(BF16) for k, lg in zip(ks, lg1)]
    upd = [_dot_tn(k, v) for k, v in zip(kz, vb)]
    for h in heads:
        st_ref[h] = jnp.exp(L * lg1[h]) * states[h] + upd[h]

    for h in heads:
        sl = slice(h * dv, (h + 1) * dv)
        mu = jnp.mean(o[h], axis=-1, keepdims=True)
        oc = o[h] - mu
        on = oc * lax.rsqrt(jnp.mean(oc * oc, axis=-1, keepdims=True) + RET_GN_EPS) * ln_ref[:, sl]
        g = g_ref[:, sl]
        o_ref[:, sl] = (g * jax.nn.sigmoid(g) * on).astype(BF16)


def _retention_core(proj, cos, sin, ln_g):
    T = proj.shape[0]
    H = RET_HEADS
    L = RET_CHUNK
    dk = cos.shape[1] * 2
    v_tot = ln_g.shape[0]
    dv = v_tot // H
    qk_tot = H * dk
    hs = jnp.arange(H, dtype=F32)
    log_gamma = jnp.broadcast_to(jnp.log(1.0 - 2.0 ** (-5.0 - hs))[:, None, None], (H, 1, L))
    return pl.pallas_call(
        _retention_kernel,
        grid=(T // L,),
        in_specs=[
            pl.BlockSpec((L, qk_tot), lambda c: (c, 0)),
            pl.BlockSpec((L, qk_tot), lambda c: (c, 1)),
            pl.BlockSpec((L, v_tot), lambda c: (c, (2 * qk_tot) // v_tot)),
            pl.BlockSpec((L, v_tot), lambda c: (c, (2 * qk_tot) // v_tot + 1)),
            pl.BlockSpec((L, dk // 2), lambda c: (c, 0)),
            pl.BlockSpec((L, dk // 2), lambda c: (c, 0)),
            pl.BlockSpec((H, 1, L), lambda c: (0, 0, 0)),
            pl.BlockSpec((1, v_tot), lambda c: (0, 0)),
        ],
        out_specs=pl.BlockSpec((L, v_tot), lambda c: (c, 0)),
        out_shape=jax.ShapeDtypeStruct((T, v_tot), BF16),
        scratch_shapes=[pltpu.VMEM((H, dk, dv), F32)],
        compiler_params=_params("arbitrary"),
        name="retention",
    )(proj, proj, proj, proj, cos, sin, log_gamma, ln_g.reshape(1, v_tot))


def _mlstm_kernel(q_ref, k_ref, v_ref, og_ref, gates_ref, gb_ref, mh_ref, o_ref, c_ref, n_ref, m_ref):
    L = ML_CHUNK
    H = ML_HEADS
    dqk = q_ref.shape[1] // H
    dv = v_ref.shape[1] // H

    @pl.when(pl.program_id(0) == 0)
    def _():
        c_ref[...] = jnp.zeros_like(c_ref)
        n_ref[...] = jnp.zeros_like(n_ref)
        m_ref[...] = jnp.zeros_like(m_ref)

    gates = gates_ref[...] + gb_ref[...]
    glane = lax.broadcasted_iota(jnp.int32, gates.shape, 1)
    row = lax.broadcasted_iota(jnp.int32, (L, L), 0)
    col = lax.broadcasted_iota(jnp.int32, (L, L), 1)
    causal = col <= row
    diag = col == row

    def gate_terms(h):
        gi = jnp.sum(jnp.where(glane == h, gates, 0.0), axis=1, keepdims=True)
        gf = jnp.sum(jnp.where(glane == h + H, gates, 0.0), axis=1, keepdims=True)
        ic = ML_IGATE_CAP * jnp.tanh(gi / ML_IGATE_CAP)
        fc = jnp.minimum(gf, 0.0) - jnp.log1p(jnp.exp(-jnp.abs(gf)))
        f_mat = jnp.broadcast_to(fc, (L, L))
        i_row = jnp.sum(jnp.where(diag, jnp.broadcast_to(ic, (L, L)), 0.0), axis=0, keepdims=True)
        f_row = jnp.sum(jnp.where(diag, f_mat, 0.0), axis=0, keepdims=True)
        b_col = jnp.sum(jnp.where(causal, jnp.broadcast_to(f_row, (L, L)), 0.0), axis=1, keepdims=True)
        b_row = jnp.sum(jnp.where(row <= col, f_mat, 0.0), axis=0, keepdims=True)
        b_last = b_row[:, L - 1:L]
        m_prev = m_ref[h]
        log_inter = b_col + m_prev
        log_intra = jnp.where(causal, b_col - b_row + i_row, -jnp.inf)
        m_t = jnp.maximum(log_inter, jnp.max(log_intra, axis=1, keepdims=True))
        log_s = b_last - b_col + ic
        m_new = jnp.maximum(b_last + m_prev, jnp.max(log_s, axis=0, keepdims=True))
        return dict(m_t=m_t, w_inter=jnp.exp(log_inter - m_t), w_intra=jnp.exp(log_intra - m_t),
                    m_new=m_new, ws=jnp.exp(log_s - m_new), carry=jnp.exp(b_last + m_prev - m_new))

    heads = range(H)
    gt = [gate_terms(h) for h in heads]
    qs = [q_ref[:, h * dqk:(h + 1) * dqk] for h in heads]
    ks = [k_ref[:, h * dqk:(h + 1) * dqk] * (dqk ** -0.5) for h in heads]
    qb = [q.astype(BF16) for q in qs]
    vb = [v_ref[:, h * dv:(h + 1) * dv].astype(BF16) for h in heads]
    s = [_dot_nt(q, k.astype(BF16)) * t["w_intra"] for q, k, t in zip(qb, ks, gt)]
    c_state = [c_ref[h] for h in heads]
    n_state = [n_ref[h] for h in heads]
    num = [_dot(x.astype(BF16), v) + t["w_inter"] * _dot(q, c.astype(BF16))
           for x, v, t, q, c in zip(s, vb, gt, qb, c_state)]
    den = [jnp.sum(x, axis=1, keepdims=True) + t["w_inter"] * jnp.sum(q * n, axis=1, keepdims=True)
           for x, t, q, n in zip(s, gt, qs, n_state)]
    kw = [k * t["ws"] for k, t in zip(ks, gt)]
    upd = [_dot_tn(k.astype(BF16), v) for k, v in zip(kw, vb)]
    for h in heads:
        c_ref[h] = gt[h]["carry"] * c_state[h] + upd[h]
        n_ref[h] = gt[h]["carry"] * n_state[h] + jnp.sum(kw[h], axis=0, keepdims=True)
        m_ref[h] = gt[h]["m_new"]

    for h in heads:
        sl = slice(h * dv, (h + 1) * dv)
        hid = num[h] / jnp.maximum(jnp.abs(den[h]), jnp.exp(-gt[h]["m_t"]))
        hn = hid * lax.rsqrt(jnp.mean(hid * hid, axis=-1, keepdims=True) + ML_NORM_EPS) * mh_ref[:, sl]
        o_ref[:, sl] = (jax.nn.sigmoid(og_ref[:, sl]) * hn).astype(BF16)


def _mlstm_core(proj, gates, gate_b, mh_g):
    T = proj.shape[0]
    H = ML_HEADS
    L = ML_CHUNK
    v_tot = mh_g.shape[0]
    dv = v_tot // H
    qk_tot = (proj.shape[1] - 2 * v_tot) // 2
    dqk = qk_tot // H
    return pl.pallas_call(
        _mlstm_kernel,
        grid=(T // L,),
        in_specs=[
            pl.BlockSpec((L, qk_tot), lambda c: (c, 0)),
            pl.BlockSpec((L, qk_tot), lambda c: (c, 1)),
            pl.BlockSpec((L, v_tot), lambda c: (c, (2 * qk_tot) // v_tot)),
            pl.BlockSpec((L, v_tot), lambda c: (c, (2 * qk_tot) // v_tot + 1)),
            pl.BlockSpec((L, 2 * H), lambda c: (c, 0)),
            pl.BlockSpec((1, 2 * H), lambda c: (0, 0)),
            pl.BlockSpec((1, v_tot), lambda c: (0, 0)),
        ],
        out_specs=pl.BlockSpec((L, v_tot), lambda c: (c, 0)),
        out_shape=jax.ShapeDtypeStruct((T, v_tot), BF16),
        scratch_shapes=[pltpu.VMEM((H, dqk, dv), F32), pltpu.VMEM((H, 1, dqk), F32),
                        pltpu.VMEM((H, 1, 1), F32)],
        compiler_params=_params("arbitrary"),
        name="mlstm",
    )(proj, proj, proj, proj, gates, gate_b.reshape(1, 2 * H), mh_g.reshape(1, v_tot))


def _gate_proj_kernel(x_ref, g_ref, w_ref, o_ref):
    o_ref[...] = _dot(_rms(x_ref[...], g_ref[...]).astype(BF16), w_ref[...])


def _gate_proj(h, gain, w, tm=512):
    T, D = h.shape
    n = w.shape[1]
    return pl.pallas_call(
        _gate_proj_kernel,
        grid=(T // tm,),
        in_specs=[
            pl.BlockSpec((tm, D), lambda i: (i, 0)),
            pl.BlockSpec((1, D), lambda i: (0, 0)),
            pl.BlockSpec((D, n), lambda i: (0, 0)),
        ],
        out_specs=pl.BlockSpec((tm, n), lambda i: (i, 0)),
        out_shape=jax.ShapeDtypeStruct((T, n), F32),
        compiler_params=_params("parallel"),
        name="gate_proj",
    )(h, gain.reshape(1, D), w)


def kernel(x, p, positions, norm_g, final_g, ffn_in, ffn_out, ple_proj, ple_gate, rwkv_lerp, rwkv_w0, rwkv_w1, rwkv_w2, rwkv_a0, rwkv_a1, rwkv_a2, rwkv_g1, rwkv_g2, rwkv_kk, rwkv_ka, rwkv_rk, rwkv_w_rkv, rwkv_w_o, rwkv_lnx_g, rwkv_lnx_b, rwkv_v0, rwkv_v1, rwkv_v2, ret_w_in, ret_ln_g, ret_w_o, ml_w_in, ml_gate_b, ml_mh_g, ml_w_o):
    B, T, D = x.shape
    depth = norm_g.shape[0]
    bf = lambda w: w.astype(BF16)
    ffn_in_b, ffn_out_b = bf(ffn_in), bf(ffn_out)
    ple_proj_b, ple_gate_b = bf(ple_proj), bf(ple_gate)
    rwkv_w_rkv_b, rwkv_w_o_b = bf(rwkv_w_rkv), bf(rwkv_w_o)
    ret_w_in_b, ret_w_o_b = bf(ret_w_in), bf(ret_w_o)
    ml_w_in_b, ml_w_o_b = bf(ml_w_in), bf(ml_w_o)
    outs = []
    for b in range(B):
        h = x[b]
        v_first = None
        for i in range(depth):
            kind, j = i % 3, i // 3
            h = _ffn(h, norm_g[i, 0], ffn_in_b, ffn_out_b, (i, 0))
            if kind == 0:
                v_res = None
                if j > 0:
                    v_res = (rwkv_v0[j - 1], bf(rwkv_v1[j - 1]), bf(rwkv_v2[j - 1]))
                r, k, v, a, lw, g = _rwkv_proj(
                    h, norm_g[i, 1], rwkv_lerp[j], rwkv_w_rkv_b, (j,), bf(rwkv_w1[j]), bf(rwkv_w2[j]),
                    bf(rwkv_a1[j]), bf(rwkv_a2[j]), bf(rwkv_g1[j]), bf(rwkv_g2[j]),
                    rwkv_w0[j], rwkv_a0[j], v_res, v_first)
                if j == 0:
                    v_first = v
                mix_in = _rwkv_scan(r, k, v, a, lw, g, rwkv_kk[j], rwkv_ka[j], rwkv_rk[j].reshape(D),
                                    rwkv_lnx_g[j], rwkv_lnx_b[j])
                w_o = rwkv_w_o_b
            elif kind == 1:
                proj = _norm_matmul(h, norm_g[i, 1], ret_w_in_b, (j,), ret_w_in.shape[2])
                half = ret_w_in.shape[2] // 6 // RET_HEADS // 2
                cos, sin = _rope_table(positions[b], half)
                mix_in = _retention_core(proj, cos, sin, ret_ln_g[j])
                w_o = ret_w_o_b
            else:
                n_main = ml_w_in.shape[2] - 2 * ML_HEADS
                proj = _norm_matmul(h, norm_g[i, 1], ml_w_in_b, (j,), n_main)
                gates = _gate_proj(h, norm_g[i, 1], ml_w_in_b[j][:, n_main:])
                mix_in = _mlstm_core(proj, gates, ml_gate_b[j], ml_mh_g[j])
                w_o = ml_w_o_b
            h = _matmul_residual(h, mix_in, w_o, (j,))
            h = _ffn(h, norm_g[i, 2], ffn_in_b, ffn_out_b, (i, 1))
            h = _ple(h, norm_g[i, 3], p[i, b], ple_proj_b, ple_gate_b, (i,), final_g, i == depth - 1)
        outs.append(h)
    return jnp.stack(outs, axis=0)
```

```python
import functools

import jax
import jax.numpy as jnp
from jax import lax
from jax.experimental import pallas as pl
from jax.experimental.pallas import tpu as pltpu

F32 = jnp.float32
BF16 = jnp.bfloat16

NORM_EPS = 1e-6
RWKV_HEAD = 64
RWKV_LNX_EPS = 64e-5
RWKV_CHUNK = 64
RET_HEADS = 8
RET_CHUNK = 128
RET_GN_EPS = 1e-6
ROPE_BASE = 10000.0
ML_HEADS = 8
ML_CHUNK = 64
ML_IGATE_CAP = 15.0
ML_NORM_EPS = 1e-6

LANES = 128
VMEM_LIMIT_BYTES = 56 * 1024 * 1024
HI = lax.Precision.HIGHEST


def _params(*sem):
    return pltpu.CompilerParams(dimension_semantics=sem, vmem_limit_bytes=VMEM_LIMIT_BYTES)


def _rms(x, gain):
    return x * lax.rsqrt(jnp.mean(x * x, axis=-1, keepdims=True) + NORM_EPS) * gain


def _wspec(block, lead, tail):
    return pl.BlockSpec((None,) * len(lead) + tuple(block), lambda *g: tuple(lead) + tuple(tail(*g)))


def _dot(a, b):
    return jnp.dot(a, b, preferred_element_type=F32)


def _dot_nt(a, b):
    return lax.dot_general(a, b, (((1,), (1,)), ((), ())), preferred_element_type=F32)


def _dot_tn(a, b):
    return lax.dot_general(a, b, (((0,), (0,)), ((), ())), preferred_element_type=F32)


def _ffn_kernel(nf, x_ref, g_ref, wg_ref, wu_ref, wo_ref, o_ref, xn_ref):
    f = pl.program_id(1)

    @pl.when(f == 0)
    def _():
        xn_ref[...] = _rms(x_ref[...], g_ref[...]).astype(BF16)
        o_ref[...] = jnp.zeros_like(o_ref)

    xn = xn_ref[...]
    gate = _dot(xn, wg_ref[...])
    up = _dot(xn, wu_ref[...])
    act = (gate * jax.nn.sigmoid(gate) * up).astype(BF16)
    o_ref[...] += _dot(act, wo_ref[...])

    @pl.when(f == nf - 1)
    def _():
        o_ref[...] = x_ref[...] + 0.5 * o_ref[...]


def _ffn(h, gain, w_in, w_out, lead, tm=1024, tf=512):
    T, D = h.shape
    F = w_out.shape[-2]
    nf = F // tf
    return pl.pallas_call(
        functools.partial(_ffn_kernel, nf),
        grid=(T // tm, nf),
        in_specs=[
            pl.BlockSpec((tm, D), lambda i, f: (i, 0)),
            pl.BlockSpec((1, D), lambda i, f: (0, 0)),
            _wspec((D, tf), lead, lambda i, f: (0, f)),
            _wspec((D, tf), lead, lambda i, f: (0, nf + f)),
            _wspec((tf, D), lead, lambda i, f: (f, 0)),
        ],
        out_specs=pl.BlockSpec((tm, D), lambda i, f: (i, 0)),
        out_shape=jax.ShapeDtypeStruct((T, D), F32),
        scratch_shapes=[pltpu.VMEM((tm, D), BF16)],
        compiler_params=_params("parallel", "arbitrary"),
        name="ffn",
    )(h, gain.reshape(1, D), w_in, w_in, w_out)


def _norm_matmul_kernel(x_ref, g_ref, w_ref, o_ref, xn_ref):
    @pl.when(pl.program_id(1) == 0)
    def _():
        xn_ref[...] = _rms(x_ref[...], g_ref[...]).astype(BF16)

    o_ref[...] = _dot(xn_ref[...], w_ref[...])


def _norm_matmul(h, gain, w, lead, N, tm=1024, tn=1024):
    T, D = h.shape
    return pl.pallas_call(
        _norm_matmul_kernel,
        grid=(T // tm, N // tn),
        in_specs=[
            pl.BlockSpec((tm, D), lambda i, j: (i, 0)),
            pl.BlockSpec((1, D), lambda i, j: (0, 0)),
            _wspec((D, tn), lead, lambda i, j: (0, j)),
        ],
        out_specs=pl.BlockSpec((tm, tn), lambda i, j: (i, j)),
        out_shape=jax.ShapeDtypeStruct((T, N), F32),
        scratch_shapes=[pltpu.VMEM((tm, D), BF16)],
        compiler_params=_params("parallel", "arbitrary"),
        name="norm_matmul",
    )(h, gain.reshape(1, D), w)


def _matmul_residual_kernel(h_ref, a_ref, w_ref, o_ref):
    o_ref[...] = h_ref[...] + _dot(a_ref[...], w_ref[...])


def _matmul_residual(h, a, w, lead, tm=512, tn=1024):
    T, D = h.shape
    K = a.shape[1]
    return pl.pallas_call(
        _matmul_residual_kernel,
        grid=(D // tn, T // tm),
        in_specs=[
            pl.BlockSpec((tm, tn), lambda j, i: (i, j)),
            pl.BlockSpec((tm, K), lambda j, i: (i, 0)),
            _wspec((K, tn), lead, lambda j, i: (0, j)),
        ],
        out_specs=pl.BlockSpec((tm, tn), lambda j, i: (i, j)),
        out_shape=jax.ShapeDtypeStruct((T, D), F32),
        compiler_params=_params("parallel", "parallel"),
        name="matmul_residual",
    )(h, a, w)


def _ple_kernel(final, h_ref, g_ref, p_ref, wp_ref, wg_ref, fg_ref, o_ref):
    h = h_ref[...]
    gate = jax.nn.sigmoid(_dot(_rms(h, g_ref[...]).astype(BF16), wg_ref[...]))
    out = h + _dot(p_ref[...].astype(BF16), wp_ref[...]) * gate
    if final:
        out = _rms(out, fg_ref[...])
    o_ref[...] = out


def _ple(h, gain, p, w_proj, w_gate, lead, final_gain, final, tm=512):
    T, D = h.shape
    P = p.shape[1]
    return pl.pallas_call(
        functools.partial(_ple_kernel, final),
        grid=(T // tm,),
        in_specs=[
            pl.BlockSpec((tm, D), lambda i: (i, 0)),
            pl.BlockSpec((1, D), lambda i: (0, 0)),
            pl.BlockSpec((tm, P), lambda i: (i, 0)),
            _wspec((P, D), lead, lambda i: (0, 0)),
            _wspec((D, D), lead, lambda i: (0, 0)),
            pl.BlockSpec((1, D), lambda i: (0, 0)),
        ],
        out_specs=pl.BlockSpec((tm, D), lambda i: (i, 0)),
        out_shape=jax.ShapeDtypeStruct((T, D), F32),
        compiler_params=_params("parallel"),
        name="ple",
    )(h, gain.reshape(1, D), p, w_proj, w_gate, final_gain.reshape(1, D))


def _rwkv_proj_kernel(has_vres, *refs):
    if has_vres:
        (h_ref, halo_ref, g_ref, lerp_ref, wr_ref, wk_ref, wv_ref, w1_ref, w2_ref, a1_ref, a2_ref,
         g1_ref, g2_ref, w0_ref, a0_ref, v1_ref, v2_ref, v0_ref, vf_ref,
         r_out, k_out, v_out, a_out, lw_out, g_out,
         xr_s, xk_s, xv_s, hw_s, ha_s, hg_s, hv_s) = refs
    else:
        (h_ref, halo_ref, g_ref, lerp_ref, wr_ref, wk_ref, wv_ref, w1_ref, w2_ref, a1_ref, a2_ref,
         g1_ref, g2_ref, w0_ref, a0_ref,
         r_out, k_out, v_out, a_out, lw_out, g_out,
         xr_s, xk_s, xv_s, hw_s, ha_s, hg_s) = refs
    i = pl.program_id(0)

    @pl.when(pl.program_id(1) == 0)
    def _():
        gain = g_ref[...]
        u = _rms(h_ref[...], gain)
        halo = _rms(halo_ref[...], gain)
        first = jnp.where(i > 0, halo[7:8, :], 0.0)
        row = lax.broadcasted_iota(jnp.int32, u.shape, 0)
        u_prev = jnp.where(row == 0, first, pltpu.roll(u, 1, 0))
        xx = u_prev - u
        lerp = lerp_ref[...]
        xr_s[...] = (u + xx * lerp[0:1]).astype(BF16)
        xw = (u + xx * lerp[1:2]).astype(BF16)
        xk_s[...] = (u + xx * lerp[2:3]).astype(BF16)
        xv = (u + xx * lerp[3:4]).astype(BF16)
        xv_s[...] = xv
        xa = (u + xx * lerp[4:5]).astype(BF16)
        xg = (u + xx * lerp[5:6]).astype(BF16)
        hw_s[...] = jnp.tanh(_dot(xw, w1_ref[...])).astype(BF16)
        ha_s[...] = _dot(xa, a1_ref[...]).astype(BF16)
        hg_s[...] = jax.nn.sigmoid(_dot(xg, g1_ref[...])).astype(BF16)
        if has_vres:
            hv_s[...] = _dot(xv, v1_ref[...]).astype(BF16)

    r_out[...] = _dot(xr_s[...], wr_ref[...])
    k_out[...] = _dot(xk_s[...], wk_ref[...])
    v = _dot(xv_s[...], wv_ref[...])
    if has_vres:
        mix = jax.nn.sigmoid(v0_ref[...] + _dot(hv_s[...], v2_ref[...]))
        v = v + (vf_ref[...] - v) * mix
    v_out[...] = v
    z = -(w0_ref[...] + _dot(hw_s[...], w2_ref[...]))
    softplus = jnp.maximum(z, 0.0) + jnp.log1p(jnp.exp(-jnp.abs(z)))
    lw_out[...] = -jnp.exp(-softplus - 0.5)
    a_out[...] = jax.nn.sigmoid(a0_ref[...] + _dot(ha_s[...], a2_ref[...]))
    g_out[...] = _dot(hg_s[...], g2_ref[...])


def _pad_cols(w, n):
    return jnp.pad(w, ((0, 0), (0, n - w.shape[1])))


def _pad_rows(w, n):
    return jnp.pad(w, ((0, n - w.shape[0]), (0, 0)))


def _rwkv_proj(h, gain, lerp, w_rkv, lead, w1, w2, a1, a2, g1, g2, w0, a0, v_res, v_first, tm=512, tn=512):
    T, D = h.shape
    has_vres = v_res is not None
    lo = LANES
    row = lambda x: x.reshape(1, D)
    full = lambda shape: pl.BlockSpec(shape, lambda i, j: (0, 0))
    col = lambda k: pl.BlockSpec((k, tn), lambda i, j: (0, j))
    tile = pl.BlockSpec((tm, tn), lambda i, j: (i, j))
    rowtile = pl.BlockSpec((1, tn), lambda i, j: (0, j))
    hb = tm // 8
    gd = g1.shape[1]
    args = [h, h, row(gain), _pad_rows(lerp, 8),
            w_rkv, w_rkv, w_rkv,
            _pad_cols(w1, lo), _pad_rows(w2, lo), _pad_cols(a1, lo), _pad_rows(a2, lo),
            g1, g2, row(w0), row(a0)]
    in_specs = [
        pl.BlockSpec((tm, D), lambda i, j: (i, 0)),
        pl.BlockSpec((8, D), lambda i, j: (jnp.maximum(i * hb - 1, 0), 0)),
        full((1, D)), full((8, D)),
        _wspec((D, tn), lead + (0,), lambda i, j: (0, j)),
        _wspec((D, tn), lead + (1,), lambda i, j: (0, j)),
        _wspec((D, tn), lead + (2,), lambda i, j: (0, j)),
        full((D, lo)), col(lo), full((D, lo)), col(lo),
        full((D, gd)), col(gd), rowtile, rowtile,
    ]
    scratch = [pltpu.VMEM((tm, D), BF16)] * 3 + [
        pltpu.VMEM((tm, lo), BF16), pltpu.VMEM((tm, lo), BF16), pltpu.VMEM((tm, gd), BF16)]
    if has_vres:
        v0, v1, v2 = v_res
        args += [_pad_cols(v1, lo), _pad_rows(v2, lo), row(v0), v_first]
        in_specs += [full((D, lo)), col(lo), rowtile, tile]
        scratch += [pltpu.VMEM((tm, lo), BF16)]
    return pl.pallas_call(
        functools.partial(_rwkv_proj_kernel, has_vres),
        grid=(T // tm, D // tn),
        in_specs=in_specs,
        out_specs=[tile] * 6,
        out_shape=[jax.ShapeDtypeStruct((T, D), F32)] * 6,
        scratch_shapes=scratch,
        compiler_params=_params("parallel", "arbitrary"),
        name="rwkv_proj",
    )(*args)


def _rwkv_scan_kernel(npairs, r_ref, k_ref, v_ref, a_ref, lw_ref, g_ref,
                      kk_ref, ka_ref, rk_ref, lg_ref, lb_ref, o_ref, s_ref):
    L = RWKV_CHUNK
    N = RWKV_HEAD
    W = 2 * N

    @pl.when(pl.program_id(1) == 0)
    def _():
        s_ref[...] = jnp.zeros_like(s_ref)

    t_i = lax.broadcasted_iota(jnp.int32, (L, W), 0)
    lane = lax.broadcasted_iota(jnp.int32, (L, W), 1)
    s_i = lane % N
    head0 = lane < N
    strict = s_i < t_i
    incl = s_i <= t_i
    same = [(t_i // b) == (s_i // b) for b in (8, 16, 32, L)]
    eye = jnp.where(s_i == t_i, 1.0, 0.0)
    brow = lax.broadcasted_iota(jnp.int32, (W, W), 0)
    blane = lax.broadcasted_iota(jnp.int32, (W, W), 1)
    bdiag = (brow // N) == (blane // N)

    def blk(x):
        return jnp.where(bdiag, jnp.concatenate([x, x], axis=0), 0.0)

    def segsum(x):
        s0 = jnp.sum(jnp.where(head0, x, 0.0), axis=1, keepdims=True)
        s1 = jnp.sum(jnp.where(head0, 0.0, x), axis=1, keepdims=True)
        return jnp.where(head0, s0, s1)

    def blk_b(x):
        return blk(x).astype(BF16)

    def mm(x, y):
        return _dot(x.astype(BF16), blk_b(y))

    tri_r = lax.broadcasted_iota(jnp.int32, (L, L), 0)
    tri_c = lax.broadcasted_iota(jnp.int32, (L, L), 1)
    tri = jnp.where(tri_c <= tri_r, 1.0, 0.0)
    c_all = jnp.dot(tri, lw_ref[...], precision=HI, preferred_element_type=F32)

    pairs = range(npairs)
    sls = [slice(p * W, (p + 1) * W) for p in pairs]

    def prep(sl):
        r = r_ref[:, sl]
        k = k_ref[:, sl]
        a = a_ref[:, sl]
        lw = lw_ref[:, sl]
        c = c_all[:, sl]
        kk = k * kk_ref[:, sl]
        kk = kk / jnp.maximum(jnp.sqrt(segsum(kk * kk)), 1e-12)
        kmod = k * (1.0 + (a - 1.0) * ka_ref[:, sl])
        alpha = -kk
        beta = kk * a
        c_mid = c[L // 2 - 1:L // 2, :]
        c_last = c[L - 1:L, :]
        e = c - c_mid
        ex_m = jnp.exp(-e)
        lhs = jnp.concatenate([alpha * jnp.exp(e - lw), r * jnp.exp(e)], axis=0).astype(BF16)
        rhs = jnp.concatenate([blk_b(beta * ex_m), blk_b(kmod * ex_m)], axis=0)
        to_end = jnp.exp(c_last - c)
        upd = jnp.concatenate([beta * to_end, kmod * to_end], axis=0).astype(BF16)
        return dict(lhs=lhs, rhs=rhs, upd=upd, r_abs=r * jnp.exp(c), al_abs=alpha * jnp.exp(c - lw),
                    p_last=jnp.exp(c_last), bonus=segsum(r * kmod * rk_ref[:, sl]))

    pre = [prep(sl) for sl in sls]
    amat = [_dot_nt(q["lhs"], q["rhs"]) for q in pre]
    n_ab = [jnp.where(strict, m[:L, :W], 0.0) for m in amat]
    a_rb = [jnp.where(incl, m[L:, :W], 0.0).astype(BF16) for m in amat]
    a_k = [jnp.concatenate([jnp.where(strict, m[:L, W:], 0.0), jnp.where(incl, m[L:, W:], 0.0)],
                           axis=0).astype(BF16) for m in amat]
    av = [_dot(x, blk_b(v_ref[:, sl])) for x, sl in zip(a_k, sls)]
    akv = [x[:L] for x in av]

    nd = [jnp.where(same[0], n, 0.0) for n in n_ab]
    tinv = [eye + n for n in nd]
    pw = [mm(n, n) for n in nd]
    both = [mm(jnp.concatenate([t, q], axis=0), q) for t, q in zip(tinv, pw)]
    tinv = [t + b[:L] for t, b in zip(tinv, both)]
    tinv = [t + mm(t, b[L:]) for t, b in zip(tinv, both)]
    for lvl in range(1, len(same)):
        off_mask = same[lvl] & jnp.logical_not(same[lvl - 1])
        inner = [mm(jnp.where(off_mask, n, 0.0), t) for n, t in zip(n_ab, tinv)]
        tinv = [t + mm(t, x) for t, x in zip(tinv, inner)]

    ta = [_dot(t.astype(BF16), jnp.concatenate([blk_b(q["al_abs"]), blk_b(x)], axis=1))
          for t, q, x in zip(tinv, pre, akv)]
    ra = [jnp.concatenate([q["r_abs"], t[:, :W]], axis=0).astype(BF16) for q, t in zip(pre, ta)]

    states = [s_ref[p] for p in pairs]
    su = [_dot_nt(x, s.astype(BF16)) for x, s in zip(ra, states)]
    us = [x[L:] + t[:, W:] for x, t in zip(su, ta)]
    ys = [x[:L] + _dot(a, blk_b(u)) + y[L:] for x, a, u, y in zip(su, a_rb, us, av)]
    uv = [jnp.concatenate([u, v_ref[:, sl]], axis=0).astype(BF16) for u, sl in zip(us, sls)]
    z = [_dot_tn(x, q["upd"]) for x, q in zip(uv, pre)]
    for p in pairs:
        s_ref[p] = states[p] * pre[p]["p_last"] + jnp.where(bdiag, z[p], 0.0)

    for p in pairs:
        sl = sls[p]
        y = ys[p]
        mu = segsum(y) * (1.0 / N)
        yc = y - mu
        var = segsum(yc * yc) * (1.0 / N)
        out = yc * lax.rsqrt(var + RWKV_LNX_EPS) * lg_ref[:, sl] + lb_ref[:, sl]
        out = out + pre[p]["bonus"] * v_ref[:, sl]
        o_ref[:, sl] = (out * g_ref[:, sl]).astype(BF16)


def _rwkv_scan(r, k, v, a, lw, g, k_k, k_a, r_k, lnx_g, lnx_b, npairs=16):
    T, D = r.shape
    L = RWKV_CHUNK
    wl = npairs * 2 * RWKV_HEAD
    tile = pl.BlockSpec((L, wl), lambda hg, c: (c, hg))
    prow = pl.BlockSpec((1, wl), lambda hg, c: (0, hg))
    row = lambda x: x.reshape(1, D)
    return pl.pallas_call(
        functools.partial(_rwkv_scan_kernel, npairs),
        grid=(D // wl, T // L),
        in_specs=[tile] * 6 + [prow] * 5,
        out_specs=tile,
        out_shape=jax.ShapeDtypeStruct((T, D), BF16),
        scratch_shapes=[pltpu.VMEM((npairs, 2 * RWKV_HEAD, 2 * RWKV_HEAD), F32)],
        compiler_params=_params("parallel", "arbitrary"),
        name="rwkv_scan",
    )(r, k, v, a, lw, g, row(k_k), row(k_a), row(r_k), row(lnx_g), row(lnx_b))


def _rope_table_kernel(pos_ref, cos_ref, sin_ref):
    half = cos_ref.shape[1]
    idx = lax.broadcasted_iota(jnp.int32, (1, half), 1).astype(F32)
    freqs = jnp.exp(idx * (-jnp.log(ROPE_BASE) / half))
    ang = pos_ref[...].astype(F32) * freqs
    cos_ref[...] = jnp.cos(ang)
    sin_ref[...] = jnp.sin(ang)


def _rope_table(positions, half, tm=1024):
    T = positions.shape[0]
    return pl.pallas_call(
        _rope_table_kernel,
        grid=(T // tm,),
        in_specs=[pl.BlockSpec((tm, 1), lambda i: (i, 0))],
        out_specs=[pl.BlockSpec((tm, half), lambda i: (i, 0))] * 2,
        out_shape=[jax.ShapeDtypeStruct((T, half), F32)] * 2,
        compiler_params=_params("parallel"),
        name="rope_table",
    )(positions.reshape(T, 1))


def _retention_kernel(q_ref, k_ref, v_ref, g_ref, cos_ref, sin_ref, lg_ref, ln_ref, o_ref, st_ref):
    L = RET_CHUNK
    H = RET_HEADS
    dk = q_ref.shape[1] // H
    dv = v_ref.shape[1] // H
    half = dk // 2

    @pl.when(pl.program_id(0) == 0)
    def _():
        st_ref[...] = jnp.zeros_like(st_ref)

    cos = cos_ref[...]
    sin = sin_ref[...]

    def rope(ref, h):
        x1 = ref[:, h * dk:h * dk + half]
        x2 = ref[:, h * dk + half:(h + 1) * dk]
        return jnp.concatenate([x1 * cos - x2 * sin, x1 * sin + x2 * cos], axis=1)

    row = lax.broadcasted_iota(jnp.int32, (L, L), 0)
    col = lax.broadcasted_iota(jnp.int32, (L, L), 1)
    rel = (row - col).astype(F32)
    tcol = lax.broadcasted_iota(jnp.int32, (L, 1), 0).astype(F32)

    heads = range(H)
    lgs = [lg_ref[h] for h in heads]
    intra = [jnp.where(rel >= 0, jnp.exp(jnp.maximum(rel, 0.0) * lg), 0.0) for lg in lgs]
    lg1 = [lg[:, 0:1] for lg in lgs]
    qb = [rope(q_ref, h).astype(BF16) for h in heads]
    ks = [rope(k_ref, h) * (dk ** -0.5) for h in heads]
    vb = [v_ref[:, h * dv:(h + 1) * dv].astype(BF16) for h in heads]
    s = [(_dot_nt(q, k.astype(BF16)) * m).astype(BF16) for q, k, m in zip(qb, ks, intra)]
    states = [st_ref[h] for h in heads]
    o = [_dot(x, v) + _dot(q, st.astype(BF16)) * jnp.exp((tcol + 1.0) * lg)
         for x, v, q, st, lg in zip(s, vb, qb, states, lg1)]
    kz = [(k * jnp.exp((L - 1.0 - tcol) * lg)).astype(BF16) for k, lg in zip(ks, lg1)]
    upd = [_dot_tn(k, v) for k, v in zip(kz, vb)]
    for h in heads:
        st_ref[h] = jnp.exp(L * lg1[h]) * states[h] + upd[h]

    for h in heads:
        sl = slice(h * dv, (h + 1) * dv)
        mu = jnp.mean(o[h], axis=-1, keepdims=True)
        oc = o[h] - mu
        on = oc * lax.rsqrt(jnp.mean(oc * oc, axis=-1, keepdims=True) + RET_GN_EPS) * ln_ref[:, sl]
        g = g_ref[:, sl]
        o_ref[:, sl] = (g * jax.nn.sigmoid(g) * on).astype(BF16)


def _retention_core(proj, cos, sin, ln_g):
    T = proj.shape[0]
    H = RET_HEADS
    L = RET_CHUNK
    dk = cos.shape[1] * 2
    v_tot = ln_g.shape[0]
    dv = v_tot // H
    qk_tot = H * dk
    hs = jnp.arange(H, dtype=F32)
    log_gamma = jnp.broadcast_to(jnp.log(1.0 - 2.0 ** (-5.0 - hs))[:, None, None], (H, 1, L))
    return pl.pallas_call(
        _retention_kernel,
        grid=(T // L,),
        in_specs=[
            pl.BlockSpec((L, qk_tot), lambda c: (c, 0)),
            pl.BlockSpec((L, qk_tot), lambda c: (c, 1)),
            pl.BlockSpec((L, v_tot), lambda c: (c, (2 * qk_tot) // v_tot)),
            pl.BlockSpec((L, v_tot), lambda c: (c, (2 * qk_tot) // v_tot + 1)),
            pl.BlockSpec((L, dk // 2), lambda c: (c, 0)),
            pl.BlockSpec((L, dk // 2), lambda c: (c, 0)),
            pl.BlockSpec((H, 1, L), lambda c: (0, 0, 0)),
            pl.BlockSpec((1, v_tot), lambda c: (0, 0)),
        ],
        out_specs=pl.BlockSpec((L, v_tot), lambda c: (c, 0)),
        out_shape=jax.ShapeDtypeStruct((T, v_tot), BF16),
        scratch_shapes=[pltpu.VMEM((H, dk, dv), F32)],
        compiler_params=_params("arbitrary"),
        name="retention",
    )(proj, proj, proj, proj, cos, sin, log_gamma, ln_g.reshape(1, v_tot))


def _mlstm_kernel(q_ref, k_ref, v_ref, og_ref, gates_ref, gb_ref, mh_ref, o_ref, c_ref, n_ref, m_ref):
    L = ML_CHUNK
    H = ML_HEADS
    dqk = q_ref.shape[1] // H
    dv = v_ref.shape[1] // H

    @pl.when(pl.program_id(0) == 0)
    def _():
        c_ref[...] = jnp.zeros_like(c_ref)
        n_ref[...] = jnp.zeros_like(n_ref)
        m_ref[...] = jnp.zeros_like(m_ref)

    row = lax.broadcasted_iota(jnp.int32, (L, L), 0)
    col = lax.broadcasted_iota(jnp.int32, (L, L), 1)
    causal = col <= row

    gates = gates_ref[...] + gb_ref[...]
    glane = lax.broadcasted_iota(jnp.int32, gates.shape, 1)
    log_i = ML_IGATE_CAP * jnp.tanh(gates / ML_IGATE_CAP)
    log_f = jnp.minimum(gates, 0.0) - jnp.log1p(jnp.exp(-jnp.abs(gates)))
    act = jnp.where(glane < H, log_i, log_f)
    cum_f = jnp.dot(jnp.where(causal, 1.0, 0.0), act[:, H:], precision=HI, preferred_element_type=F32)
    cols = jnp.concatenate([act, cum_f, jnp.zeros((L, LANES - 3 * H), F32)], axis=1)
    rows = cols.T

    heads = range(H)
    ic = [cols[:, h:h + 1] for h in heads]
    b_col = [cols[:, 2 * H + h:2 * H + h + 1] for h in heads]
    i_row = [rows[h:h + 1, :] for h in heads]
    b_row = [rows[2 * H + h:2 * H + h + 1, :] for h in heads]
    b_last = [b[L - 1:L, :] for b in b_col]
    m_prev = [m_ref[h] for h in heads]
    log_inter = [b + m for b, m in zip(b_col, m_prev)]
    log_intra = [jnp.where(causal, bc - br + ir, -jnp.inf) for bc, br, ir in zip(b_col, b_row, i_row)]
    m_t = [jnp.maximum(x, jnp.max(y, axis=1, keepdims=True)) for x, y in zip(log_inter, log_intra)]
    log_s = [bl - bc + i for bl, bc, i in zip(b_last, b_col, ic)]
    m_new = [jnp.maximum(bl + m, jnp.max(s, axis=0, keepdims=True)) for bl, m, s in zip(b_last, m_prev, log_s)]
    gt = [dict(m_t=mt, w_inter=jnp.exp(x - mt), w_intra=jnp.exp(y - mt), m_new=mn,
               ws=jnp.exp(s - mn), carry=jnp.exp(bl + m - mn))
          for mt, x, y, mn, s, bl, m in zip(m_t, log_inter, log_intra, m_new, log_s, b_last, m_prev)]
    qs = [q_ref[:, h * dqk:(h + 1) * dqk] for h in heads]
    ks = [k_ref[:, h * dqk:(h + 1) * dqk] * (dqk ** -0.5) for h in heads]
    qb = [q.astype(BF16) for q in qs]
    vb = [v_ref[:, h * dv:(h + 1) * dv].astype(BF16) for h in heads]
    s = [_dot_nt(q, k.astype(BF16)) * t["w_intra"] for q, k, t in zip(qb, ks, gt)]
    c_state = [c_ref[h] for h in heads]
    n_state = [n_ref[h] for h in heads]
    num = [_dot(x.astype(BF16), v) + t["w_inter"] * _dot(q, c.astype(BF16))
           for x, v, t, q, c in zip(s, vb, gt, qb, c_state)]
    den = [jnp.sum(x, axis=1, keepdims=True) + t["w_inter"] * jnp.sum(q * n, axis=1, keepdims=True)
           for x, t, q, n in zip(s, gt, qs, n_state)]
    kw = [k * t["ws"] for k, t in zip(ks, gt)]
    upd = [_dot_tn(k.astype(BF16), v) for k, v in zip(kw, vb)]
    for h in heads:
        c_ref[h] = gt[h]["carry"] * c_state[h] + upd[h]
        n_ref[h] = gt[h]["carry"] * n_state[h] + jnp.sum(kw[h], axis=0, keepdims=True)
        m_ref[h] = gt[h]["m_new"]

    for h in heads:
        sl = slice(h * dv, (h + 1) * dv)
        hid = num[h] / jnp.maximum(jnp.abs(den[h]), jnp.exp(-gt[h]["m_t"]))
        hn = hid * lax.rsqrt(jnp.mean(hid * hid, axis=-1, keepdims=True) + ML_NORM_EPS) * mh_ref[:, sl]
        o_ref[:, sl] = (jax.nn.sigmoid(og_ref[:, sl]) * hn).astype(BF16)


def _mlstm_core(proj, gates, gate_b, mh_g):
    T = proj.shape[0]
    H = ML_HEADS
    L = ML_CHUNK
    v_tot = mh_g.shape[0]
    dv = v_tot // H
    qk_tot = (proj.shape[1] - 2 * v_tot) // 2
    dqk = qk_tot // H
    return pl.pallas_call(
        _mlstm_kernel,
        grid=(T // L,),
        in_specs=[
            pl.BlockSpec((L, qk_tot), lambda c: (c, 0)),
            pl.BlockSpec((L, qk_tot), lambda c: (c, 1)),
            pl.BlockSpec((L, v_tot), lambda c: (c, (2 * qk_tot) // v_tot)),
            pl.BlockSpec((L, v_tot), lambda c: (c, (2 * qk_tot) // v_tot + 1)),
            pl.BlockSpec((L, 2 * H), lambda c: (c, 0)),
            pl.BlockSpec((1, 2 * H), lambda c: (0, 0)),
            pl.BlockSpec((1, v_tot), lambda c: (0, 0)),
        ],
        out_specs=pl.BlockSpec((L, v_tot), lambda c: (c, 0)),
        out_shape=jax.ShapeDtypeStruct((T, v_tot), BF16),
        scratch_shapes=[pltpu.VMEM((H, dqk, dv), F32), pltpu.VMEM((H, 1, dqk), F32),
                        pltpu.VMEM((H, 1, 1), F32)],
        compiler_params=_params("arbitrary"),
        name="mlstm",
    )(proj, proj, proj, proj, gates, gate_b.reshape(1, 2 * H), mh_g.reshape(1, v_tot))


def _gate_proj_kernel(x_ref, g_ref, w_ref, o_ref):
    o_ref[...] = _dot(_rms(x_ref[...], g_ref[...]).astype(BF16), w_ref[...])


def _gate_proj(h, gain, w, tm=512):
    T, D = h.shape
    n = w.shape[1]
    return pl.pallas_call(
        _gate_proj_kernel,
        grid=(T // tm,),
        in_specs=[
            pl.BlockSpec((tm, D), lambda i: (i, 0)),
            pl.BlockSpec((1, D), lambda i: (0, 0)),
            pl.BlockSpec((D, n), lambda i: (0, 0)),
        ],
        out_specs=pl.BlockSpec((tm, n), lambda i: (i, 0)),
        out_shape=jax.ShapeDtypeStruct((T, n), F32),
        compiler_params=_params("parallel"),
        name="gate_proj",
    )(h, gain.reshape(1, D), w)


def kernel(x, p, positions, norm_g, final_g, ffn_in, ffn_out, ple_proj, ple_gate, rwkv_lerp, rwkv_w0, rwkv_w1, rwkv_w2, rwkv_a0, rwkv_a1, rwkv_a2, rwkv_g1, rwkv_g2, rwkv_kk, rwkv_ka, rwkv_rk, rwkv_w_rkv, rwkv_w_o, rwkv_lnx_g, rwkv_lnx_b, rwkv_v0, rwkv_v1, rwkv_v2, ret_w_in, ret_ln_g, ret_w_o, ml_w_in, ml_gate_b, ml_mh_g, ml_w_o):
    B, T, D = x.shape
    depth = norm_g.shape[0]
    bf = lambda w: w.astype(BF16)
    ffn_in_b, ffn_out_b = bf(ffn_in), bf(ffn_out)
    ple_proj_b, ple_gate_b = bf(ple_proj), bf(ple_gate)
    rwkv_w_rkv_b, rwkv_w_o_b = bf(rwkv_w_rkv), bf(rwkv_w_o)
    ret_w_in_b, ret_w_o_b = bf(ret_w_in), bf(ret_w_o)
    ml_w_in_b, ml_w_o_b = bf(ml_w_in), bf(ml_w_o)
    outs = []
    for b in range(B):
        h = x[b]
        v_first = None
        for i in range(depth):
            kind, j = i % 3, i // 3
            h = _ffn(h, norm_g[i, 0], ffn_in_b, ffn_out_b, (i, 0))
            if kind == 0:
                v_res = None
                if j > 0:
                    v_res = (rwkv_v0[j - 1], bf(rwkv_v1[j - 1]), bf(rwkv_v2[j - 1]))
                r, k, v, a, lw, g = _rwkv_proj(
                    h, norm_g[i, 1], rwkv_lerp[j], rwkv_w_rkv_b, (j,), bf(rwkv_w1[j]), bf(rwkv_w2[j]),
                    bf(rwkv_a1[j]), bf(rwkv_a2[j]), bf(rwkv_g1[j]), bf(rwkv_g2[j]),
                    rwkv_w0[j], rwkv_a0[j], v_res, v_first)
                if j == 0:
                    v_first = v
                mix_in = _rwkv_scan(r, k, v, a, lw, g, rwkv_kk[j], rwkv_ka[j], rwkv_rk[j].reshape(D),
                                    rwkv_lnx_g[j], rwkv_lnx_b[j])
                w_o = rwkv_w_o_b
            elif kind == 1:
                proj = _norm_matmul(h, norm_g[i, 1], ret_w_in_b, (j,), ret_w_in.shape[2])
                half = ret_w_in.shape[2] // 6 // RET_HEADS // 2
                cos, sin = _rope_table(positions[b], half)
                mix_in = _retention_core(proj, cos, sin, ret_ln_g[j])
                w_o = ret_w_o_b
            else:
                n_main = ml_w_in.shape[2] - 2 * ML_HEADS
                proj = _norm_matmul(h, norm_g[i, 1], ml_w_in_b, (j,), n_main)
                gates = _gate_proj(h, norm_g[i, 1], ml_w_in_b[j][:, n_main:])
                mix_in = _mlstm_core(proj, gates, ml_gate_b[j], ml_mh_g[j])
                w_o = ml_w_o_b
            h = _matmul_residual(h, mix_in, w_o, (j,))
            h = _ffn(h, norm_g[i, 2], ffn_in_b, ffn_out_b, (i, 1))
            h = _ple(h, norm_g[i, 3], p[i, b], ple_proj_b, ple_gate_b, (i,), final_g, i == depth - 1)
        outs.append(h)
    return jnp.stack(outs, axis=0)
```

```python
import functools
import math

import jax
import jax.numpy as jnp
from jax import lax
from jax.experimental import pallas as pl
from jax.experimental.pallas import tpu as pltpu

F32 = jnp.float32
BF16 = jnp.bfloat16

NORM_EPS = 1e-6
RWKV_HEAD = 64
RWKV_LNX_EPS = 64e-5
RWKV_CHUNK = 64
RET_HEADS = 8
RET_CHUNK = 128
RET_GN_EPS = 1e-6
ROPE_BASE = 10000.0
ML_HEADS = 8
ML_CHUNK = 64
ML_IGATE_CAP = 15.0
ML_NORM_EPS = 1e-6

LANES = 128
VMEM_LIMIT_BYTES = 60 * 1024 * 1024
HI = lax.Precision.HIGHEST


def _params(*sem):
    return pltpu.CompilerParams(dimension_semantics=sem, vmem_limit_bytes=VMEM_LIMIT_BYTES)


def _rms(x, gain):
    return x * lax.rsqrt(jnp.mean(x * x, axis=-1, keepdims=True) + NORM_EPS) * gain


def _wspec(block, lead, tail):
    return pl.BlockSpec((None,) * len(lead) + tuple(block), lambda *g: tuple(lead) + tuple(tail(*g)))


def _dot(a, b):
    return jnp.dot(a, b, preferred_element_type=F32)


def _dot_nt(a, b):
    return lax.dot_general(a, b, (((1,), (1,)), ((), ())), preferred_element_type=F32)


def _dot_tn(a, b):
    return lax.dot_general(a, b, (((0,), (0,)), ((), ())), preferred_element_type=F32)


def _ffn_kernel(nf, x_ref, g_ref, wg_ref, wu_ref, wo_ref, o_ref, xn_ref):
    f = pl.program_id(1)

    @pl.when(f == 0)
    def _():
        xn_ref[...] = _rms(x_ref[...], g_ref[...]).astype(BF16)
        o_ref[...] = jnp.zeros_like(o_ref)

    xn = xn_ref[...]
    gate = _dot(xn, wg_ref[...].astype(BF16))
    up = _dot(xn, wu_ref[...].astype(BF16))
    act = (gate * jax.nn.sigmoid(gate) * up).astype(BF16)
    o_ref[...] += _dot(act, wo_ref[...].astype(BF16))

    @pl.when(f == nf - 1)
    def _():
        o_ref[...] = x_ref[...] + 0.5 * o_ref[...]


def _ffn(h, gain, w_in, w_out, lead, tm=1024, tf=256):
    T, D = h.shape
    F = w_out.shape[-2]
    nf = F // tf
    return pl.pallas_call(
        functools.partial(_ffn_kernel, nf),
        grid=(T // tm, nf),
        in_specs=[
            pl.BlockSpec((tm, D), lambda i, f: (i, 0)),
            pl.BlockSpec((1, D), lambda i, f: (0, 0)),
            _wspec((D, tf), lead, lambda i, f: (0, f)),
            _wspec((D, tf), lead, lambda i, f: (0, nf + f)),
            _wspec((tf, D), lead, lambda i, f: (f, 0)),
        ],
        out_specs=pl.BlockSpec((tm, D), lambda i, f: (i, 0)),
        out_shape=jax.ShapeDtypeStruct((T, D), F32),
        scratch_shapes=[pltpu.VMEM((tm, D), BF16)],
        compiler_params=_params("parallel", "arbitrary"),
        name="ffn",
    )(h, gain.reshape(1, D), w_in, w_in, w_out)


def _norm_matmul_kernel(x_ref, g_ref, w_ref, o_ref, xn_ref):
    @pl.when(pl.program_id(1) == 0)
    def _():
        xn_ref[...] = _rms(x_ref[...], g_ref[...]).astype(BF16)

    o_ref[...] = _dot(xn_ref[...], w_ref[...])


def _norm_matmul(h, gain, w, lead, N, tm=1024, tn=1024):
    T, D = h.shape
    return pl.pallas_call(
        _norm_matmul_kernel,
        grid=(T // tm, N // tn),
        in_specs=[
            pl.BlockSpec((tm, D), lambda i, j: (i, 0)),
            pl.BlockSpec((1, D), lambda i, j: (0, 0)),
            _wspec((D, tn), lead, lambda i, j: (0, j)),
        ],
        out_specs=pl.BlockSpec((tm, tn), lambda i, j: (i, j)),
        out_shape=jax.ShapeDtypeStruct((T, N), F32),
        scratch_shapes=[pltpu.VMEM((tm, D), BF16)],
        compiler_params=_params("parallel", "arbitrary"),
        name="norm_matmul",
    )(h, gain.reshape(1, D), w)


def _matmul_residual_kernel(h_ref, a_ref, w_ref, o_ref):
    o_ref[...] = h_ref[...] + _dot(a_ref[...], w_ref[...])


def _matmul_residual(h, a, w, lead, tm=512, tn=1024):
    T, D = h.shape
    K = a.shape[1]
    return pl.pallas_call(
        _matmul_residual_kernel,
        grid=(D // tn, T // tm),
        in_specs=[
            pl.BlockSpec((tm, tn), lambda j, i: (i, j)),
            pl.BlockSpec((tm, K), lambda j, i: (i, 0)),
            _wspec((K, tn), lead, lambda j, i: (0, j)),
        ],
        out_specs=pl.BlockSpec((tm, tn), lambda j, i: (i, j)),
        out_shape=jax.ShapeDtypeStruct((T, D), F32),
        compiler_params=_params("parallel", "parallel"),
        name="matmul_residual",
    )(h, a, w)


def _ple_kernel(final, h_ref, g_ref, p_ref, wp_ref, wg_ref, fg_ref, o_ref):
    h = h_ref[...]
    gate = jax.nn.sigmoid(_dot(_rms(h, g_ref[...]).astype(BF16), wg_ref[...]))
    out = h + _dot(p_ref[...].astype(BF16), wp_ref[...]) * gate
    if final:
        out = _rms(out, fg_ref[...])
    o_ref[...] = out


def _ple(h, gain, p, w_proj, w_gate, lead, final_gain, final, tm=512):
    T, D = h.shape
    P = p.shape[1]
    return pl.pallas_call(
        functools.partial(_ple_kernel, final),
        grid=(T // tm,),
        in_specs=[
            pl.BlockSpec((tm, D), lambda i: (i, 0)),
            pl.BlockSpec((1, D), lambda i: (0, 0)),
            pl.BlockSpec((tm, P), lambda i: (i, 0)),
            _wspec((P, D), lead, lambda i: (0, 0)),
            _wspec((D, D), lead, lambda i: (0, 0)),
            pl.BlockSpec((1, D), lambda i: (0, 0)),
        ],
        out_specs=pl.BlockSpec((tm, D), lambda i: (i, 0)),
        out_shape=jax.ShapeDtypeStruct((T, D), F32),
        compiler_params=_params("parallel"),
        name="ple",
    )(h, gain.reshape(1, D), p, w_proj, w_gate, final_gain.reshape(1, D))


def _rwkv_proj_kernel(has_vres, *refs):
    if has_vres:
        (h_ref, halo_ref, g_ref, lerp_ref, wr_ref, wk_ref, wv_ref, w1_ref, w2_ref, a1_ref, a2_ref,
         g1_ref, g2_ref, w0_ref, a0_ref, v1_ref, v2_ref, v0_ref, vf_ref,
         r_out, k_out, v_out, a_out, lw_out, g_out,
         xr_s, xk_s, xv_s, hw_s, ha_s, hg_s, hv_s) = refs
    else:
        (h_ref, halo_ref, g_ref, lerp_ref, wr_ref, wk_ref, wv_ref, w1_ref, w2_ref, a1_ref, a2_ref,
         g1_ref, g2_ref, w0_ref, a0_ref,
         r_out, k_out, v_out, a_out, lw_out, g_out,
         xr_s, xk_s, xv_s, hw_s, ha_s, hg_s) = refs
    i = pl.program_id(0)

    @pl.when(pl.program_id(1) == 0)
    def _():
        gain = g_ref[...]
        u = _rms(h_ref[...], gain)
        halo = _rms(halo_ref[...], gain)
        first = jnp.where(i > 0, halo[7:8, :], 0.0)
        row = lax.broadcasted_iota(jnp.int32, u.shape, 0)
        u_prev = jnp.where(row == 0, first, pltpu.roll(u, 1, 0))
        xx = u_prev - u
        lerp = lerp_ref[...]
        xr_s[...] = (u + xx * lerp[0:1]).astype(BF16)
        xw = (u + xx * lerp[1:2]).astype(BF16)
        xk_s[...] = (u + xx * lerp[2:3]).astype(BF16)
        xv = (u + xx * lerp[3:4]).astype(BF16)
        xv_s[...] = xv
        xa = (u + xx * lerp[4:5]).astype(BF16)
        xg = (u + xx * lerp[5:6]).astype(BF16)
        hw_s[...] = jnp.tanh(_dot(xw, w1_ref[...])).astype(BF16)
        ha_s[...] = _dot(xa, a1_ref[...]).astype(BF16)
        hg_s[...] = jax.nn.sigmoid(_dot(xg, g1_ref[...])).astype(BF16)
        if has_vres:
            hv_s[...] = _dot(xv, v1_ref[...]).astype(BF16)

    r_out[...] = _dot(xr_s[...], wr_ref[...])
    k_out[...] = _dot(xk_s[...], wk_ref[...])
    v = _dot(xv_s[...], wv_ref[...])
    if has_vres:
        mix = jax.nn.sigmoid(v0_ref[...] + _dot(hv_s[...], v2_ref[...]))
        v = v + (vf_ref[...] - v) * mix
    v_out[...] = v
    z = w0_ref[...] + _dot(hw_s[...], w2_ref[...])
    lw_out[...] = -math.exp(-0.5) * jax.nn.sigmoid(z)
    a_out[...] = jax.nn.sigmoid(a0_ref[...] + _dot(ha_s[...], a2_ref[...]))
    g_out[...] = _dot(hg_s[...], g2_ref[...])


def _pad_cols(w, n):
    return jnp.pad(w, ((0, 0), (0, n - w.shape[1])))


def _pad_rows(w, n):
    return jnp.pad(w, ((0, n - w.shape[0]), (0, 0)))


def _rwkv_proj(h, gain, lerp, w_rkv, lead, w1, w2, a1, a2, g1, g2, w0, a0, v_res, v_first, tm=512, tn=512):
    T, D = h.shape
    has_vres = v_res is not None
    lo = LANES
    row = lambda x: x.reshape(1, D)
    full = lambda shape: pl.BlockSpec(shape, lambda i, j: (0, 0))
    col = lambda k: pl.BlockSpec((k, tn), lambda i, j: (0, j))
    tile = pl.BlockSpec((tm, tn), lambda i, j: (i, j))
    rowtile = pl.BlockSpec((1, tn), lambda i, j: (0, j))
    hb = tm // 8
    gd = g1.shape[1]
    args = [h, h, row(gain), _pad_rows(lerp, 8),
            w_rkv, w_rkv, w_rkv,
            _pad_cols(w1, lo), _pad_rows(w2, lo), _pad_cols(a1, lo), _pad_rows(a2, lo),
            g1, g2, row(w0), row(a0)]
    in_specs = [
        pl.BlockSpec((tm, D), lambda i, j: (i, 0)),
        pl.BlockSpec((8, D), lambda i, j: (jnp.maximum(i * hb - 1, 0), 0)),
        full((1, D)), full((8, D)),
        _wspec((D, tn), lead + (0,), lambda i, j: (0, j)),
        _wspec((D, tn), lead + (1,), lambda i, j: (0, j)),
        _wspec((D, tn), lead + (2,), lambda i, j: (0, j)),
        full((D, lo)), col(lo), full((D, lo)), col(lo),
        full((D, gd)), col(gd), rowtile, rowtile,
    ]
    scratch = [pltpu.VMEM((tm, D), BF16)] * 3 + [
        pltpu.VMEM((tm, lo), BF16), pltpu.VMEM((tm, lo), BF16), pltpu.VMEM((tm, gd), BF16)]
    if has_vres:
        v0, v1, v2 = v_res
        args += [_pad_cols(v1, lo), _pad_rows(v2, lo), row(v0), v_first]
        in_specs += [full((D, lo)), col(lo), rowtile, tile]
        scratch += [pltpu.VMEM((tm, lo), BF16)]
    return pl.pallas_call(
        functools.partial(_rwkv_proj_kernel, has_vres),
        grid=(T // tm, D // tn),
        in_specs=in_specs,
        out_specs=[tile] * 6,
        out_shape=[jax.ShapeDtypeStruct((T, D), F32)] * 6,
        scratch_shapes=scratch,
        compiler_params=_params("parallel", "arbitrary"),
        name="rwkv_proj",
    )(*args)


def _rwkv_scan_kernel(npairs, r_ref, k_ref, v_ref, a_ref, lw_ref, g_ref,
                      kk_ref, ka_ref, rk_ref, lg_ref, lb_ref, o_ref, s_ref):
    L = RWKV_CHUNK
    N = RWKV_HEAD
    W = 2 * N

    @pl.when(pl.program_id(1) == 0)
    def _():
        s_ref[...] = jnp.zeros_like(s_ref)

    t_i = lax.broadcasted_iota(jnp.int32, (L, W), 0)
    lane = lax.broadcasted_iota(jnp.int32, (L, W), 1)
    s_i = lane % N
    head0 = lane < N
    strict = s_i < t_i
    incl = s_i <= t_i
    same = [(t_i // b) == (s_i // b) for b in (8, 16, 32, L)]
    eye = jnp.where(s_i == t_i, 1.0, 0.0)
    brow = lax.broadcasted_iota(jnp.int32, (W, W), 0)
    blane = lax.broadcasted_iota(jnp.int32, (W, W), 1)
    bdiag = (brow // N) == (blane // N)

    def blk(x):
        return jnp.where(bdiag, jnp.concatenate([x, x], axis=0), 0.0)

    def segsum(x):
        s0 = jnp.sum(jnp.where(head0, x, 0.0), axis=1, keepdims=True)
        s1 = jnp.sum(jnp.where(head0, 0.0, x), axis=1, keepdims=True)
        return jnp.where(head0, s0, s1)

    def blk_b(x):
        return blk(x).astype(BF16)

    def mm(x, y):
        return _dot(x.astype(BF16), blk_b(y))

    tri_r = lax.broadcasted_iota(jnp.int32, (L, L), 0)
    tri_c = lax.broadcasted_iota(jnp.int32, (L, L), 1)
    tri = jnp.where(tri_c <= tri_r, 1.0, 0.0)
    c_all = jnp.dot(tri, lw_ref[...], precision=HI, preferred_element_type=F32)

    pairs = range(npairs)
    sls = [slice(p * W, (p + 1) * W) for p in pairs]

    def prep(sl):
        r = r_ref[:, sl]
        k = k_ref[:, sl]
        a = a_ref[:, sl]
        lw = lw_ref[:, sl]
        c = c_all[:, sl]
        kk = k * kk_ref[:, sl]
        kk = kk / jnp.maximum(jnp.sqrt(segsum(kk * kk)), 1e-12)
        kmod = k * (1.0 + (a - 1.0) * ka_ref[:, sl])
        alpha = -kk
        beta = kk * a
        c_mid = c[L // 2 - 1:L // 2, :]
        c_last = c[L - 1:L, :]
        e = c - c_mid
        ex_m = jnp.exp(-e)
        lhs = jnp.concatenate([alpha * jnp.exp(e - lw), r * jnp.exp(e)], axis=0).astype(BF16)
        rhs = jnp.concatenate([blk_b(beta * ex_m), blk_b(kmod * ex_m)], axis=0)
        to_end = jnp.exp(c_last - c)
        upd = jnp.concatenate([beta * to_end, kmod * to_end], axis=0).astype(BF16)
        return dict(lhs=lhs, rhs=rhs, upd=upd, r_abs=r * jnp.exp(c), al_abs=alpha * jnp.exp(c - lw),
                    p_last=jnp.exp(c_last), bonus=segsum(r * kmod * rk_ref[:, sl]))

    pre = [prep(sl) for sl in sls]
    amat = [_dot_nt(q["lhs"], q["rhs"]) for q in pre]
    n_ab = [jnp.where(strict, m[:L, :W], 0.0) for m in amat]
    a_rb = [jnp.where(incl, m[L:, :W], 0.0).astype(BF16) for m in amat]
    a_k = [jnp.concatenate([jnp.where(strict, m[:L, W:], 0.0), jnp.where(incl, m[L:, W:], 0.0)],
                           axis=0).astype(BF16) for m in amat]
    av = [_dot(x, blk_b(v_ref[:, sl])) for x, sl in zip(a_k, sls)]
    akv = [x[:L] for x in av]

    nd = [jnp.where(same[0], n, 0.0) for n in n_ab]
    tinv = [eye + n for n in nd]
    pw = [mm(n, n) for n in nd]
    both = [mm(jnp.concatenate([t, q], axis=0), q) for t, q in zip(tinv, pw)]
    tinv = [t + b[:L] for t, b in zip(tinv, both)]
    tinv = [t + mm(t, b[L:]) for t, b in zip(tinv, both)]
    for lvl in range(1, len(same)):
        off_mask = same[lvl] & jnp.logical_not(same[lvl - 1])
        inner = [mm(jnp.where(off_mask, n, 0.0), t) for n, t in zip(n_ab, tinv)]
        tinv = [t + mm(t, x) for t, x in zip(tinv, inner)]

    ta = [_dot(t.astype(BF16), jnp.concatenate([blk_b(q["al_abs"]), blk_b(x)], axis=1))
          for t, q, x in zip(tinv, pre, akv)]
    ra = [jnp.concatenate([q["r_abs"], t[:, :W]], axis=0).astype(BF16) for q, t in zip(pre, ta)]

    states = [s_ref[p] for p in pairs]
    su = [_dot(x, s.astype(BF16)) for x, s in zip(ra, states)]
    us = [x[L:] + t[:, W:] for x, t in zip(su, ta)]
    ys = [x[:L] + _dot(a, blk_b(u)) + y[L:] for x, a, u, y in zip(su, a_rb, us, av)]
    uv = [jnp.concatenate([u, v_ref[:, sl]], axis=0).astype(BF16) for u, sl in zip(us, sls)]
    z = [_dot_tn(q["upd"], x) for x, q in zip(uv, pre)]
    c_end = jnp.concatenate([c_all[L - 1:L, sl] for sl in sls]
                            + [jnp.zeros((W - npairs, W), F32)], axis=0).T
    for p in pairs:
        s_ref[p] = states[p] * jnp.exp(c_end[:, p:p + 1]) + jnp.where(bdiag, z[p], 0.0)

    for p in pairs:
        sl = sls[p]
        y = ys[p]
        mu = segsum(y) * (1.0 / N)
        yc = y - mu
        var = segsum(yc * yc) * (1.0 / N)
        out = yc * lax.rsqrt(var + RWKV_LNX_EPS) * lg_ref[:, sl] + lb_ref[:, sl]
        out = out + pre[p]["bonus"] * v_ref[:, sl]
        o_ref[:, sl] = (out * g_ref[:, sl]).astype(BF16)


def _rwkv_scan(r, k, v, a, lw, g, k_k, k_a, r_k, lnx_g, lnx_b, npairs=16):
    T, D = r.shape
    L = RWKV_CHUNK
    wl = npairs * 2 * RWKV_HEAD
    tile = pl.BlockSpec((L, wl), lambda hg, c: (c, hg))
    prow = pl.BlockSpec((1, wl), lambda hg, c: (0, hg))
    row = lambda x: x.reshape(1, D)
    return pl.pallas_call(
        functools.partial(_rwkv_scan_kernel, npairs),
        grid=(D // wl, T // L),
        in_specs=[tile] * 6 + [prow] * 5,
        out_specs=tile,
        out_shape=jax.ShapeDtypeStruct((T, D), BF16),
        scratch_shapes=[pltpu.VMEM((npairs, 2 * RWKV_HEAD, 2 * RWKV_HEAD), F32)],
        compiler_params=_params("parallel", "arbitrary"),
        name="rwkv_scan",
    )(r, k, v, a, lw, g, row(k_k), row(k_a), row(r_k), row(lnx_g), row(lnx_b))


def _rope_table_kernel(pos_ref, cos_ref, sin_ref):
    half = cos_ref.shape[1]
    idx = lax.broadcasted_iota(jnp.int32, (1, half), 1).astype(F32)
    freqs = jnp.exp(idx * (-jnp.log(ROPE_BASE) / half))
    ang = pos_ref[...].astype(F32) * freqs
    cos_ref[...] = jnp.cos(ang)
    sin_ref[...] = jnp.sin(ang)


def _rope_table(positions, half, tm=1024):
    T = positions.shape[0]
    return pl.pallas_call(
        _rope_table_kernel,
        grid=(T // tm,),
        in_specs=[pl.BlockSpec((tm, 1), lambda i: (i, 0))],
        out_specs=[pl.BlockSpec((tm, half), lambda i: (i, 0))] * 2,
        out_shape=[jax.ShapeDtypeStruct((T, half), F32)] * 2,
        compiler_params=_params("parallel"),
        name="rope_table",
    )(positions.reshape(T, 1))


def _retention_kernel(q_ref, k_ref, v_ref, g_ref, cos_ref, sin_ref, lg_ref, ln_ref, o_ref, st_ref):
    L = RET_CHUNK
    H = RET_HEADS
    dk = q_ref.shape[1] // H
    dv = v_ref.shape[1] // H
    half = dk // 2

    @pl.when(pl.program_id(0) == 0)
    def _():
        st_ref[...] = jnp.zeros_like(st_ref)

    cos = cos_ref[...]
    sin = sin_ref[...]

    def rope(ref, h):
        x1 = ref[:, h * dk:h * dk + half]
        x2 = ref[:, h * dk + half:(h + 1) * dk]
        return jnp.concatenate([x1 * cos - x2 * sin, x1 * sin + x2 * cos], axis=1)

    row = lax.broadcasted_iota(jnp.int32, (L, L), 0)
    col = lax.broadcasted_iota(jnp.int32, (L, L), 1)
    rel = (row - col).astype(F32)
    tcol = lax.broadcasted_iota(jnp.int32, (L, 1), 0).astype(F32)

    heads = range(H)
    lgs = [lg_ref[h] for h in heads]
    intra = [jnp.where(rel >= 0, jnp.exp(jnp.maximum(rel, 0.0) * lg), 0.0) for lg in lgs]
    lg1 = [lg[:, 0:1] for lg in lgs]
    qb = [rope(q_ref, h).astype(BF16) for h in heads]
    ks = [rope(k_ref, h) * (dk ** -0.5) for h in heads]
    vb = [v_ref[:, h * dv:(h + 1) * dv].astype(BF16) for h in heads]
    s = [(_dot_nt(q, k.astype(BF16)) * m).astype(BF16) for q, k, m in zip(qb, ks, intra)]
    states = [st_ref[h] for h in heads]
    o = [_dot(x, v) + _dot(q, st.astype(BF16)) * jnp.exp((tcol + 1.0) * lg)
         for x, v, q, st, lg in zip(s, vb, qb, states, lg1)]
    kz = [(k * jnp.exp((L - 1.0 - tcol) * lg)).astype(BF16) for k, lg in zip(ks, lg1)]
    upd = [_dot_tn(k, v) for k, v in zip(kz, vb)]
    for h in heads:
        st_ref[h] = jnp.exp(L * lg1[h]) * states[h] + upd[h]

    for h in heads:
        sl = slice(h * dv, (h + 1) * dv)
        mu = jnp.mean(o[h], axis=-1, keepdims=True)
        oc = o[h] - mu
        on = oc * lax.rsqrt(jnp.mean(oc * oc, axis=-1, keepdims=True) + RET_GN_EPS) * ln_ref[:, sl]
        g = g_ref[:, sl]
        o_ref[:, sl] = (g * jax.nn.sigmoid(g) * on).astype(BF16)


def _retention_core(proj, cos, sin, ln_g):
    T = proj.shape[0]
    H = RET_HEADS
    L = RET_CHUNK
    dk = cos.shape[1] * 2
    v_tot = ln_g.shape[0]
    dv = v_tot // H
    qk_tot = H * dk
    hs = jnp.arange(H, dtype=F32)
    log_gamma = jnp.broadcast_to(jnp.log(1.0 - 2.0 ** (-5.0 - hs))[:, None, None], (H, 1, L))
    return pl.pallas_call(
        _retention_kernel,
        grid=(T // L,),
        in_specs=[
            pl.BlockSpec((L, qk_tot), lambda c: (c, 0)),
            pl.BlockSpec((L, qk_tot), lambda c: (c, 1)),
            pl.BlockSpec((L, v_tot), lambda c: (c, (2 * qk_tot) // v_tot)),
            pl.BlockSpec((L, v_tot), lambda c: (c, (2 * qk_tot) // v_tot + 1)),
            pl.BlockSpec((L, dk // 2), lambda c: (c, 0)),
            pl.BlockSpec((L, dk // 2), lambda c: (c, 0)),
            pl.BlockSpec((H, 1, L), lambda c: (0, 0, 0)),
            pl.BlockSpec((1, v_tot), lambda c: (0, 0)),
        ],
        out_specs=pl.BlockSpec((L, v_tot), lambda c: (c, 0)),
        out_shape=jax.ShapeDtypeStruct((T, v_tot), BF16),
        scratch_shapes=[pltpu.VMEM((H, dk, dv), F32)],
        compiler_params=_params("arbitrary"),
        name="retention",
    )(proj, proj, proj, proj, cos, sin, log_gamma, ln_g.reshape(1, v_tot))


def _mlstm_kernel(q_ref, k_ref, v_ref, og_ref, gates_ref, gb_ref, mh_ref, o_ref, c_ref, n_ref, m_ref):
    L = ML_CHUNK
    H = ML_HEADS
    dqk = q_ref.shape[1] // H
    dv = v_ref.shape[1] // H

    @pl.when(pl.program_id(0) == 0)
    def _():
        c_ref[...] = jnp.zeros_like(c_ref)
        n_ref[...] = jnp.zeros_like(n_ref)
        m_ref[...] = jnp.zeros_like(m_ref)

    row = lax.broadcasted_iota(jnp.int32, (L, L), 0)
    col = lax.broadcasted_iota(jnp.int32, (L, L), 1)
    causal = col <= row

    gates = gates_ref[...] + gb_ref[...]
    glane = lax.broadcasted_iota(jnp.int32, gates.shape, 1)
    log_i = ML_IGATE_CAP * jnp.tanh(gates / ML_IGATE_CAP)
    log_f = jnp.minimum(gates, 0.0) - jnp.log1p(jnp.exp(-jnp.abs(gates)))
    act = jnp.where(glane < H, log_i, log_f)
    cum_f = jnp.dot(jnp.where(causal, 1.0, 0.0), act[:, H:], precision=HI, preferred_element_type=F32)
    cols = jnp.concatenate([act, cum_f, jnp.zeros((L, LANES - 3 * H), F32)], axis=1)
    rows = cols.T

    heads = range(H)
    ic = [cols[:, h:h + 1] for h in heads]
    b_col = [cols[:, 2 * H + h:2 * H + h + 1] for h in heads]
    i_row = [rows[h:h + 1, :] for h in heads]
    b_row = [rows[2 * H + h:2 * H + h + 1, :] for h in heads]
    b_last = [b[L - 1:L, :] for b in b_col]
    m_prev = [m_ref[h] for h in heads]
    log_inter = [b + m for b, m in zip(b_col, m_prev)]
    log_intra = [jnp.where(causal, bc - br + ir, -jnp.inf) for bc, br, ir in zip(b_col, b_row, i_row)]
    m_t = [jnp.maximum(x, jnp.max(y, axis=1, keepdims=True)) for x, y in zip(log_inter, log_intra)]
    log_s = [bl - bc + i for bl, bc, i in zip(b_last, b_col, ic)]
    m_new = [jnp.maximum(bl + m, jnp.max(s, axis=0, keepdims=True)) for bl, m, s in zip(b_last, m_prev, log_s)]
    gt = [dict(m_t=mt, w_inter=jnp.exp(x - mt), w_intra=jnp.exp(y - mt), m_new=mn,
               ws=jnp.exp(s - mn), carry=jnp.exp(bl + m - mn))
          for mt, x, y, mn, s, bl, m in zip(m_t, log_inter, log_intra, m_new, log_s, b_last, m_prev)]
    qs = [q_ref[:, h * dqk:(h + 1) * dqk] for h in heads]
    ks = [k_ref[:, h * dqk:(h + 1) * dqk] * (dqk ** -0.5) for h in heads]
    qb = [q.astype(BF16) for q in qs]
    vb = [v_ref[:, h * dv:(h + 1) * dv].astype(BF16) for h in heads]
    s = [_dot_nt(q, k.astype(BF16)) * t["w_intra"] for q, k, t in zip(qb, ks, gt)]
    c_state = [c_ref[h] for h in heads]
    n_state = [n_ref[h] for h in heads]
    num = [_dot(x.astype(BF16), v) + t["w_inter"] * _dot(q, c.astype(BF16))
           for x, v, t, q, c in zip(s, vb, gt, qb, c_state)]
    den = [jnp.sum(x, axis=1, keepdims=True) + t["w_inter"] * jnp.sum(q * n, axis=1, keepdims=True)
           for x, t, q, n in zip(s, gt, qs, n_state)]
    kw = [k * t["ws"] for k, t in zip(ks, gt)]
    upd = [_dot_tn(k.astype(BF16), v) for k, v in zip(kw, vb)]
    for h in heads:
        c_ref[h] = gt[h]["carry"] * c_state[h] + upd[h]
        n_ref[h] = gt[h]["carry"] * n_state[h] + jnp.sum(kw[h], axis=0, keepdims=True)
        m_ref[h] = gt[h]["m_new"]

    for h in heads:
        sl = slice(h * dv, (h + 1) * dv)
        hid = num[h] / jnp.maximum(jnp.abs(den[h]), jnp.exp(-gt[h]["m_t"]))
        hn = hid * lax.rsqrt(jnp.mean(hid * hid, axis=-1, keepdims=True) + ML_NORM_EPS) * mh_ref[:, sl]
        o_ref[:, sl] = (jax.nn.sigmoid(og_ref[:, sl]) * hn).astype(BF16)


def _mlstm_core(proj, gates, gate_b, mh_g):
    T = proj.shape[0]
    H = ML_HEADS
    L = ML_CHUNK
    v_tot = mh_g.shape[0]
    dv = v_tot // H
    qk_tot = (proj.shape[1] - 2 * v_tot) // 2
    dqk = qk_tot // H
    return pl.pallas_call(
        _mlstm_kernel,
        grid=(T // L,),
        in_specs=[
            pl.BlockSpec((L, qk_tot), lambda c: (c, 0)),
            pl.BlockSpec((L, qk_tot), lambda c: (c, 1)),
            pl.BlockSpec((L, v_tot), lambda c: (c, (2 * qk_tot) // v_tot)),
            pl.BlockSpec((L, v_tot), lambda c: (c, (2 * qk_tot) // v_tot + 1)),
            pl.BlockSpec((L, 2 * H), lambda c: (c, 0)),
            pl.BlockSpec((1, 2 * H), lambda c: (0, 0)),
            pl.BlockSpec((1, v_tot), lambda c: (0, 0)),
        ],
        out_specs=pl.BlockSpec((L, v_tot), lambda c: (c, 0)),
        out_shape=jax.ShapeDtypeStruct((T, v_tot), BF16),
        scratch_shapes=[pltpu.VMEM((H, dqk, dv), F32), pltpu.VMEM((H, 1, dqk), F32),
                        pltpu.VMEM((H, 1, 1), F32)],
        compiler_params=_params("arbitrary"),
        name="mlstm",
    )(proj, proj, proj, proj, gates, gate_b.reshape(1, 2 * H), mh_g.reshape(1, v_tot))


def _gate_proj_kernel(x_ref, g_ref, w_ref, o_ref):
    o_ref[...] = _dot(_rms(x_ref[...], g_ref[...]).astype(BF16), w_ref[...])


def _gate_proj(h, gain, w, tm=512):
    T, D = h.shape
    n = w.shape[1]
    return pl.pallas_call(
        _gate_proj_kernel,
        grid=(T // tm,),
        in_specs=[
            pl.BlockSpec((tm, D), lambda i: (i, 0)),
            pl.BlockSpec((1, D), lambda i: (0, 0)),
            pl.BlockSpec((D, n), lambda i: (0, 0)),
        ],
        out_specs=pl.BlockSpec((tm, n), lambda i: (i, 0)),
        out_shape=jax.ShapeDtypeStruct((T, n), F32),
        compiler_params=_params("parallel"),
        name="gate_proj",
    )(h, gain.reshape(1, D), w)


def kernel(x, p, positions, norm_g, final_g, ffn_in, ffn_out, ple_proj, ple_gate, rwkv_lerp, rwkv_w0, rwkv_w1, rwkv_w2, rwkv_a0, rwkv_a1, rwkv_a2, rwkv_g1, rwkv_g2, rwkv_kk, rwkv_ka, rwkv_rk, rwkv_w_rkv, rwkv_w_o, rwkv_lnx_g, rwkv_lnx_b, rwkv_v0, rwkv_v1, rwkv_v2, ret_w_in, ret_ln_g, ret_w_o, ml_w_in, ml_gate_b, ml_mh_g, ml_w_o):
    B, T, D = x.shape
    depth = norm_g.shape[0]
    bf = lambda w: w.astype(BF16)
    ffn_in_b, ffn_out_b = bf(ffn_in), bf(ffn_out)
    ple_proj_b, ple_gate_b = bf(ple_proj), bf(ple_gate)
    rwkv_w_rkv_b, rwkv_w_o_b = bf(rwkv_w_rkv), bf(rwkv_w_o)
    ret_w_in_b, ret_w_o_b = bf(ret_w_in), bf(ret_w_o)
    ml_w_in_b, ml_w_o_b = bf(ml_w_in), bf(ml_w_o)
    outs = []
    for b in range(B):
        h = x[b]
        v_first = None
        for i in range(depth):
            kind, j = i % 3, i // 3
            h = _ffn(h, norm_g[i, 0], ffn_in, ffn_out, (i, 0))
            if kind == 0:
                v_res = None
                if j > 0:
                    v_res = (rwkv_v0[j - 1], bf(rwkv_v1[j - 1]), bf(rwkv_v2[j - 1]))
                r, k, v, a, lw, g = _rwkv_proj(
                    h, norm_g[i, 1], rwkv_lerp[j], rwkv_w_rkv_b, (j,), bf(rwkv_w1[j]), bf(rwkv_w2[j]),
                    bf(rwkv_a1[j]), bf(rwkv_a2[j]), bf(rwkv_g1[j]), bf(rwkv_g2[j]),
                    rwkv_w0[j], rwkv_a0[j], v_res, v_first)
                if j == 0:
                    v_first = v
                mix_in = _rwkv_scan(r, k, v, a, lw, g, rwkv_kk[j], rwkv_ka[j], rwkv_rk[j].reshape(D),
                                    rwkv_lnx_g[j], rwkv_lnx_b[j])
                w_o = rwkv_w_o_b
            elif kind == 1:
                proj = _norm_matmul(h, norm_g[i, 1], ret_w_in_b, (j,), ret_w_in.shape[2])
                half = ret_w_in.shape[2] // 6 // RET_HEADS // 2
                cos, sin = _rope_table(positions[b], half)
                mix_in = _retention_core(proj, cos, sin, ret_ln_g[j])
                w_o = ret_w_o_b
            else:
                n_main = ml_w_in.shape[2] - 2 * ML_HEADS
                proj = _norm_matmul(h, norm_g[i, 1], ml_w_in_b, (j,), n_main)
                gates = _gate_proj(h, norm_g[i, 1], ml_w_in_b[j][:, n_main:])
                mix_in = _mlstm_core(proj, gates, ml_gate_b[j], ml_mh_g[j])
                w_o = ml_w_o_b
            h = _matmul_residual(h, mix_in, w_o, (j,))
            h = _ffn(h, norm_g[i, 2], ffn_in, ffn_out, (i, 1))
            h = _ple(h, norm_g[i, 3], p[i, b], ple_proj_b, ple_gate_b, (i,), final_g, i == depth - 1)
        outs.append(h)
    return jnp.stack(outs, axis=0)
```

```python
import functools
import math

import jax
import jax.numpy as jnp
from jax import lax
from jax.experimental import pallas as pl
from jax.experimental.pallas import tpu as pltpu

F32 = jnp.float32
BF16 = jnp.bfloat16

NORM_EPS = 1e-6
RWKV_HEAD = 64
RWKV_LNX_EPS = 64e-5
RWKV_CHUNK = 64
RET_HEADS = 8
RET_CHUNK = 128
RET_GN_EPS = 1e-6
ROPE_BASE = 10000.0
ML_HEADS = 8
ML_CHUNK = 64
ML_IGATE_CAP = 15.0
ML_NORM_EPS = 1e-6

LANES = 128
VMEM_LIMIT_BYTES = 60 * 1024 * 1024
HI = lax.Precision.HIGHEST


def _params(*sem):
    return pltpu.CompilerParams(dimension_semantics=sem, vmem_limit_bytes=VMEM_LIMIT_BYTES)


def _rms(x, gain):
    return x * lax.rsqrt(jnp.mean(x * x, axis=-1, keepdims=True) + NORM_EPS) * gain


def _wspec(block, lead, tail):
    return pl.BlockSpec((None,) * len(lead) + tuple(block), lambda *g: tuple(lead) + tuple(tail(*g)))


def _dot(a, b):
    return jnp.dot(a, b, preferred_element_type=F32)


def _dot_nt(a, b):
    return lax.dot_general(a, b, (((1,), (1,)), ((), ())), preferred_element_type=F32)


def _dot_tn(a, b):
    return lax.dot_general(a, b, (((0,), (0,)), ((), ())), preferred_element_type=F32)


def _ffn_kernel(nf, x_ref, g_ref, wg_ref, wu_ref, wo_ref, o_ref, xn_ref):
    f = pl.program_id(1)

    @pl.when(f == 0)
    def _():
        xn_ref[...] = _rms(x_ref[...], g_ref[...]).astype(BF16)
        o_ref[...] = jnp.zeros_like(o_ref)

    xn = xn_ref[...]
    gate = _dot(xn, wg_ref[...].astype(BF16))
    up = _dot(xn, wu_ref[...].astype(BF16))
    act = (gate * jax.nn.sigmoid(gate) * up).astype(BF16)
    o_ref[...] += _dot(act, wo_ref[...].astype(BF16))

    @pl.when(f == nf - 1)
    def _():
        o_ref[...] = x_ref[...] + 0.5 * o_ref[...]


def _ffn(h, gain, w_in, w_out, lead, tm=1024, tf=256):
    T, D = h.shape
    F = w_out.shape[-2]
    nf = F // tf
    return pl.pallas_call(
        functools.partial(_ffn_kernel, nf),
        grid=(T // tm, nf),
        in_specs=[
            pl.BlockSpec((tm, D), lambda i, f: (i, 0)),
            pl.BlockSpec((1, D), lambda i, f: (0, 0)),
            _wspec((D, tf), lead, lambda i, f: (0, f)),
            _wspec((D, tf), lead, lambda i, f: (0, nf + f)),
            _wspec((tf, D), lead, lambda i, f: (f, 0)),
        ],
        out_specs=pl.BlockSpec((tm, D), lambda i, f: (i, 0)),
        out_shape=jax.ShapeDtypeStruct((T, D), F32),
        scratch_shapes=[pltpu.VMEM((tm, D), BF16)],
        compiler_params=_params("parallel", "arbitrary"),
        name="ffn",
    )(h, gain.reshape(1, D), w_in, w_in, w_out)


def _norm_matmul_kernel(x_ref, g_ref, w_ref, o_ref, xn_ref):
    @pl.when(pl.program_id(1) == 0)
    def _():
        xn_ref[...] = _rms(x_ref[...], g_ref[...]).astype(BF16)

    o_ref[...] = _dot(xn_ref[...], w_ref[...]).astype(o_ref.dtype)


def _norm_matmul(h, gain, w, lead, N, tm=1024, tn=2048):
    T, D = h.shape
    return pl.pallas_call(
        _norm_matmul_kernel,
        grid=(T // tm, N // tn),
        in_specs=[
            pl.BlockSpec((tm, D), lambda i, j: (i, 0)),
            pl.BlockSpec((1, D), lambda i, j: (0, 0)),
            _wspec((D, tn), lead, lambda i, j: (0, j)),
        ],
        out_specs=pl.BlockSpec((tm, tn), lambda i, j: (i, j)),
        out_shape=jax.ShapeDtypeStruct((T, N), BF16),
        scratch_shapes=[pltpu.VMEM((tm, D), BF16)],
        compiler_params=_params("parallel", "arbitrary"),
        name="norm_matmul",
    )(h, gain.reshape(1, D), w)


def _matmul_residual_kernel(h_ref, a_ref, w_ref, o_ref):
    o_ref[...] = h_ref[...] + _dot(a_ref[...], w_ref[...])


def _matmul_residual(h, a, w, lead, tm=1024, tn=1024):
    T, D = h.shape
    K = a.shape[1]
    return pl.pallas_call(
        _matmul_residual_kernel,
        grid=(D // tn, T // tm),
        in_specs=[
            pl.BlockSpec((tm, tn), lambda j, i: (i, j)),
            pl.BlockSpec((tm, K), lambda j, i: (i, 0)),
            _wspec((K, tn), lead, lambda j, i: (0, j)),
        ],
        out_specs=pl.BlockSpec((tm, tn), lambda j, i: (i, j)),
        out_shape=jax.ShapeDtypeStruct((T, D), F32),
        compiler_params=_params("parallel", "parallel"),
        name="matmul_residual",
    )(h, a, w)


def _ple_kernel(final, h_ref, g_ref, p_ref, wp_ref, wg_ref, fg_ref, o_ref):
    h = h_ref[...]
    gate = jax.nn.sigmoid(_dot(_rms(h, g_ref[...]).astype(BF16), wg_ref[...]))
    out = h + _dot(p_ref[...].astype(BF16), wp_ref[...]) * gate
    if final:
        out = _rms(out, fg_ref[...])
    o_ref[...] = out


def _ple(h, gain, p, w_proj, w_gate, lead, final_gain, final, tm=512):
    T, D = h.shape
    P = p.shape[1]
    return pl.pallas_call(
        functools.partial(_ple_kernel, final),
        grid=(T // tm,),
        in_specs=[
            pl.BlockSpec((tm, D), lambda i: (i, 0)),
            pl.BlockSpec((1, D), lambda i: (0, 0)),
            pl.BlockSpec((tm, P), lambda i: (i, 0)),
            _wspec((P, D), lead, lambda i: (0, 0)),
            _wspec((D, D), lead, lambda i: (0, 0)),
            pl.BlockSpec((1, D), lambda i: (0, 0)),
        ],
        out_specs=pl.BlockSpec((tm, D), lambda i: (i, 0)),
        out_shape=jax.ShapeDtypeStruct((T, D), F32),
        compiler_params=_params("parallel"),
        name="ple",
    )(h, gain.reshape(1, D), p, w_proj, w_gate, final_gain.reshape(1, D))


def _rwkv_proj_kernel(has_vres, *refs):
    if has_vres:
        (h_ref, halo_ref, g_ref, lerp_ref, wr_ref, wk_ref, wv_ref, w1_ref, w2_ref, a1_ref, a2_ref,
         g1_ref, g2_ref, w0_ref, a0_ref, v1_ref, v2_ref, v0_ref, vf_ref,
         r_out, k_out, v_out, a_out, lw_out, g_out,
         xr_s, xk_s, xv_s, hw_s, ha_s, hg_s, hv_s) = refs
    else:
        (h_ref, halo_ref, g_ref, lerp_ref, wr_ref, wk_ref, wv_ref, w1_ref, w2_ref, a1_ref, a2_ref,
         g1_ref, g2_ref, w0_ref, a0_ref,
         r_out, k_out, v_out, a_out, lw_out, g_out,
         xr_s, xk_s, xv_s, hw_s, ha_s, hg_s) = refs
    i = pl.program_id(0)

    @pl.when(pl.program_id(1) == 0)
    def _():
        gain = g_ref[...]
        u = _rms(h_ref[...], gain)
        halo = _rms(halo_ref[...], gain)
        first = jnp.where(i > 0, halo[7:8, :], 0.0)
        row = lax.broadcasted_iota(jnp.int32, u.shape, 0)
        u_prev = jnp.where(row == 0, first, pltpu.roll(u, 1, 0))
        xx = u_prev - u
        lerp = lerp_ref[...]
        xr_s[...] = (u + xx * lerp[0:1]).astype(BF16)
        xw = (u + xx * lerp[1:2]).astype(BF16)
        xk_s[...] = (u + xx * lerp[2:3]).astype(BF16)
        xv = (u + xx * lerp[3:4]).astype(BF16)
        xv_s[...] = xv
        xa = (u + xx * lerp[4:5]).astype(BF16)
        xg = (u + xx * lerp[5:6]).astype(BF16)
        hw_s[...] = jnp.tanh(_dot(xw, w1_ref[...])).astype(BF16)
        ha_s[...] = _dot(xa, a1_ref[...]).astype(BF16)
        hg_s[...] = jax.nn.sigmoid(_dot(xg, g1_ref[...])).astype(BF16)
        if has_vres:
            hv_s[...] = _dot(xv, v1_ref[...]).astype(BF16)

    r_out[...] = _dot(xr_s[...], wr_ref[...])
    k_out[...] = _dot(xk_s[...], wk_ref[...])
    v = _dot(xv_s[...], wv_ref[...])
    if has_vres:
        mix = jax.nn.sigmoid(v0_ref[...] + _dot(hv_s[...], v2_ref[...]))
        v = v + (vf_ref[...] - v) * mix
    v_out[...] = v
    z = w0_ref[...] + _dot(hw_s[...], w2_ref[...])
    lw_out[...] = -math.exp(-0.5) * jax.nn.sigmoid(z)
    a_out[...] = jax.nn.sigmoid(a0_ref[...] + _dot(ha_s[...], a2_ref[...]))
    g_out[...] = _dot(hg_s[...], g2_ref[...])


def _pad_cols(w, n):
    return jnp.pad(w, ((0, 0), (0, n - w.shape[1])))


def _pad_rows(w, n):
    return jnp.pad(w, ((0, n - w.shape[0]), (0, 0)))


def _rwkv_proj(h, gain, lerp, w_rkv, lead, w1, w2, a1, a2, g1, g2, w0, a0, v_res, v_first, tm=512, tn=512):
    T, D = h.shape
    has_vres = v_res is not None
    lo = LANES
    row = lambda x: x.reshape(1, D)
    full = lambda shape: pl.BlockSpec(shape, lambda i, j: (0, 0))
    col = lambda k: pl.BlockSpec((k, tn), lambda i, j: (0, j))
    tile = pl.BlockSpec((tm, tn), lambda i, j: (i, j))
    rowtile = pl.BlockSpec((1, tn), lambda i, j: (0, j))
    hb = tm // 8
    gd = g1.shape[1]
    args = [h, h, row(gain), _pad_rows(lerp, 8),
            w_rkv, w_rkv, w_rkv,
            _pad_cols(w1, lo), _pad_rows(w2, lo), _pad_cols(a1, lo), _pad_rows(a2, lo),
            g1, g2, row(w0), row(a0)]
    in_specs = [
        pl.BlockSpec((tm, D), lambda i, j: (i, 0)),
        pl.BlockSpec((8, D), lambda i, j: (jnp.maximum(i * hb - 1, 0), 0)),
        full((1, D)), full((8, D)),
        _wspec((D, tn), lead + (0,), lambda i, j: (0, j)),
        _wspec((D, tn), lead + (1,), lambda i, j: (0, j)),
        _wspec((D, tn), lead + (2,), lambda i, j: (0, j)),
        full((D, lo)), col(lo), full((D, lo)), col(lo),
        full((D, gd)), col(gd), rowtile, rowtile,
    ]
    scratch = [pltpu.VMEM((tm, D), BF16)] * 3 + [
        pltpu.VMEM((tm, lo), BF16), pltpu.VMEM((tm, lo), BF16), pltpu.VMEM((tm, gd), BF16)]
    if has_vres:
        v0, v1, v2 = v_res
        args += [_pad_cols(v1, lo), _pad_rows(v2, lo), row(v0), v_first]
        in_specs += [full((D, lo)), col(lo), rowtile, tile]
        scratch += [pltpu.VMEM((tm, lo), BF16)]
    return pl.pallas_call(
        functools.partial(_rwkv_proj_kernel, has_vres),
        grid=(T // tm, D // tn),
        in_specs=in_specs,
        out_specs=[tile] * 6,
        out_shape=[jax.ShapeDtypeStruct((T, D), F32)] * 6,
        scratch_shapes=scratch,
        compiler_params=_params("parallel", "arbitrary"),
        name="rwkv_proj",
    )(*args)


def _rwkv_scan_kernel(npairs, r_ref, k_ref, v_ref, a_ref, lw_ref, g_ref,
                      kk_ref, ka_ref, rk_ref, lg_ref, lb_ref, o_ref, s_ref):
    L = RWKV_CHUNK
    N = RWKV_HEAD
    W = 2 * N

    @pl.when(pl.program_id(1) == 0)
    def _():
        s_ref[...] = jnp.zeros_like(s_ref)

    t_i = lax.broadcasted_iota(jnp.int32, (L, W), 0)
    lane = lax.broadcasted_iota(jnp.int32, (L, W), 1)
    s_i = lane % N
    head0 = lane < N
    strict = s_i < t_i
    incl = s_i <= t_i
    same = [(t_i // b) == (s_i // b) for b in (8, 16, 32, L)]
    eye = jnp.where(s_i == t_i, 1.0, 0.0)
    brow = lax.broadcasted_iota(jnp.int32, (W, W), 0)
    blane = lax.broadcasted_iota(jnp.int32, (W, W), 1)
    bdiag = (brow // N) == (blane // N)

    def blk(x):
        return jnp.where(bdiag, jnp.concatenate([x, x], axis=0), 0.0)

    def segsum(x):
        s0 = jnp.sum(jnp.where(head0, x, 0.0), axis=1, keepdims=True)
        s1 = jnp.sum(jnp.where(head0, 0.0, x), axis=1, keepdims=True)
        return jnp.where(head0, s0, s1)

    def blk_b(x):
        return blk(x).astype(BF16)

    def mm(x, y):
        return _dot(x.astype(BF16), blk_b(y))

    tri_r = lax.broadcasted_iota(jnp.int32, (L, L), 0)
    tri_c = lax.broadcasted_iota(jnp.int32, (L, L), 1)
    tri = jnp.where(tri_c <= tri_r, 1.0, 0.0)
    c_all = jnp.dot(tri, lw_ref[...], precision=HI, preferred_element_type=F32)

    pairs = range(npairs)
    sls = [slice(p * W, (p + 1) * W) for p in pairs]

    def prep(sl):
        r = r_ref[:, sl]
        k = k_ref[:, sl]
        a = a_ref[:, sl]
        lw = lw_ref[:, sl]
        c = c_all[:, sl]
        kk = k * kk_ref[:, sl]
        kk = kk / jnp.maximum(jnp.sqrt(segsum(kk * kk)), 1e-12)
        kmod = k * (1.0 + (a - 1.0) * ka_ref[:, sl])
        alpha = -kk
        beta = kk * a
        c_mid = c[L // 2 - 1:L // 2, :]
        c_last = c[L - 1:L, :]
        e = c - c_mid
        ex_m = jnp.exp(-e)
        lhs = jnp.concatenate([alpha * jnp.exp(e - lw), r * jnp.exp(e)], axis=0).astype(BF16)
        rhs = jnp.concatenate([blk_b(beta * ex_m), blk_b(kmod * ex_m)], axis=0)
        to_end = jnp.exp(c_last - c)
        upd = jnp.concatenate([beta * to_end, kmod * to_end], axis=0).astype(BF16)
        return dict(lhs=lhs, rhs=rhs, upd=upd, r_abs=r * jnp.exp(c), al_abs=alpha * jnp.exp(c - lw),
                    p_last=jnp.exp(c_last), bonus=segsum(r * kmod * rk_ref[:, sl]))

    pre = [prep(sl) for sl in sls]
    amat = [_dot_nt(q["lhs"], q["rhs"]) for q in pre]
    n_ab = [jnp.where(strict, m[:L, :W], 0.0) for m in amat]
    a_rb = [jnp.where(incl, m[L:, :W], 0.0).astype(BF16) for m in amat]
    a_k = [jnp.concatenate([jnp.where(strict, m[:L, W:], 0.0), jnp.where(incl, m[L:, W:], 0.0)],
                           axis=0).astype(BF16) for m in amat]
    av = [_dot(x, blk_b(v_ref[:, sl])) for x, sl in zip(a_k, sls)]
    akv = [x[:L] for x in av]

    nd = [jnp.where(same[0], n, 0.0) for n in n_ab]
    tinv = [eye + n for n in nd]
    pw = [mm(n, n) for n in nd]
    both = [mm(jnp.concatenate([t, q], axis=0), q) for t, q in zip(tinv, pw)]
    tinv = [t + b[:L] for t, b in zip(tinv, both)]
    tinv = [t + mm(t, b[L:]) for t, b in zip(tinv, both)]
    for lvl in range(1, len(same)):
        off_mask = same[lvl] & jnp.logical_not(same[lvl - 1])
        inner = [mm(jnp.where(off_mask, n, 0.0), t) for n, t in zip(n_ab, tinv)]
        tinv = [t + mm(t, x) for t, x in zip(tinv, inner)]

    ta = [_dot(t.astype(BF16), jnp.concatenate([blk_b(q["al_abs"]), blk_b(x)], axis=1))
          for t, q, x in zip(tinv, pre, akv)]
    ra = [jnp.concatenate([q["r_abs"], t[:, :W]], axis=0).astype(BF16) for q, t in zip(pre, ta)]

    states = [s_ref[p] for p in pairs]
    su = [_dot(x, s.astype(BF16)) for x, s in zip(ra, states)]
    us = [x[L:] + t[:, W:] for x, t in zip(su, ta)]
    ys = [x[:L] + _dot(a, blk_b(u)) + y[L:] for x, a, u, y in zip(su, a_rb, us, av)]
    uv = [jnp.concatenate([u, v_ref[:, sl]], axis=0).astype(BF16) for u, sl in zip(us, sls)]
    z = [_dot_tn(q["upd"], x) for x, q in zip(uv, pre)]
    c_end = jnp.concatenate([c_all[L - 1:L, sl] for sl in sls]
                            + [jnp.zeros((W - npairs, W), F32)], axis=0).T
    for p in pairs:
        s_ref[p] = states[p] * jnp.exp(c_end[:, p:p + 1]) + jnp.where(bdiag, z[p], 0.0)

    for p in pairs:
        sl = sls[p]
        y = ys[p]
        mu = segsum(y) * (1.0 / N)
        yc = y - mu
        var = segsum(yc * yc) * (1.0 / N)
        out = yc * lax.rsqrt(var + RWKV_LNX_EPS) * lg_ref[:, sl] + lb_ref[:, sl]
        out = out + pre[p]["bonus"] * v_ref[:, sl]
        o_ref[:, sl] = (out * g_ref[:, sl]).astype(BF16)


def _rwkv_scan(r, k, v, a, lw, g, k_k, k_a, r_k, lnx_g, lnx_b, npairs=16):
    T, D = r.shape
    L = RWKV_CHUNK
    wl = npairs * 2 * RWKV_HEAD
    tile = pl.BlockSpec((L, wl), lambda hg, c: (c, hg))
    prow = pl.BlockSpec((1, wl), lambda hg, c: (0, hg))
    row = lambda x: x.reshape(1, D)
    return pl.pallas_call(
        functools.partial(_rwkv_scan_kernel, npairs),
        grid=(D // wl, T // L),
        in_specs=[tile] * 6 + [prow] * 5,
        out_specs=tile,
        out_shape=jax.ShapeDtypeStruct((T, D), BF16),
        scratch_shapes=[pltpu.VMEM((npairs, 2 * RWKV_HEAD, 2 * RWKV_HEAD), F32)],
        compiler_params=_params("parallel", "arbitrary"),
        name="rwkv_scan",
    )(r, k, v, a, lw, g, row(k_k), row(k_a), row(r_k), row(lnx_g), row(lnx_b))


def _rope_table_kernel(pos_ref, cos_ref, sin_ref):
    half = cos_ref.shape[1]
    idx = lax.broadcasted_iota(jnp.int32, (1, half), 1).astype(F32)
    freqs = jnp.exp(idx * (-jnp.log(ROPE_BASE) / half))
    ang = pos_ref[...].astype(F32) * freqs
    cos_ref[...] = jnp.cos(ang)
    sin_ref[...] = jnp.sin(ang)


def _rope_table(positions, half, tm=1024):
    T = positions.shape[0]
    return pl.pallas_call(
        _rope_table_kernel,
        grid=(T // tm,),
        in_specs=[pl.BlockSpec((tm, 1), lambda i: (i, 0))],
        out_specs=[pl.BlockSpec((tm, half), lambda i: (i, 0))] * 2,
        out_shape=[jax.ShapeDtypeStruct((T, half), F32)] * 2,
        compiler_params=_params("parallel"),
        name="rope_table",
    )(positions.reshape(T, 1))


def _retention_kernel(q_ref, k_ref, v_ref, g_ref, cos_ref, sin_ref, lg_ref, ln_ref, o_ref, st_ref):
    L = RET_CHUNK
    H = RET_HEADS
    dk = q_ref.shape[1] // H
    dv = v_ref.shape[1] // H
    half = dk // 2

    @pl.when(pl.program_id(0) == 0)
    def _():
        st_ref[...] = jnp.zeros_like(st_ref)

    cos = cos_ref[...]
    sin = sin_ref[...]

    def rope(ref, h):
        x1 = ref[:, h * dk:h * dk + half].astype(F32)
        x2 = ref[:, h * dk + half:(h + 1) * dk].astype(F32)
        return jnp.concatenate([x1 * cos - x2 * sin, x1 * sin + x2 * cos], axis=1)

    row = lax.broadcasted_iota(jnp.int32, (L, L), 0)
    col = lax.broadcasted_iota(jnp.int32, (L, L), 1)
    rel = (row - col).astype(F32)
    tcol = lax.broadcasted_iota(jnp.int32, (L, 1), 0).astype(F32)

    heads = range(H)
    lgs = [lg_ref[h] for h in heads]
    intra = [jnp.where(rel >= 0, jnp.exp(jnp.maximum(rel, 0.0) * lg), 0.0) for lg in lgs]
    lg1 = [lg[:, 0:1] for lg in lgs]
    qb = [rope(q_ref, h).astype(BF16) for h in heads]
    ks = [rope(k_ref, h) * (dk ** -0.5) for h in heads]
    vb = [v_ref[:, h * dv:(h + 1) * dv].astype(BF16) for h in heads]
    s = [(_dot_nt(q, k.astype(BF16)) * m).astype(BF16) for q, k, m in zip(qb, ks, intra)]
    states = [st_ref[h] for h in heads]
    o = [_dot(x, v) + _dot(q, st.astype(BF16)) * jnp.exp((tcol + 1.0) * lg)
         for x, v, q, st, lg in zip(s, vb, qb, states, lg1)]
    kz = [(k * jnp.exp((L - 1.0 - tcol) * lg)).astype(BF16) for k, lg in zip(ks, lg1)]
    upd = [_dot_tn(k, v) for k, v in zip(kz, vb)]
    for h in heads:
        st_ref[h] = jnp.exp(L * lg1[h]) * states[h] + upd[h]

    for h in heads:
        sl = slice(h * dv, (h + 1) * dv)
        mu = jnp.mean(o[h], axis=-1, keepdims=True)
        oc = o[h] - mu
        on = oc * lax.rsqrt(jnp.mean(oc * oc, axis=-1, keepdims=True) + RET_GN_EPS) * ln_ref[:, sl]
        g = g_ref[:, sl].astype(F32)
        o_ref[:, sl] = (g * jax.nn.sigmoid(g) * on).astype(BF16)


def _retention_core(proj, cos, sin, ln_g):
    T = proj.shape[0]
    H = RET_HEADS
    L = RET_CHUNK
    dk = cos.shape[1] * 2
    v_tot = ln_g.shape[0]
    dv = v_tot // H
    qk_tot = H * dk
    hs = jnp.arange(H, dtype=F32)
    log_gamma = jnp.broadcast_to(jnp.log(1.0 - 2.0 ** (-5.0 - hs))[:, None, None], (H, 1, L))
    return pl.pallas_call(
        _retention_kernel,
        grid=(T // L,),
        in_specs=[
            pl.BlockSpec((L, qk_tot), lambda c: (c, 0)),
            pl.BlockSpec((L, qk_tot), lambda c: (c, 1)),
            pl.BlockSpec((L, v_tot), lambda c: (c, (2 * qk_tot) // v_tot)),
            pl.BlockSpec((L, v_tot), lambda c: (c, (2 * qk_tot) // v_tot + 1)),
            pl.BlockSpec((L, dk // 2), lambda c: (c, 0)),
            pl.BlockSpec((L, dk // 2), lambda c: (c, 0)),
            pl.BlockSpec((H, 1, L), lambda c: (0, 0, 0)),
            pl.BlockSpec((1, v_tot), lambda c: (0, 0)),
        ],
        out_specs=pl.BlockSpec((L, v_tot), lambda c: (c, 0)),
        out_shape=jax.ShapeDtypeStruct((T, v_tot), BF16),
        scratch_shapes=[pltpu.VMEM((H, dk, dv), F32)],
        compiler_params=_params("arbitrary"),
        name="retention",
    )(proj, proj, proj, proj, cos, sin, log_gamma, ln_g.reshape(1, v_tot))


def _mlstm_kernel(q_ref, k_ref, v_ref, og_ref, gates_ref, gb_ref, mh_ref, o_ref, c_ref, n_ref, m_ref):
    L = ML_CHUNK
    H = ML_HEADS
    dqk = q_ref.shape[1] // H
    dv = v_ref.shape[1] // H

    @pl.when(pl.program_id(0) == 0)
    def _():
        c_ref[...] = jnp.zeros_like(c_ref)
        n_ref[...] = jnp.zeros_like(n_ref)
        m_ref[...] = jnp.zeros_like(m_ref)

    row = lax.broadcasted_iota(jnp.int32, (L, L), 0)
    col = lax.broadcasted_iota(jnp.int32, (L, L), 1)
    causal = col <= row

    gates = gates_ref[...] + gb_ref[...]
    glane = lax.broadcasted_iota(jnp.int32, gates.shape, 1)
    log_i = ML_IGATE_CAP * jnp.tanh(gates / ML_IGATE_CAP)
    log_f = jnp.minimum(gates, 0.0) - jnp.log1p(jnp.exp(-jnp.abs(gates)))
    act = jnp.where(glane < H, log_i, log_f)
    cum_f = jnp.dot(jnp.where(causal, 1.0, 0.0), act[:, H:], precision=HI, preferred_element_type=F32)
    cols = jnp.concatenate([act, cum_f, jnp.zeros((L, LANES - 3 * H), F32)], axis=1)
    rows = cols.T

    heads = range(H)
    ic = [cols[:, h:h + 1] for h in heads]
    b_col = [cols[:, 2 * H + h:2 * H + h + 1] for h in heads]
    i_row = [rows[h:h + 1, :] for h in heads]
    b_row = [rows[2 * H + h:2 * H + h + 1, :] for h in heads]
    b_last = [b[L - 1:L, :] for b in b_col]
    m_prev = [m_ref[h] for h in heads]
    log_inter = [b + m for b, m in zip(b_col, m_prev)]
    log_intra = [jnp.where(causal, bc - br + ir, -jnp.inf) for bc, br, ir in zip(b_col, b_row, i_row)]
    m_t = [jnp.maximum(x, jnp.max(y, axis=1, keepdims=True)) for x, y in zip(log_inter, log_intra)]
    log_s = [bl - bc + i for bl, bc, i in zip(b_last, b_col, ic)]
    m_new = [jnp.maximum(bl + m, jnp.max(s, axis=0, keepdims=True)) for bl, m, s in zip(b_last, m_prev, log_s)]
    gt = [dict(m_t=mt, w_inter=jnp.exp(x - mt), w_intra=jnp.exp(y - mt), m_new=mn,
               ws=jnp.exp(s - mn), carry=jnp.exp(bl + m - mn))
          for mt, x, y, mn, s, bl, m in zip(m_t, log_inter, log_intra, m_new, log_s, b_last, m_prev)]
    qs = [q_ref[:, h * dqk:(h + 1) * dqk].astype(F32) for h in heads]
    ks = [k_ref[:, h * dqk:(h + 1) * dqk].astype(F32) * (dqk ** -0.5) for h in heads]
    qb = [q.astype(BF16) for q in qs]
    vb = [v_ref[:, h * dv:(h + 1) * dv].astype(BF16) for h in heads]
    s = [_dot_nt(q, k.astype(BF16)) * t["w_intra"] for q, k, t in zip(qb, ks, gt)]
    c_state = [c_ref[h] for h in heads]
    n_state = [n_ref[h] for h in heads]
    num = [_dot(x.astype(BF16), v) + t["w_inter"] * _dot(q, c.astype(BF16))
           for x, v, t, q, c in zip(s, vb, gt, qb, c_state)]
    den = [jnp.sum(x, axis=1, keepdims=True) + t["w_inter"] * jnp.sum(q * n, axis=1, keepdims=True)
           for x, t, q, n in zip(s, gt, qs, n_state)]
    kw = [k * t["ws"] for k, t in zip(ks, gt)]
    upd = [_dot_tn(k.astype(BF16), v) for k, v in zip(kw, vb)]
    for h in heads:
        c_ref[h] = gt[h]["carry"] * c_state[h] + upd[h]
        n_ref[h] = gt[h]["carry"] * n_state[h] + jnp.sum(kw[h], axis=0, keepdims=True)
        m_ref[h] = gt[h]["m_new"]

    for h in heads:
        sl = slice(h * dv, (h + 1) * dv)
        hid = num[h] / jnp.maximum(jnp.abs(den[h]), jnp.exp(-gt[h]["m_t"]))
        hn = hid * lax.rsqrt(jnp.mean(hid * hid, axis=-1, keepdims=True) + ML_NORM_EPS) * mh_ref[:, sl]
        o_ref[:, sl] = (jax.nn.sigmoid(og_ref[:, sl].astype(F32)) * hn).astype(BF16)


def _mlstm_core(proj, gates, gate_b, mh_g):
    T = proj.shape[0]
    H = ML_HEADS
    L = ML_CHUNK
    v_tot = mh_g.shape[0]
    dv = v_tot // H
    qk_tot = (proj.shape[1] - 2 * v_tot) // 2
    dqk = qk_tot // H
    return pl.pallas_call(
        _mlstm_kernel,
        grid=(T // L,),
        in_specs=[
            pl.BlockSpec((L, qk_tot), lambda c: (c, 0)),
            pl.BlockSpec((L, qk_tot), lambda c: (c, 1)),
            pl.BlockSpec((L, v_tot), lambda c: (c, (2 * qk_tot) // v_tot)),
            pl.BlockSpec((L, v_tot), lambda c: (c, (2 * qk_tot) // v_tot + 1)),
            pl.BlockSpec((L, 2 * H), lambda c: (c, 0)),
            pl.BlockSpec((1, 2 * H), lambda c: (0, 0)),
            pl.BlockSpec((1, v_tot), lambda c: (0, 0)),
        ],
        out_specs=pl.BlockSpec((L, v_tot), lambda c: (c, 0)),
        out_shape=jax.ShapeDtypeStruct((T, v_tot), BF16),
        scratch_shapes=[pltpu.VMEM((H, dqk, dv), F32), pltpu.VMEM((H, 1, dqk), F32),
                        pltpu.VMEM((H, 1, 1), F32)],
        compiler_params=_params("arbitrary"),
        name="mlstm",
    )(proj, proj, proj, proj, gates, gate_b.reshape(1, 2 * H), mh_g.reshape(1, v_tot))


def _gate_proj_kernel(x_ref, g_ref, w_ref, o_ref):
    o_ref[...] = _dot(_rms(x_ref[...], g_ref[...]).astype(BF16), w_ref[...])


def _gate_proj(h, gain, w, tm=512):
    T, D = h.shape
    n = w.shape[1]
    return pl.pallas_call(
        _gate_proj_kernel,
        grid=(T // tm,),
        in_specs=[
            pl.BlockSpec((tm, D), lambda i: (i, 0)),
            pl.BlockSpec((1, D), lambda i: (0, 0)),
            pl.BlockSpec((D, n), lambda i: (0, 0)),
        ],
        out_specs=pl.BlockSpec((tm, n), lambda i: (i, 0)),
        out_shape=jax.ShapeDtypeStruct((T, n), F32),
        compiler_params=_params("parallel"),
        name="gate_proj",
    )(h, gain.reshape(1, D), w)


def kernel(x, p, positions, norm_g, final_g, ffn_in, ffn_out, ple_proj, ple_gate, rwkv_lerp, rwkv_w0, rwkv_w1, rwkv_w2, rwkv_a0, rwkv_a1, rwkv_a2, rwkv_g1, rwkv_g2, rwkv_kk, rwkv_ka, rwkv_rk, rwkv_w_rkv, rwkv_w_o, rwkv_lnx_g, rwkv_lnx_b, rwkv_v0, rwkv_v1, rwkv_v2, ret_w_in, ret_ln_g, ret_w_o, ml_w_in, ml_gate_b, ml_mh_g, ml_w_o):
    B, T, D = x.shape
    depth = norm_g.shape[0]
    bf = lambda w: w.astype(BF16)
    ffn_in_b, ffn_out_b = bf(ffn_in), bf(ffn_out)
    ple_proj_b, ple_gate_b = bf(ple_proj), bf(ple_gate)
    rwkv_w_rkv_b, rwkv_w_o_b = bf(rwkv_w_rkv), bf(rwkv_w_o)
    ret_w_in_b, ret_w_o_b = bf(ret_w_in), bf(ret_w_o)
    ml_w_in_b, ml_w_o_b = bf(ml_w_in), bf(ml_w_o)
    outs = []
    for b in range(B):
        h = x[b]
        v_first = None
        for i in range(depth):
            kind, j = i % 3, i // 3
            h = _ffn(h, norm_g[i, 0], ffn_in, ffn_out, (i, 0))
            if kind == 0:
                v_res = None
                if j > 0:
                    v_res = (rwkv_v0[j - 1], bf(rwkv_v1[j - 1]), bf(rwkv_v2[j - 1]))
                r, k, v, a, lw, g = _rwkv_proj(
                    h, norm_g[i, 1], rwkv_lerp[j], rwkv_w_rkv_b, (j,), bf(rwkv_w1[j]), bf(rwkv_w2[j]),
                    bf(rwkv_a1[j]), bf(rwkv_a2[j]), bf(rwkv_g1[j]), bf(rwkv_g2[j]),
                    rwkv_w0[j], rwkv_a0[j], v_res, v_first)
                if j == 0:
                    v_first = v
                mix_in = _rwkv_scan(r, k, v, a, lw, g, rwkv_kk[j], rwkv_ka[j], rwkv_rk[j].reshape(D),
                                    rwkv_lnx_g[j], rwkv_lnx_b[j])
                w_o = rwkv_w_o_b
            elif kind == 1:
                proj = _norm_matmul(h, norm_g[i, 1], ret_w_in_b, (j,), ret_w_in.shape[2])
                half = ret_w_in.shape[2] // 6 // RET_HEADS // 2
                cos, sin = _rope_table(positions[b], half)
                mix_in = _retention_core(proj, cos, sin, ret_ln_g[j])
                w_o = ret_w_o_b
            else:
                n_main = ml_w_in.shape[2] - 2 * ML_HEADS
                proj = _norm_matmul(h, norm_g[i, 1], ml_w_in_b, (j,), n_main)
                gates = _gate_proj(h, norm_g[i, 1], ml_w_in_b[j][:, n_main:])
                mix_in = _mlstm_core(proj, gates, ml_gate_b[j], ml_mh_g[j])
                w_o = ml_w_o_b
            h = _matmul_residual(h, mix_in, w_o, (j,))
            h = _ffn(h, norm_g[i, 2], ffn_in, ffn_out, (i, 1))
            h = _ple(h, norm_g[i, 3], p[i, b], ple_proj_b, ple_gate_b, (i,), final_g, i == depth - 1)
        outs.append(h)
    return jnp.stack(outs, axis=0)
```

```python
import functools
import math

import jax
import jax.numpy as jnp
from jax import lax
from jax.experimental import pallas as pl
from jax.experimental.pallas import tpu as pltpu

F32 = jnp.float32
BF16 = jnp.bfloat16

NORM_EPS = 1e-6
RWKV_HEAD = 64
RWKV_LNX_EPS = 64e-5
RWKV_CHUNK = 64
RET_HEADS = 8
RET_CHUNK = 128
RET_GN_EPS = 1e-6
ROPE_BASE = 10000.0
ML_HEADS = 8
ML_CHUNK = 64
ML_IGATE_CAP = 15.0
ML_NORM_EPS = 1e-6

LANES = 128
VMEM_LIMIT_BYTES = 60 * 1024 * 1024
HI = lax.Precision.HIGHEST


def _params(*sem):
    return pltpu.CompilerParams(dimension_semantics=sem, vmem_limit_bytes=VMEM_LIMIT_BYTES)


def _rms(x, gain):
    return x * lax.rsqrt(jnp.mean(x * x, axis=-1, keepdims=True) + NORM_EPS) * gain


def _wspec(block, lead, tail):
    return pl.BlockSpec((None,) * len(lead) + tuple(block), lambda *g: tuple(lead) + tuple(tail(*g)))


def _dot(a, b):
    return jnp.dot(a, b, preferred_element_type=F32)


def _dot_nt(a, b):
    return lax.dot_general(a, b, (((1,), (1,)), ((), ())), preferred_element_type=F32)


def _dot_tn(a, b):
    return lax.dot_general(a, b, (((0,), (0,)), ((), ())), preferred_element_type=F32)


def _ffn_kernel(nf, x_ref, g_ref, wg_ref, wu_ref, wo_ref, o_ref, xn_ref):
    f = pl.program_id(1)

    @pl.when(f == 0)
    def _():
        xn_ref[...] = _rms(x_ref[...], g_ref[...]).astype(BF16)
        o_ref[...] = jnp.zeros_like(o_ref)

    xn = xn_ref[...]
    gate = _dot(xn, wg_ref[...].astype(BF16))
    up = _dot(xn, wu_ref[...].astype(BF16))
    act = (gate * jax.nn.sigmoid(gate) * up).astype(BF16)
    o_ref[...] += _dot(act, wo_ref[...].astype(BF16))

    @pl.when(f == nf - 1)
    def _():
        o_ref[...] = x_ref[...] + 0.5 * o_ref[...]


def _ffn(h, gain, w_in, w_out, lead, tm=1024, tf=256):
    T, D = h.shape
    F = w_out.shape[-2]
    nf = F // tf
    return pl.pallas_call(
        functools.partial(_ffn_kernel, nf),
        grid=(T // tm, nf),
        in_specs=[
            pl.BlockSpec((tm, D), lambda i, f: (i, 0)),
            pl.BlockSpec((1, D), lambda i, f: (0, 0)),
            _wspec((D, tf), lead, lambda i, f: (0, f)),
            _wspec((D, tf), lead, lambda i, f: (0, nf + f)),
            _wspec((tf, D), lead, lambda i, f: (f, 0)),
        ],
        out_specs=pl.BlockSpec((tm, D), lambda i, f: (i, 0)),
        out_shape=jax.ShapeDtypeStruct((T, D), F32),
        scratch_shapes=[pltpu.VMEM((tm, D), BF16)],
        compiler_params=_params("parallel", "arbitrary"),
        name="ffn",
    )(h, gain.reshape(1, D), w_in, w_in, w_out)


def _norm_matmul_kernel(x_ref, g_ref, w_ref, o_ref, xn_ref):
    @pl.when(pl.program_id(1) == 0)
    def _():
        xn_ref[...] = _rms(x_ref[...], g_ref[...]).astype(BF16)

    o_ref[...] = _dot(xn_ref[...], w_ref[...]).astype(o_ref.dtype)


def _norm_matmul(h, gain, w, lead, N, tm=1024, tn=2048):
    T, D = h.shape
    return pl.pallas_call(
        _norm_matmul_kernel,
        grid=(T // tm, N // tn),
        in_specs=[
            pl.BlockSpec((tm, D), lambda i, j: (i, 0)),
            pl.BlockSpec((1, D), lambda i, j: (0, 0)),
            _wspec((D, tn), lead, lambda i, j: (0, j)),
        ],
        out_specs=pl.BlockSpec((tm, tn), lambda i, j: (i, j)),
        out_shape=jax.ShapeDtypeStruct((T, N), BF16),
        scratch_shapes=[pltpu.VMEM((tm, D), BF16)],
        compiler_params=_params("parallel", "arbitrary"),
        name="norm_matmul",
    )(h, gain.reshape(1, D), w)


def _matmul_residual_kernel(h_ref, a_ref, w_ref, o_ref):
    o_ref[...] = h_ref[...] + _dot(a_ref[...], w_ref[...])


def _matmul_residual(h, a, w, lead, tm=1024, tn=1024):
    T, D = h.shape
    K = a.shape[1]
    return pl.pallas_call(
        _matmul_residual_kernel,
        grid=(D // tn, T // tm),
        in_specs=[
            pl.BlockSpec((tm, tn), lambda j, i: (i, j)),
            pl.BlockSpec((tm, K), lambda j, i: (i, 0)),
            _wspec((K, tn), lead, lambda j, i: (0, j)),
        ],
        out_specs=pl.BlockSpec((tm, tn), lambda j, i: (i, j)),
        out_shape=jax.ShapeDtypeStruct((T, D), F32),
        compiler_params=_params("parallel", "parallel"),
        name="matmul_residual",
    )(h, a, w)


def _ple_kernel(final, h_ref, g_ref, p_ref, wp_ref, wg_ref, fg_ref, o_ref):
    h = h_ref[...]
    gate = jax.nn.sigmoid(_dot(_rms(h, g_ref[...]).astype(BF16), wg_ref[...]))
    out = h + _dot(p_ref[...].astype(BF16), wp_ref[...]) * gate
    if final:
        out = _rms(out, fg_ref[...])
    o_ref[...] = out


def _ple(h, gain, p, w_proj, w_gate, lead, final_gain, final, tm=512):
    T, D = h.shape
    P = p.shape[1]
    return pl.pallas_call(
        functools.partial(_ple_kernel, final),
        grid=(T // tm,),
        in_specs=[
            pl.BlockSpec((tm, D), lambda i: (i, 0)),
            pl.BlockSpec((1, D), lambda i: (0, 0)),
            pl.BlockSpec((tm, P), lambda i: (i, 0)),
            _wspec((P, D), lead, lambda i: (0, 0)),
            _wspec((D, D), lead, lambda i: (0, 0)),
            pl.BlockSpec((1, D), lambda i: (0, 0)),
        ],
        out_specs=pl.BlockSpec((tm, D), lambda i: (i, 0)),
        out_shape=jax.ShapeDtypeStruct((T, D), F32),
        compiler_params=_params("parallel"),
        name="ple",
    )(h, gain.reshape(1, D), p, w_proj, w_gate, final_gain.reshape(1, D))


def _rwkv_proj_kernel(has_vres, *refs):
    if has_vres:
        (h_ref, halo_ref, g_ref, lerp_ref, wr_ref, wk_ref, wv_ref, w1_ref, w2_ref, a1_ref, a2_ref,
         g1_ref, g2_ref, w0_ref, a0_ref, v1_ref, v2_ref, v0_ref, vf_ref,
         r_out, k_out, v_out, a_out, lw_out, g_out,
         xr_s, xk_s, xv_s, hw_s, ha_s, hg_s, hv_s) = refs
    else:
        (h_ref, halo_ref, g_ref, lerp_ref, wr_ref, wk_ref, wv_ref, w1_ref, w2_ref, a1_ref, a2_ref,
         g1_ref, g2_ref, w0_ref, a0_ref,
         r_out, k_out, v_out, a_out, lw_out, g_out,
         xr_s, xk_s, xv_s, hw_s, ha_s, hg_s) = refs
    i = pl.program_id(0)

    @pl.when(pl.program_id(1) == 0)
    def _():
        gain = g_ref[...]
        u = _rms(h_ref[...], gain)
        halo = _rms(halo_ref[...], gain)
        first = jnp.where(i > 0, halo[7:8, :], 0.0)
        row = lax.broadcasted_iota(jnp.int32, u.shape, 0)
        u_prev = jnp.where(row == 0, first, pltpu.roll(u, 1, 0))
        xx = u_prev - u
        lerp = lerp_ref[...]
        xr_s[...] = (u + xx * lerp[0:1]).astype(BF16)
        xw = (u + xx * lerp[1:2]).astype(BF16)
        xk_s[...] = (u + xx * lerp[2:3]).astype(BF16)
        xv = (u + xx * lerp[3:4]).astype(BF16)
        xv_s[...] = xv
        xa = (u + xx * lerp[4:5]).astype(BF16)
        xg = (u + xx * lerp[5:6]).astype(BF16)
        hw_s[...] = jnp.tanh(_dot(xw, w1_ref[...])).astype(BF16)
        ha_s[...] = _dot(xa, a1_ref[...]).astype(BF16)
        hg_s[...] = jax.nn.sigmoid(_dot(xg, g1_ref[...])).astype(BF16)
        if has_vres:
            hv_s[...] = _dot(xv, v1_ref[...]).astype(BF16)

    r_out[...] = _dot(xr_s[...], wr_ref[...]).astype(r_out.dtype)
    k_out[...] = _dot(xk_s[...], wk_ref[...]).astype(k_out.dtype)
    v = _dot(xv_s[...], wv_ref[...])
    if has_vres:
        mix = jax.nn.sigmoid(v0_ref[...] + _dot(hv_s[...], v2_ref[...]))
        v = v + (vf_ref[...].astype(F32) - v) * mix
    v_out[...] = v.astype(v_out.dtype)
    z = w0_ref[...] + _dot(hw_s[...], w2_ref[...])
    lw_out[...] = -math.exp(-0.5) * jax.nn.sigmoid(z)
    a_out[...] = jax.nn.sigmoid(a0_ref[...] + _dot(ha_s[...], a2_ref[...])).astype(a_out.dtype)
    g_out[...] = _dot(hg_s[...], g2_ref[...]).astype(g_out.dtype)


def _pad_cols(w, n):
    return jnp.pad(w, ((0, 0), (0, n - w.shape[1])))


def _pad_rows(w, n):
    return jnp.pad(w, ((0, n - w.shape[0]), (0, 0)))


def _rwkv_proj(h, gain, lerp, w_rkv, lead, w1, w2, a1, a2, g1, g2, w0, a0, v_res, v_first, tm=512, tn=512):
    T, D = h.shape
    has_vres = v_res is not None
    lo = LANES
    row = lambda x: x.reshape(1, D)
    full = lambda shape: pl.BlockSpec(shape, lambda i, j: (0, 0))
    col = lambda k: pl.BlockSpec((k, tn), lambda i, j: (0, j))
    tile = pl.BlockSpec((tm, tn), lambda i, j: (i, j))
    rowtile = pl.BlockSpec((1, tn), lambda i, j: (0, j))
    hb = tm // 8
    gd = g1.shape[1]
    args = [h, h, row(gain), _pad_rows(lerp, 8),
            w_rkv, w_rkv, w_rkv,
            _pad_cols(w1, lo), _pad_rows(w2, lo), _pad_cols(a1, lo), _pad_rows(a2, lo),
            g1, g2, row(w0), row(a0)]
    in_specs = [
        pl.BlockSpec((tm, D), lambda i, j: (i, 0)),
        pl.BlockSpec((8, D), lambda i, j: (jnp.maximum(i * hb - 1, 0), 0)),
        full((1, D)), full((8, D)),
        _wspec((D, tn), lead + (0,), lambda i, j: (0, j)),
        _wspec((D, tn), lead + (1,), lambda i, j: (0, j)),
        _wspec((D, tn), lead + (2,), lambda i, j: (0, j)),
        full((D, lo)), col(lo), full((D, lo)), col(lo),
        full((D, gd)), col(gd), rowtile, rowtile,
    ]
    scratch = [pltpu.VMEM((tm, D), BF16)] * 3 + [
        pltpu.VMEM((tm, lo), BF16), pltpu.VMEM((tm, lo), BF16), pltpu.VMEM((tm, gd), BF16)]
    if has_vres:
        v0, v1, v2 = v_res
        args += [_pad_cols(v1, lo), _pad_rows(v2, lo), row(v0), v_first]
        in_specs += [full((D, lo)), col(lo), rowtile, tile]
        scratch += [pltpu.VMEM((tm, lo), BF16)]
    return pl.pallas_call(
        functools.partial(_rwkv_proj_kernel, has_vres),
        grid=(T // tm, D // tn),
        in_specs=in_specs,
        out_specs=[tile] * 6,
        out_shape=[jax.ShapeDtypeStruct((T, D), dt) for dt in (BF16, BF16, BF16, BF16, F32, BF16)],
        scratch_shapes=scratch,
        compiler_params=_params("parallel", "arbitrary"),
        name="rwkv_proj",
    )(*args)


def _rwkv_scan_kernel(npairs, r_ref, k_ref, v_ref, a_ref, lw_ref, g_ref,
                      kk_ref, ka_ref, rk_ref, lg_ref, lb_ref, o_ref, s_ref):
    L = RWKV_CHUNK
    N = RWKV_HEAD
    W = 2 * N

    @pl.when(pl.program_id(1) == 0)
    def _():
        s_ref[...] = jnp.zeros_like(s_ref)

    t_i = lax.broadcasted_iota(jnp.int32, (L, W), 0)
    lane = lax.broadcasted_iota(jnp.int32, (L, W), 1)
    s_i = lane % N
    head0 = lane < N
    strict = s_i < t_i
    incl = s_i <= t_i
    same = [(t_i // b) == (s_i // b) for b in (8, 16, 32, L)]
    eye = jnp.where(s_i == t_i, 1.0, 0.0)
    brow = lax.broadcasted_iota(jnp.int32, (W, W), 0)
    blane = lax.broadcasted_iota(jnp.int32, (W, W), 1)
    bdiag = (brow // N) == (blane // N)

    def blk(x):
        return jnp.where(bdiag, jnp.concatenate([x, x], axis=0), 0.0)

    def segsum(x):
        s0 = jnp.sum(jnp.where(head0, x, 0.0), axis=1, keepdims=True)
        s1 = jnp.sum(jnp.where(head0, 0.0, x), axis=1, keepdims=True)
        return jnp.where(head0, s0, s1)

    def blk_b(x):
        return blk(x).astype(BF16)

    def mm(x, y):
        return _dot(x.astype(BF16), blk_b(y))

    tri_r = lax.broadcasted_iota(jnp.int32, (L, L), 0)
    tri_c = lax.broadcasted_iota(jnp.int32, (L, L), 1)
    tri = jnp.where(tri_c <= tri_r, 1.0, 0.0)
    c_all = jnp.dot(tri, lw_ref[...], precision=HI, preferred_element_type=F32)

    pairs = range(npairs)
    sls = [slice(p * W, (p + 1) * W) for p in pairs]

    def prep(sl):
        r = r_ref[:, sl].astype(F32)
        k = k_ref[:, sl].astype(F32)
        a = a_ref[:, sl].astype(F32)
        lw = lw_ref[:, sl]
        c = c_all[:, sl]
        kk = k * kk_ref[:, sl]
        kk = kk / jnp.maximum(jnp.sqrt(segsum(kk * kk)), 1e-12)
        kmod = k * (1.0 + (a - 1.0) * ka_ref[:, sl])
        alpha = -kk
        beta = kk * a
        c_mid = c[L // 2 - 1:L // 2, :]
        c_last = c[L - 1:L, :]
        e = c - c_mid
        ex_m = jnp.exp(-e)
        lhs = jnp.concatenate([alpha * jnp.exp(e - lw), r * jnp.exp(e)], axis=0).astype(BF16)
        rhs = jnp.concatenate([blk_b(beta * ex_m), blk_b(kmod * ex_m)], axis=0)
        to_end = jnp.exp(c_last - c)
        upd = jnp.concatenate([beta * to_end, kmod * to_end], axis=0).astype(BF16)
        return dict(lhs=lhs, rhs=rhs, upd=upd, r_abs=r * jnp.exp(c), al_abs=alpha * jnp.exp(c - lw),
                    p_last=jnp.exp(c_last), bonus=segsum(r * kmod * rk_ref[:, sl]))

    pre = [prep(sl) for sl in sls]
    amat = [_dot_nt(q["lhs"], q["rhs"]) for q in pre]
    n_ab = [jnp.where(strict, m[:L, :W], 0.0) for m in amat]
    a_rb = [jnp.where(incl, m[L:, :W], 0.0).astype(BF16) for m in amat]
    a_k = [jnp.concatenate([jnp.where(strict, m[:L, W:], 0.0), jnp.where(incl, m[L:, W:], 0.0)],
                           axis=0).astype(BF16) for m in amat]
    av = [_dot(x, blk_b(v_ref[:, sl])) for x, sl in zip(a_k, sls)]
    akv = [x[:L] for x in av]

    nd = [jnp.where(same[0], n, 0.0) for n in n_ab]
    tinv = [eye + n for n in nd]
    pw = [mm(n, n) for n in nd]
    both = [mm(jnp.concatenate([t, q], axis=0), q) for t, q in zip(tinv, pw)]
    tinv = [t + b[:L] for t, b in zip(tinv, both)]
    tinv = [t + mm(t, b[L:]) for t, b in zip(tinv, both)]
    for lvl in range(1, len(same)):
        off_mask = same[lvl] & jnp.logical_not(same[lvl - 1])
        inner = [mm(jnp.where(off_mask, n, 0.0), t) for n, t in zip(n_ab, tinv)]
        tinv = [t + mm(t, x) for t, x in zip(tinv, inner)]

    ta = [_dot(t.astype(BF16), jnp.concatenate([blk_b(q["al_abs"]), blk_b(x)], axis=1))
          for t, q, x in zip(tinv, pre, akv)]
    ra = [jnp.concatenate([q["r_abs"], t[:, :W]], axis=0).astype(BF16) for q, t in zip(pre, ta)]

    states = [s_ref[p] for p in pairs]
    su = [_dot(x, s.astype(BF16)) for x, s in zip(ra, states)]
    us = [x[L:] + t[:, W:] for x, t in zip(su, ta)]
    ys = [x[:L] + _dot(a, blk_b(u)) + y[L:] for x, a, u, y in zip(su, a_rb, us, av)]
    uv = [jnp.concatenate([u.astype(BF16), v_ref[:, sl]], axis=0) for u, sl in zip(us, sls)]
    z = [_dot_tn(q["upd"], x) for x, q in zip(uv, pre)]
    c_end = jnp.concatenate([c_all[L - 1:L, sl] for sl in sls]
                            + [jnp.zeros((W - npairs, W), F32)], axis=0).T
    for p in pairs:
        s_ref[p] = states[p] * jnp.exp(c_end[:, p:p + 1]) + jnp.where(bdiag, z[p], 0.0)

    for p in pairs:
        sl = sls[p]
        y = ys[p]
        mu = segsum(y) * (1.0 / N)
        yc = y - mu
        var = segsum(yc * yc) * (1.0 / N)
        out = yc * lax.rsqrt(var + RWKV_LNX_EPS) * lg_ref[:, sl] + lb_ref[:, sl]
        out = out + pre[p]["bonus"] * v_ref[:, sl].astype(F32)
        o_ref[:, sl] = (out * g_ref[:, sl].astype(F32)).astype(BF16)


def _rwkv_scan(r, k, v, a, lw, g, k_k, k_a, r_k, lnx_g, lnx_b, npairs=16):
    T, D = r.shape
    L = RWKV_CHUNK
    wl = npairs * 2 * RWKV_HEAD
    tile = pl.BlockSpec((L, wl), lambda hg, c: (c, hg))
    prow = pl.BlockSpec((1, wl), lambda hg, c: (0, hg))
    row = lambda x: x.reshape(1, D)
    return pl.pallas_call(
        functools.partial(_rwkv_scan_kernel, npairs),
        grid=(D // wl, T // L),
        in_specs=[tile] * 6 + [prow] * 5,
        out_specs=tile,
        out_shape=jax.ShapeDtypeStruct((T, D), BF16),
        scratch_shapes=[pltpu.VMEM((npairs, 2 * RWKV_HEAD, 2 * RWKV_HEAD), F32)],
        compiler_params=_params("parallel", "arbitrary"),
        name="rwkv_scan",
    )(r, k, v, a, lw, g, row(k_k), row(k_a), row(r_k), row(lnx_g), row(lnx_b))


def _rope_table_kernel(pos_ref, cos_ref, sin_ref):
    half = cos_ref.shape[1]
    idx = lax.broadcasted_iota(jnp.int32, (1, half), 1).astype(F32)
    freqs = jnp.exp(idx * (-jnp.log(ROPE_BASE) / half))
    ang = pos_ref[...].astype(F32) * freqs
    cos_ref[...] = jnp.cos(ang)
    sin_ref[...] = jnp.sin(ang)


def _rope_table(positions, half, tm=1024):
    T = positions.shape[0]
    return pl.pallas_call(
        _rope_table_kernel,
        grid=(T // tm,),
        in_specs=[pl.BlockSpec((tm, 1), lambda i: (i, 0))],
        out_specs=[pl.BlockSpec((tm, half), lambda i: (i, 0))] * 2,
        out_shape=[jax.ShapeDtypeStruct((T, half), F32)] * 2,
        compiler_params=_params("parallel"),
        name="rope_table",
    )(positions.reshape(T, 1))


def _retention_kernel(q_ref, k_ref, v_ref, g_ref, cos_ref, sin_ref, lg_ref, ln_ref, o_ref, st_ref):
    L = RET_CHUNK
    H = RET_HEADS
    dk = q_ref.shape[1] // H
    dv = v_ref.shape[1] // H
    half = dk // 2

    @pl.when(pl.program_id(0) == 0)
    def _():
        st_ref[...] = jnp.zeros_like(st_ref)

    cos = cos_ref[...]
    sin = sin_ref[...]

    def rope(ref, h):
        x1 = ref[:, h * dk:h * dk + half].astype(F32)
        x2 = ref[:, h * dk + half:(h + 1) * dk].astype(F32)
        return jnp.concatenate([x1 * cos - x2 * sin, x1 * sin + x2 * cos], axis=1)

    row = lax.broadcasted_iota(jnp.int32, (L, L), 0)
    col = lax.broadcasted_iota(jnp.int32, (L, L), 1)
    rel = (row - col).astype(F32)
    tcol = lax.broadcasted_iota(jnp.int32, (L, 1), 0).astype(F32)

    heads = range(H)
    lgs = [lg_ref[h] for h in heads]
    intra = [jnp.where(rel >= 0, jnp.exp(jnp.maximum(rel, 0.0) * lg), 0.0) for lg in lgs]
    lg1 = [lg[:, 0:1] for lg in lgs]
    qb = [rope(q_ref, h).astype(BF16) for h in heads]
    ks = [rope(k_ref, h) * (dk ** -0.5) for h in heads]
    vb = [v_ref[:, h * dv:(h + 1) * dv].astype(BF16) for h in heads]
    s = [(_dot_nt(q, k.astype(BF16)) * m).astype(BF16) for q, k, m in zip(qb, ks, intra)]
    states = [st_ref[h] for h in heads]
    o = [_dot(x, v) + _dot(q, st.astype(BF16)) * jnp.exp((tcol + 1.0) * lg)
         for x, v, q, st, lg in zip(s, vb, qb, states, lg1)]
    kz = [(k * jnp.exp((L - 1.0 - tcol) * lg)).astype(BF16) for k, lg in zip(ks, lg1)]
    upd = [_dot_tn(k, v) for k, v in zip(kz, vb)]
    for h in heads:
        st_ref[h] = jnp.exp(L * lg1[h]) * states[h] + upd[h]

    for h in heads:
        sl = slice(h * dv, (h + 1) * dv)
        mu = jnp.mean(o[h], axis=-1, keepdims=True)
        oc = o[h] - mu
        on = oc * lax.rsqrt(jnp.mean(oc * oc, axis=-1, keepdims=True) + RET_GN_EPS) * ln_ref[:, sl]
        g = g_ref[:, sl].astype(F32)
        o_ref[:, sl] = (g * jax.nn.sigmoid(g) * on).astype(BF16)


def _retention_core(proj, cos, sin, ln_g):
    T = proj.shape[0]
    H = RET_HEADS
    L = RET_CHUNK
    dk = cos.shape[1] * 2
    v_tot = ln_g.shape[0]
    dv = v_tot // H
    qk_tot = H * dk
    hs = jnp.arange(H, dtype=F32)
    log_gamma = jnp.broadcast_to(jnp.log(1.0 - 2.0 ** (-5.0 - hs))[:, None, None], (H, 1, L))
    return pl.pallas_call(
        _retention_kernel,
        grid=(T // L,),
        in_specs=[
            pl.BlockSpec((L, qk_tot), lambda c: (c, 0)),
            pl.BlockSpec((L, qk_tot), lambda c: (c, 1)),
            pl.BlockSpec((L, v_tot), lambda c: (c, (2 * qk_tot) // v_tot)),
            pl.BlockSpec((L, v_tot), lambda c: (c, (2 * qk_tot) // v_tot + 1)),
            pl.BlockSpec((L, dk // 2), lambda c: (c, 0)),
            pl.BlockSpec((L, dk // 2), lambda c: (c, 0)),
            pl.BlockSpec((H, 1, L), lambda c: (0, 0, 0)),
            pl.BlockSpec((1, v_tot), lambda c: (0, 0)),
        ],
        out_specs=pl.BlockSpec((L, v_tot), lambda c: (c, 0)),
        out_shape=jax.ShapeDtypeStruct((T, v_tot), BF16),
        scratch_shapes=[pltpu.VMEM((H, dk, dv), F32)],
        compiler_params=_params("arbitrary"),
        name="retention",
    )(proj, proj, proj, proj, cos, sin, log_gamma, ln_g.reshape(1, v_tot))


def _mlstm_kernel(q_ref, k_ref, v_ref, og_ref, gates_ref, gb_ref, mh_ref, o_ref, c_ref, n_ref, m_ref):
    L = ML_CHUNK
    H = ML_HEADS
    dqk = q_ref.shape[1] // H
    dv = v_ref.shape[1] // H

    @pl.when(pl.program_id(0) == 0)
    def _():
        c_ref[...] = jnp.zeros_like(c_ref)
        n_ref[...] = jnp.zeros_like(n_ref)
        m_ref[...] = jnp.zeros_like(m_ref)

    row = lax.broadcasted_iota(jnp.int32, (L, L), 0)
    col = lax.broadcasted_iota(jnp.int32, (L, L), 1)
    causal = col <= row

    gates = gates_ref[...] + gb_ref[...]
    glane = lax.broadcasted_iota(jnp.int32, gates.shape, 1)
    log_i = ML_IGATE_CAP * jnp.tanh(gates / ML_IGATE_CAP)
    log_f = jnp.minimum(gates, 0.0) - jnp.log1p(jnp.exp(-jnp.abs(gates)))
    act = jnp.where(glane < H, log_i, log_f)
    cum_f = jnp.dot(jnp.where(causal, 1.0, 0.0), act[:, H:], precision=HI, preferred_element_type=F32)
    cols = jnp.concatenate([act, cum_f, jnp.zeros((L, LANES - 3 * H), F32)], axis=1)
    rows = cols.T

    heads = range(H)
    ic = [cols[:, h:h + 1] for h in heads]
    b_col = [cols[:, 2 * H + h:2 * H + h + 1] for h in heads]
    i_row = [rows[h:h + 1, :] for h in heads]
    b_row = [rows[2 * H + h:2 * H + h + 1, :] for h in heads]
    b_last = [b[L - 1:L, :] for b in b_col]
    m_prev = [m_ref[h] for h in heads]
    log_inter = [b + m for b, m in zip(b_col, m_prev)]
    log_intra = [jnp.where(causal, bc - br + ir, -jnp.inf) for bc, br, ir in zip(b_col, b_row, i_row)]
    m_t = [jnp.maximum(x, jnp.max(y, axis=1, keepdims=True)) for x, y in zip(log_inter, log_intra)]
    log_s = [bl - bc + i for bl, bc, i in zip(b_last, b_col, ic)]
    m_new = [jnp.maximum(bl + m, jnp.max(s, axis=0, keepdims=True)) for bl, m, s in zip(b_last, m_prev, log_s)]
    gt = [dict(m_t=mt, w_inter=jnp.exp(x - mt), w_intra=jnp.exp(y - mt), m_new=mn,
               ws=jnp.exp(s - mn), carry=jnp.exp(bl + m - mn))
          for mt, x, y, mn, s, bl, m in zip(m_t, log_inter, log_intra, m_new, log_s, b_last, m_prev)]
    qs = [q_ref[:, h * dqk:(h + 1) * dqk].astype(F32) for h in heads]
    ks = [k_ref[:, h * dqk:(h + 1) * dqk].astype(F32) * (dqk ** -0.5) for h in heads]
    qb = [q.astype(BF16) for q in qs]
    vb = [v_ref[:, h * dv:(h + 1) * dv].astype(BF16) for h in heads]
    s = [_dot_nt(q, k.astype(BF16)) * t["w_intra"] for q, k, t in zip(qb, ks, gt)]
    c_state = [c_ref[h] for h in heads]
    n_state = [n_ref[h] for h in heads]
    num = [_dot(x.astype(BF16), v) + t["w_inter"] * _dot(q, c.astype(BF16))
           for x, v, t, q, c in zip(s, vb, gt, qb, c_state)]
    den = [jnp.sum(x, axis=1, keepdims=True) + t["w_inter"] * jnp.sum(q * n, axis=1, keepdims=True)
           for x, t, q, n in zip(s, gt, qs, n_state)]
    kw = [k * t["ws"] for k, t in zip(ks, gt)]
    upd = [_dot_tn(k.astype(BF16), v) for k, v in zip(kw, vb)]
    for h in heads:
        c_ref[h] = gt[h]["carry"] * c_state[h] + upd[h]
        n_ref[h] = gt[h]["carry"] * n_state[h] + jnp.sum(kw[h], axis=0, keepdims=True)
        m_ref[h] = gt[h]["m_new"]

    for h in heads:
        sl = slice(h * dv, (h + 1) * dv)
        hid = num[h] / jnp.maximum(jnp.abs(den[h]), jnp.exp(-gt[h]["m_t"]))
        hn = hid * lax.rsqrt(jnp.mean(hid * hid, axis=-1, keepdims=True) + ML_NORM_EPS) * mh_ref[:, sl]
        o_ref[:, sl] = (jax.nn.sigmoid(og_ref[:, sl].astype(F32)) * hn).astype(BF16)


def _mlstm_core(proj, gates, gate_b, mh_g):
    T = proj.shape[0]
    H = ML_HEADS
    L = ML_CHUNK
    v_tot = mh_g.shape[0]
    dv = v_tot // H
    qk_tot = (proj.shape[1] - 2 * v_tot) // 2
    dqk = qk_tot // H
    return pl.pallas_call(
        _mlstm_kernel,
        grid=(T // L,),
        in_specs=[
            pl.BlockSpec((L, qk_tot), lambda c: (c, 0)),
            pl.BlockSpec((L, qk_tot), lambda c: (c, 1)),
            pl.BlockSpec((L, v_tot), lambda c: (c, (2 * qk_tot) // v_tot)),
            pl.BlockSpec((L, v_tot), lambda c: (c, (2 * qk_tot) // v_tot + 1)),
            pl.BlockSpec((L, 2 * H), lambda c: (c, 0)),
            pl.BlockSpec((1, 2 * H), lambda c: (0, 0)),
            pl.BlockSpec((1, v_tot), lambda c: (0, 0)),
        ],
        out_specs=pl.BlockSpec((L, v_tot), lambda c: (c, 0)),
        out_shape=jax.ShapeDtypeStruct((T, v_tot), BF16),
        scratch_shapes=[pltpu.VMEM((H, dqk, dv), F32), pltpu.VMEM((H, 1, dqk), F32),
                        pltpu.VMEM((H, 1, 1), F32)],
        compiler_params=_params("arbitrary"),
        name="mlstm",
    )(proj, proj, proj, proj, gates, gate_b.reshape(1, 2 * H), mh_g.reshape(1, v_tot))


def _gate_proj_kernel(x_ref, g_ref, w_ref, o_ref):
    o_ref[...] = _dot(_rms(x_ref[...], g_ref[...]).astype(BF16), w_ref[...])


def _gate_proj(h, gain, w, tm=512):
    T, D = h.shape
    n = w.shape[1]
    return pl.pallas_call(
        _gate_proj_kernel,
        grid=(T // tm,),
        in_specs=[
            pl.BlockSpec((tm, D), lambda i: (i, 0)),
            pl.BlockSpec((1, D), lambda i: (0, 0)),
            pl.BlockSpec((D, n), lambda i: (0, 0)),
        ],
        out_specs=pl.BlockSpec((tm, n), lambda i: (i, 0)),
        out_shape=jax.ShapeDtypeStruct((T, n), F32),
        compiler_params=_params("parallel"),
        name="gate_proj",
    )(h, gain.reshape(1, D), w)


def kernel(x, p, positions, norm_g, final_g, ffn_in, ffn_out, ple_proj, ple_gate, rwkv_lerp, rwkv_w0, rwkv_w1, rwkv_w2, rwkv_a0, rwkv_a1, rwkv_a2, rwkv_g1, rwkv_g2, rwkv_kk, rwkv_ka, rwkv_rk, rwkv_w_rkv, rwkv_w_o, rwkv_lnx_g, rwkv_lnx_b, rwkv_v0, rwkv_v1, rwkv_v2, ret_w_in, ret_ln_g, ret_w_o, ml_w_in, ml_gate_b, ml_mh_g, ml_w_o):
    B, T, D = x.shape
    depth = norm_g.shape[0]
    bf = lambda w: w.astype(BF16)
    ffn_in_b, ffn_out_b = bf(ffn_in), bf(ffn_out)
    ple_proj_b, ple_gate_b = bf(ple_proj), bf(ple_gate)
    rwkv_w_rkv_b, rwkv_w_o_b = bf(rwkv_w_rkv), bf(rwkv_w_o)
    ret_w_in_b, ret_w_o_b = bf(ret_w_in), bf(ret_w_o)
    ml_w_in_b, ml_w_o_b = bf(ml_w_in), bf(ml_w_o)
    outs = []
    for b in range(B):
        h = x[b]
        v_first = None
        for i in range(depth):
            kind, j = i % 3, i // 3
            h = _ffn(h, norm_g[i, 0], ffn_in, ffn_out, (i, 0))
            if kind == 0:
                v_res = None
                if j > 0:
                    v_res = (rwkv_v0[j - 1], bf(rwkv_v1[j - 1]), bf(rwkv_v2[j - 1]))
                r, k, v, a, lw, g = _rwkv_proj(
                    h, norm_g[i, 1], rwkv_lerp[j], rwkv_w_rkv_b, (j,), bf(rwkv_w1[j]), bf(rwkv_w2[j]),
                    bf(rwkv_a1[j]), bf(rwkv_a2[j]), bf(rwkv_g1[j]), bf(rwkv_g2[j]),
                    rwkv_w0[j], rwkv_a0[j], v_res, v_first)
                if j == 0:
                    v_first = v
                mix_in = _rwkv_scan(r, k, v, a, lw, g, rwkv_kk[j], rwkv_ka[j], rwkv_rk[j].reshape(D),
                                    rwkv_lnx_g[j], rwkv_lnx_b[j])
                w_o = rwkv_w_o_b
            elif kind == 1:
                proj = _norm_matmul(h, norm_g[i, 1], ret_w_in_b, (j,), ret_w_in.shape[2])
                half = ret_w_in.shape[2] // 6 // RET_HEADS // 2
                cos, sin = _rope_table(positions[b], half)
                mix_in = _retention_core(proj, cos, sin, ret_ln_g[j])
                w_o = ret_w_o_b
            else:
                n_main = ml_w_in.shape[2] - 2 * ML_HEADS
                proj = _norm_matmul(h, norm_g[i, 1], ml_w_in_b, (j,), n_main)
                gates = _gate_proj(h, norm_g[i, 1], ml_w_in_b[j][:, n_main:])
                mix_in = _mlstm_core(proj, gates, ml_gate_b[j], ml_mh_g[j])
                w_o = ml_w_o_b
            h = _matmul_residual(h, mix_in, w_o, (j,))
            h = _ffn(h, norm_g[i, 2], ffn_in, ffn_out, (i, 1))
            h = _ple(h, norm_g[i, 3], p[i, b], ple_proj_b, ple_gate_b, (i,), final_g, i == depth - 1)
        outs.append(h)
    return outs[0][None] if B == 1 else jnp.stack(outs, axis=0)
```

```python
import functools
import math

import jax
import jax.numpy as jnp
from jax import lax
from jax.experimental import pallas as pl
from jax.experimental.pallas import tpu as pltpu

F32 = jnp.float32
BF16 = jnp.bfloat16

NORM_EPS = 1e-6
RWKV_HEAD = 64
RWKV_LNX_EPS = 64e-5
RWKV_CHUNK = 64
RET_HEADS = 8
RET_CHUNK = 128
RET_GN_EPS = 1e-6
ROPE_BASE = 10000.0
ML_HEADS = 8
ML_CHUNK = 64
ML_IGATE_CAP = 15.0
ML_NORM_EPS = 1e-6

LANES = 128
SUBLANES = 8
VMEM_LIMIT_BYTES = 60 * 1024 * 1024
HI = lax.Precision.HIGHEST


def _params(*sem):
    return pltpu.CompilerParams(dimension_semantics=sem, vmem_limit_bytes=VMEM_LIMIT_BYTES)


def _rms(x, gain):
    return x * lax.rsqrt(jnp.mean(x * x, axis=-1, keepdims=True) + NORM_EPS) * gain


def _wspec(block, lead, tail):
    return pl.BlockSpec((None,) * len(lead) + tuple(block), lambda *g: tuple(lead) + tuple(tail(*g)))


def _dot(a, b):
    return jnp.dot(a, b, preferred_element_type=F32)


def _dot_nt(a, b):
    return lax.dot_general(a, b, (((1,), (1,)), ((), ())), preferred_element_type=F32)


def _dot_tn(a, b):
    return lax.dot_general(a, b, (((0,), (0,)), ((), ())), preferred_element_type=F32)


def _ffn_kernel(nf, x_ref, g_ref, wg_ref, wu_ref, wo_ref, o_ref, xn_ref):
    f = pl.program_id(1)

    @pl.when(f == 0)
    def _():
        xn_ref[...] = _rms(x_ref[...], g_ref[...]).astype(BF16)
        o_ref[...] = jnp.zeros_like(o_ref)

    xn = xn_ref[...]
    gate = _dot(xn, wg_ref[...].astype(BF16))
    up = _dot(xn, wu_ref[...].astype(BF16))
    act = (gate * jax.nn.sigmoid(gate) * up).astype(BF16)
    o_ref[...] += _dot(act, wo_ref[...].astype(BF16))

    @pl.when(f == nf - 1)
    def _():
        o_ref[...] = x_ref[...] + 0.5 * o_ref[...]


def _ffn(h, gain, w_in, w_out, lead, tm=1024, tf=256):
    T, D = h.shape
    F = w_out.shape[-2]
    nf = F // tf
    return pl.pallas_call(
        functools.partial(_ffn_kernel, nf),
        grid=(T // tm, nf),
        in_specs=[
            pl.BlockSpec((tm, D), lambda i, f: (i, 0)),
            pl.BlockSpec((1, D), lambda i, f: (0, 0)),
            _wspec((D, tf), lead, lambda i, f: (0, f)),
            _wspec((D, tf), lead, lambda i, f: (0, nf + f)),
            _wspec((tf, D), lead, lambda i, f: (f, 0)),
        ],
        out_specs=pl.BlockSpec((tm, D), lambda i, f: (i, 0)),
        out_shape=jax.ShapeDtypeStruct((T, D), F32),
        scratch_shapes=[pltpu.VMEM((tm, D), BF16)],
        compiler_params=_params("parallel", "arbitrary"),
        name="ffn",
    )(h, gain.reshape(1, D), w_in, w_in, w_out)


def _norm_matmul_kernel(x_ref, g_ref, w_ref, o_ref, xn_ref):
    @pl.when(pl.program_id(1) == 0)
    def _():
        xn_ref[...] = _rms(x_ref[...], g_ref[...]).astype(BF16)

    o_ref[...] = _dot(xn_ref[...], w_ref[...]).astype(o_ref.dtype)


def _norm_matmul(h, gain, w, lead, N, tm=1024, tn=2048):
    T, D = h.shape
    return pl.pallas_call(
        _norm_matmul_kernel,
        grid=(T // tm, N // tn),
        in_specs=[
            pl.BlockSpec((tm, D), lambda i, j: (i, 0)),
            pl.BlockSpec((1, D), lambda i, j: (0, 0)),
            _wspec((D, tn), lead, lambda i, j: (0, j)),
        ],
        out_specs=pl.BlockSpec((tm, tn), lambda i, j: (i, j)),
        out_shape=jax.ShapeDtypeStruct((T, N), BF16),
        scratch_shapes=[pltpu.VMEM((tm, D), BF16)],
        compiler_params=_params("parallel", "arbitrary"),
        name="norm_matmul",
    )(h, gain.reshape(1, D), w)


def _matmul_residual_kernel(h_ref, a_ref, w_ref, o_ref):
    o_ref[...] = h_ref[...] + _dot(a_ref[...], w_ref[...])


def _matmul_residual(h, a, w, lead, tm=1024, tn=1024):
    T, D = h.shape
    K = a.shape[1]
    return pl.pallas_call(
        _matmul_residual_kernel,
        grid=(D // tn, T // tm),
        in_specs=[
            pl.BlockSpec((tm, tn), lambda j, i: (i, j)),
            pl.BlockSpec((tm, K), lambda j, i: (i, 0)),
            _wspec((K, tn), lead, lambda j, i: (0, j)),
        ],
        out_specs=pl.BlockSpec((tm, tn), lambda j, i: (i, j)),
        out_shape=jax.ShapeDtypeStruct((T, D), F32),
        compiler_params=_params("parallel", "parallel"),
        name="matmul_residual",
    )(h, a, w)


def _ple_kernel(final, h_ref, g_ref, p_ref, wp_ref, wg_ref, fg_ref, o_ref):
    h = h_ref[...]
    gate = jax.nn.sigmoid(_dot(_rms(h, g_ref[...]).astype(BF16), wg_ref[...]))
    out = h + _dot(p_ref[...].astype(BF16), wp_ref[...]) * gate
    if final:
        out = _rms(out, fg_ref[...])
    o_ref[...] = out


def _ple(h, gain, p, w_proj, w_gate, lead, final_gain, final, tm=512):
    T, D = h.shape
    P = p.shape[1]
    return pl.pallas_call(
        functools.partial(_ple_kernel, final),
        grid=(T // tm,),
        in_specs=[
            pl.BlockSpec((tm, D), lambda i: (i, 0)),
            pl.BlockSpec((1, D), lambda i: (0, 0)),
            pl.BlockSpec((tm, P), lambda i: (i, 0)),
            _wspec((P, D), lead, lambda i: (0, 0)),
            _wspec((D, D), lead, lambda i: (0, 0)),
            pl.BlockSpec((1, D), lambda i: (0, 0)),
        ],
        out_specs=pl.BlockSpec((tm, D), lambda i: (i, 0)),
        out_shape=jax.ShapeDtypeStruct((T, D), F32),
        compiler_params=_params("parallel"),
        name="ple",
    )(h, gain.reshape(1, D), p, w_proj, w_gate, final_gain.reshape(1, D))


def _rwkv_proj_kernel(has_vres, *refs):
    if has_vres:
        (h_ref, halo_ref, g_ref, lerp_ref, wr_ref, wk_ref, wv_ref, w1_ref, w2_ref, a1_ref, a2_ref,
         g1_ref, g2_ref, w0_ref, a0_ref, v1_ref, v2_ref, v0_ref, vf_ref,
         r_out, k_out, v_out, a_out, lw_out, g_out,
         xr_s, xk_s, xv_s, hw_s, ha_s, hg_s, hv_s) = refs
    else:
        (h_ref, halo_ref, g_ref, lerp_ref, wr_ref, wk_ref, wv_ref, w1_ref, w2_ref, a1_ref, a2_ref,
         g1_ref, g2_ref, w0_ref, a0_ref,
         r_out, k_out, v_out, a_out, lw_out, g_out,
         xr_s, xk_s, xv_s, hw_s, ha_s, hg_s) = refs
    i = pl.program_id(0)

    @pl.when(pl.program_id(1) == 0)
    def _():
        gain = g_ref[...]
        u = _rms(h_ref[...], gain)
        halo = _rms(halo_ref[...], gain)
        first = jnp.where(i > 0, halo[SUBLANES - 1:SUBLANES, :], 0.0)
        row = lax.broadcasted_iota(jnp.int32, u.shape, 0)
        u_prev = jnp.where(row == 0, first, pltpu.roll(u, 1, 0))
        xx = u_prev - u
        lerp = lerp_ref[...]
        xr_s[...] = (u + xx * lerp[0:1]).astype(BF16)
        xw = (u + xx * lerp[1:2]).astype(BF16)
        xk_s[...] = (u + xx * lerp[2:3]).astype(BF16)
        xv = (u + xx * lerp[3:4]).astype(BF16)
        xv_s[...] = xv
        xa = (u + xx * lerp[4:5]).astype(BF16)
        xg = (u + xx * lerp[5:6]).astype(BF16)
        hw_s[...] = jnp.tanh(_dot(xw, w1_ref[...])).astype(BF16)
        ha_s[...] = _dot(xa, a1_ref[...]).astype(BF16)
        hg_s[...] = jax.nn.sigmoid(_dot(xg, g1_ref[...])).astype(BF16)
        if has_vres:
            hv_s[...] = _dot(xv, v1_ref[...]).astype(BF16)

    r_out[...] = _dot(xr_s[...], wr_ref[...])
    k_out[...] = _dot(xk_s[...], wk_ref[...])
    v = _dot(xv_s[...], wv_ref[...])
    if has_vres:
        mix = jax.nn.sigmoid(v0_ref[...] + _dot(hv_s[...], v2_ref[...]))
        v = v + (vf_ref[...] - v) * mix
    v_out[...] = v
    z = w0_ref[...] + _dot(hw_s[...], w2_ref[...])
    lw_out[...] = -math.exp(-0.5) * jax.nn.sigmoid(z)
    a_out[...] = jax.nn.sigmoid(a0_ref[...] + _dot(ha_s[...], a2_ref[...]))
    g_out[...] = _dot(hg_s[...], g2_ref[...])


def _pad_cols(w, n):
    return jnp.pad(w, ((0, 0), (0, n - w.shape[1])))


def _pad_rows(w, n):
    return jnp.pad(w, ((0, n - w.shape[0]), (0, 0)))


def _rwkv_proj(h, gain, lerp, w_rkv, lead, w1, w2, a1, a2, g1, g2, w0, a0, v_res, v_first, tm=512, tn=512):
    T, D = h.shape
    has_vres = v_res is not None
    lo = LANES
    row = lambda x: x.reshape(1, D)
    full = lambda shape: pl.BlockSpec(shape, lambda i, j: (0, 0))
    col = lambda k: pl.BlockSpec((k, tn), lambda i, j: (0, j))
    tile = pl.BlockSpec((tm, tn), lambda i, j: (i, j))
    rowtile = pl.BlockSpec((1, tn), lambda i, j: (0, j))
    hb = tm // SUBLANES
    gd = g1.shape[1]
    args = [h, h, row(gain), _pad_rows(lerp, SUBLANES),
            w_rkv, w_rkv, w_rkv,
            _pad_cols(w1, lo), _pad_rows(w2, lo), _pad_cols(a1, lo), _pad_rows(a2, lo),
            g1, g2, row(w0), row(a0)]
    in_specs = [
        pl.BlockSpec((tm, D), lambda i, j: (i, 0)),
        pl.BlockSpec((SUBLANES, D), lambda i, j: (jnp.maximum(i * hb - 1, 0), 0)),
        full((1, D)), full((SUBLANES, D)),
        _wspec((D, tn), lead + (0,), lambda i, j: (0, j)),
        _wspec((D, tn), lead + (1,), lambda i, j: (0, j)),
        _wspec((D, tn), lead + (2,), lambda i, j: (0, j)),
        full((D, lo)), col(lo), full((D, lo)), col(lo),
        full((D, gd)), col(gd), rowtile, rowtile,
    ]
    scratch = [pltpu.VMEM((tm, D), BF16)] * 3 + [
        pltpu.VMEM((tm, lo), BF16), pltpu.VMEM((tm, lo), BF16), pltpu.VMEM((tm, gd), BF16)]
    if has_vres:
        v0, v1, v2 = v_res
        args += [_pad_cols(v1, lo), _pad_rows(v2, lo), row(v0), v_first]
        in_specs += [full((D, lo)), col(lo), rowtile, tile]
        scratch += [pltpu.VMEM((tm, lo), BF16)]
    return pl.pallas_call(
        functools.partial(_rwkv_proj_kernel, has_vres),
        grid=(T // tm, D // tn),
        in_specs=in_specs,
        out_specs=[tile] * 6,
        out_shape=[jax.ShapeDtypeStruct((T, D), F32)] * 6,
        scratch_shapes=scratch,
        compiler_params=_params("parallel", "arbitrary"),
        name="rwkv_proj",
    )(*args)


def _rwkv_scan_kernel(npairs, r_ref, k_ref, v_ref, a_ref, lw_ref, g_ref,
                      kk_ref, ka_ref, rk_ref, lg_ref, lb_ref, o_ref, s_ref):
    L = RWKV_CHUNK
    N = RWKV_HEAD
    W = 2 * N

    @pl.when(pl.program_id(1) == 0)
    def _():
        s_ref[...] = jnp.zeros_like(s_ref)

    t_i = lax.broadcasted_iota(jnp.int32, (L, W), 0)
    lane = lax.broadcasted_iota(jnp.int32, (L, W), 1)
    s_i = lane % N
    head0 = lane < N
    strict = s_i < t_i
    incl = s_i <= t_i
    same = [(t_i // b) == (s_i // b) for b in (8, 16, 32, L)]
    eye = jnp.where(s_i == t_i, 1.0, 0.0)
    brow = lax.broadcasted_iota(jnp.int32, (W, W), 0)
    blane = lax.broadcasted_iota(jnp.int32, (W, W), 1)
    bdiag = (brow // N) == (blane // N)

    def blk(x):
        return jnp.where(bdiag, jnp.concatenate([x, x], axis=0), 0.0)

    def segsum(x):
        s0 = jnp.sum(jnp.where(head0, x, 0.0), axis=1, keepdims=True)
        s1 = jnp.sum(jnp.where(head0, 0.0, x), axis=1, keepdims=True)
        return jnp.where(head0, s0, s1)

    def blk_b(x):
        return blk(x).astype(BF16)

    def mm(x, y):
        return _dot(x.astype(BF16), blk_b(y))

    tri_r = lax.broadcasted_iota(jnp.int32, (L, L), 0)
    tri_c = lax.broadcasted_iota(jnp.int32, (L, L), 1)
    tri = jnp.where(tri_c <= tri_r, 1.0, 0.0)
    c_all = jnp.dot(tri, lw_ref[...], precision=HI, preferred_element_type=F32)

    pairs = range(npairs)
    sls = [slice(p * W, (p + 1) * W) for p in pairs]

    def prep(sl):
        r = r_ref[:, sl]
        k = k_ref[:, sl]
        a = a_ref[:, sl]
        lw = lw_ref[:, sl]
        c = c_all[:, sl]
        kk = k * kk_ref[:, sl]
        kk = kk / jnp.maximum(jnp.sqrt(segsum(kk * kk)), 1e-12)
        kmod = k * (1.0 + (a - 1.0) * ka_ref[:, sl])
        alpha = -kk
        beta = kk * a
        c_mid = c[L // 2 - 1:L // 2, :]
        c_last = c[L - 1:L, :]
        e = c - c_mid
        ex_m = jnp.exp(-e)
        lhs = jnp.concatenate([alpha * jnp.exp(e - lw), r * jnp.exp(e)], axis=0).astype(BF16)
        rhs = jnp.concatenate([blk_b(beta * ex_m), blk_b(kmod * ex_m)], axis=0)
        to_end = jnp.exp(c_last - c)
        upd = jnp.concatenate([beta * to_end, kmod * to_end], axis=0).astype(BF16)
        return dict(lhs=lhs, rhs=rhs, upd=upd, r_abs=r * jnp.exp(c), al_abs=alpha * jnp.exp(c - lw),
                    p_last=jnp.exp(c_last), bonus=segsum(r * kmod * rk_ref[:, sl]))

    pre = [prep(sl) for sl in sls]
    amat = [_dot_nt(q["lhs"], q["rhs"]) for q in pre]
    n_ab = [jnp.where(strict, m[:L, :W], 0.0) for m in amat]
    a_rb = [jnp.where(incl, m[L:, :W], 0.0).astype(BF16) for m in amat]
    a_k = [jnp.concatenate([jnp.where(strict, m[:L, W:], 0.0), jnp.where(incl, m[L:, W:], 0.0)],
                           axis=0).astype(BF16) for m in amat]
    av = [_dot(x, blk_b(v_ref[:, sl])) for x, sl in zip(a_k, sls)]
    akv = [x[:L] for x in av]

    nd = [jnp.where(same[0], n, 0.0) for n in n_ab]
    tinv = [eye + n for n in nd]
    pw = [mm(n, n) for n in nd]
    both = [mm(jnp.concatenate([t, q], axis=0), q) for t, q in zip(tinv, pw)]
    tinv = [t + b[:L] for t, b in zip(tinv, both)]
    tinv = [t + mm(t, b[L:]) for t, b in zip(tinv, both)]
    for lvl in range(1, len(same)):
        off_mask = same[lvl] & jnp.logical_not(same[lvl - 1])
        inner = [mm(jnp.where(off_mask, n, 0.0), t) for n, t in zip(n_ab, tinv)]
        tinv = [t + mm(t, x) for t, x in zip(tinv, inner)]

    ta = [_dot(t.astype(BF16), jnp.concatenate([blk_b(q["al_abs"]), blk_b(x)], axis=1))
          for t, q, x in zip(tinv, pre, akv)]
    ra = [jnp.concatenate([q["r_abs"], t[:, :W]], axis=0).astype(BF16) for q, t in zip(pre, ta)]

    states = [s_ref[p] for p in pairs]
    su = [_dot(x, s.astype(BF16)) for x, s in zip(ra, states)]
    us = [x[L:] + t[:, W:] for x, t in zip(su, ta)]
    ys = [x[:L] + _dot(a, blk_b(u)) + y[L:] for x, a, u, y in zip(su, a_rb, us, av)]
    uv = [jnp.concatenate([u, v_ref[:, sl]], axis=0).astype(BF16) for u, sl in zip(us, sls)]
    z = [_dot_tn(q["upd"], x) for x, q in zip(uv, pre)]
    c_end = jnp.concatenate([c_all[L - 1:L, sl] for sl in sls]
                            + [jnp.zeros((W - npairs, W), F32)], axis=0).T
    for p in pairs:
        s_ref[p] = states[p] * jnp.exp(c_end[:, p:p + 1]) + jnp.where(bdiag, z[p], 0.0)

    for p in pairs:
        sl = sls[p]
        y = ys[p]
        mu = segsum(y) * (1.0 / N)
        yc = y - mu
        var = segsum(yc * yc) * (1.0 / N)
        out = yc * lax.rsqrt(var + RWKV_LNX_EPS) * lg_ref[:, sl] + lb_ref[:, sl]
        out = out + pre[p]["bonus"] * v_ref[:, sl]
        o_ref[:, sl] = (out * g_ref[:, sl]).astype(BF16)


def _rwkv_scan(r, k, v, a, lw, g, k_k, k_a, r_k, lnx_g, lnx_b, npairs=16):
    T, D = r.shape
    L = RWKV_CHUNK
    wl = npairs * 2 * RWKV_HEAD
    tile = pl.BlockSpec((L, wl), lambda hg, c: (c, hg))
    prow = pl.BlockSpec((1, wl), lambda hg, c: (0, hg))
    row = lambda x: x.reshape(1, D)
    return pl.pallas_call(
        functools.partial(_rwkv_scan_kernel, npairs),
        grid=(D // wl, T // L),
        in_specs=[tile] * 6 + [prow] * 5,
        out_specs=tile,
        out_shape=jax.ShapeDtypeStruct((T, D), BF16),
        scratch_shapes=[pltpu.VMEM((npairs, 2 * RWKV_HEAD, 2 * RWKV_HEAD), F32)],
        compiler_params=_params("parallel", "arbitrary"),
        name="rwkv_scan",
    )(r, k, v, a, lw, g, row(k_k), row(k_a), row(r_k), row(lnx_g), row(lnx_b))


def _rope_table_kernel(pos_ref, cos_ref, sin_ref):
    half = cos_ref.shape[1]
    idx = lax.broadcasted_iota(jnp.int32, (1, half), 1).astype(F32)
    freqs = jnp.exp(idx * (-jnp.log(ROPE_BASE) / half))
    ang = pos_ref[...].astype(F32) * freqs
    cos_ref[...] = jnp.cos(ang)
    sin_ref[...] = jnp.sin(ang)


def _rope_table(positions, half, tm=1024):
    T = positions.shape[0]
    return pl.pallas_call(
        _rope_table_kernel,
        grid=(T // tm,),
        in_specs=[pl.BlockSpec((tm, 1), lambda i: (i, 0))],
        out_specs=[pl.BlockSpec((tm, half), lambda i: (i, 0))] * 2,
        out_shape=[jax.ShapeDtypeStruct((T, half), F32)] * 2,
        compiler_params=_params("parallel"),
        name="rope_table",
    )(positions.reshape(T, 1))


def _retention_kernel(q_ref, k_ref, v_ref, g_ref, cos_ref, sin_ref, lg_ref, ln_ref, o_ref, st_ref):
    L = RET_CHUNK
    H = RET_HEADS
    dk = q_ref.shape[1] // H
    dv = v_ref.shape[1] // H
    half = dk // 2

    @pl.when(pl.program_id(0) == 0)
    def _():
        st_ref[...] = jnp.zeros_like(st_ref)

    cos = cos_ref[...]
    sin = sin_ref[...]

    def rope(ref, h):
        x1 = ref[:, h * dk:h * dk + half].astype(F32)
        x2 = ref[:, h * dk + half:(h + 1) * dk].astype(F32)
        return jnp.concatenate([x1 * cos - x2 * sin, x1 * sin + x2 * cos], axis=1)

    row = lax.broadcasted_iota(jnp.int32, (L, L), 0)
    col = lax.broadcasted_iota(jnp.int32, (L, L), 1)
    rel = (row - col).astype(F32)
    tcol = lax.broadcasted_iota(jnp.int32, (L, 1), 0).astype(F32)

    heads = range(H)
    lgs = [lg_ref[h] for h in heads]
    intra = [jnp.where(rel >= 0, jnp.exp(jnp.maximum(rel, 0.0) * lg), 0.0) for lg in lgs]
    lg1 = [lg[:, 0:1] for lg in lgs]
    qb = [rope(q_ref, h).astype(BF16) for h in heads]
    ks = [rope(k_ref, h) * (dk ** -0.5) for h in heads]
    vb = [v_ref[:, h * dv:(h + 1) * dv].astype(BF16) for h in heads]
    s = [(_dot_nt(q, k.astype(BF16)) * m).astype(BF16) for q, k, m in zip(qb, ks, intra)]
    states = [st_ref[h] for h in heads]
    o = [_dot(x, v) + _dot(q, st.astype(BF16)) * jnp.exp((tcol + 1.0) * lg)
         for x, v, q, st, lg in zip(s, vb, qb, states, lg1)]
    kz = [(k * jnp.exp((L - 1.0 - tcol) * lg)).astype(BF16) for k, lg in zip(ks, lg1)]
    upd = [_dot_tn(k, v) for k, v in zip(kz, vb)]
    for h in heads:
        st_ref[h] = jnp.exp(L * lg1[h]) * states[h] + upd[h]

    for h in heads:
        sl = slice(h * dv, (h + 1) * dv)
        mu = jnp.mean(o[h], axis=-1, keepdims=True)
        oc = o[h] - mu
        on = oc * lax.rsqrt(jnp.mean(oc * oc, axis=-1, keepdims=True) + RET_GN_EPS) * ln_ref[:, sl]
        g = g_ref[:, sl].astype(F32)
        o_ref[:, sl] = (g * jax.nn.sigmoid(g) * on).astype(BF16)


def _retention_core(proj, cos, sin, ln_g):
    T = proj.shape[0]
    H = RET_HEADS
    L = RET_CHUNK
    dk = cos.shape[1] * 2
    v_tot = ln_g.shape[0]
    dv = v_tot // H
    qk_tot = H * dk
    hs = jnp.arange(H, dtype=F32)
    log_gamma = jnp.broadcast_to(jnp.log(1.0 - 2.0 ** (-5.0 - hs))[:, None, None], (H, 1, L))
    return pl.pallas_call(
        _retention_kernel,
        grid=(T // L,),
        in_specs=[
            pl.BlockSpec((L, qk_tot), lambda c: (c, 0)),
            pl.BlockSpec((L, qk_tot), lambda c: (c, 1)),
            pl.BlockSpec((L, v_tot), lambda c: (c, (2 * qk_tot) // v_tot)),
            pl.BlockSpec((L, v_tot), lambda c: (c, (2 * qk_tot) // v_tot + 1)),
            pl.BlockSpec((L, dk // 2), lambda c: (c, 0)),
            pl.BlockSpec((L, dk // 2), lambda c: (c, 0)),
            pl.BlockSpec((H, 1, L), lambda c: (0, 0, 0)),
            pl.BlockSpec((1, v_tot), lambda c: (0, 0)),
        ],
        out_specs=pl.BlockSpec((L, v_tot), lambda c: (c, 0)),
        out_shape=jax.ShapeDtypeStruct((T, v_tot), BF16),
        scratch_shapes=[pltpu.VMEM((H, dk, dv), F32)],
        compiler_params=_params("arbitrary"),
        name="retention",
    )(proj, proj, proj, proj, cos, sin, log_gamma, ln_g.reshape(1, v_tot))


def _mlstm_kernel(q_ref, k_ref, v_ref, og_ref, gates_ref, gb_ref, mh_ref, o_ref, c_ref, n_ref, m_ref):
    L = ML_CHUNK
    H = ML_HEADS
    dqk = q_ref.shape[1] // H
    dv = v_ref.shape[1] // H

    @pl.when(pl.program_id(0) == 0)
    def _():
        c_ref[...] = jnp.zeros_like(c_ref)
        n_ref[...] = jnp.zeros_like(n_ref)
        m_ref[...] = jnp.zeros_like(m_ref)

    row = lax.broadcasted_iota(jnp.int32, (L, L), 0)
    col = lax.broadcasted_iota(jnp.int32, (L, L), 1)
    causal = col <= row

    def chunk(r0, c_state, n_state, m_prev):
        rs = slice(r0, r0 + L)
        gates = gates_ref[rs, :] + gb_ref[...]
        glane = lax.broadcasted_iota(jnp.int32, gates.shape, 1)
        log_i = ML_IGATE_CAP * jnp.tanh(gates / ML_IGATE_CAP)
        log_f = jnp.minimum(gates, 0.0) - jnp.log1p(jnp.exp(-jnp.abs(gates)))
        act = jnp.where(glane < H, log_i, log_f)
        cum_f = jnp.dot(jnp.where(causal, 1.0, 0.0), act[:, H:], precision=HI, preferred_element_type=F32)
        cols = jnp.concatenate([act, cum_f, jnp.zeros((L, LANES - 3 * H), F32)], axis=1)
        rows = cols.T

        ic = [cols[:, h:h + 1] for h in heads]
        b_col = [cols[:, 2 * H + h:2 * H + h + 1] for h in heads]
        i_row = [rows[h:h + 1, :] for h in heads]
        b_row = [rows[2 * H + h:2 * H + h + 1, :] for h in heads]
        b_last = [b[L - 1:L, :] for b in b_col]
        log_inter = [b + m for b, m in zip(b_col, m_prev)]
        log_intra = [jnp.where(causal, bc - br + ir, -jnp.inf) for bc, br, ir in zip(b_col, b_row, i_row)]
        m_t = [jnp.maximum(x, jnp.max(y, axis=1, keepdims=True)) for x, y in zip(log_inter, log_intra)]
        log_s = [bl - bc + i for bl, bc, i in zip(b_last, b_col, ic)]
        m_new = [jnp.maximum(bl + m, jnp.max(s, axis=0, keepdims=True)) for bl, m, s in zip(b_last, m_prev, log_s)]
        w_inter = [jnp.exp(x - mt) for x, mt in zip(log_inter, m_t)]
        w_intra = [jnp.exp(y - mt) for y, mt in zip(log_intra, m_t)]
        ws = [jnp.exp(s - mn) for s, mn in zip(log_s, m_new)]
        carry = [jnp.exp(bl + m - mn) for bl, m, mn in zip(b_last, m_prev, m_new)]
        qs = [q_ref[rs, h * dqk:(h + 1) * dqk].astype(F32) for h in heads]
        ks = [k_ref[rs, h * dqk:(h + 1) * dqk].astype(F32) * (dqk ** -0.5) for h in heads]
        qb = [q.astype(BF16) for q in qs]
        vb = [v_ref[rs, h * dv:(h + 1) * dv].astype(BF16) for h in heads]
        s = [_dot_nt(q, k.astype(BF16)) * w for q, k, w in zip(qb, ks, w_intra)]
        num = [_dot(x.astype(BF16), v) + w * _dot(q, c.astype(BF16))
               for x, v, w, q, c in zip(s, vb, w_inter, qb, c_state)]
        den = [jnp.sum(x, axis=1, keepdims=True) + w * jnp.sum(q * n, axis=1, keepdims=True)
               for x, w, q, n in zip(s, w_inter, qs, n_state)]
        kw = [k * w for k, w in zip(ks, ws)]
        upd = [_dot_tn(k.astype(BF16), v) for k, v in zip(kw, vb)]
        c_next = [cy * c + u for cy, c, u in zip(carry, c_state, upd)]
        n_next = [cy * n + jnp.sum(k, axis=0, keepdims=True) for cy, n, k in zip(carry, n_state, kw)]
        for h in heads:
            sl = slice(h * dv, (h + 1) * dv)
            hid = num[h] / jnp.maximum(jnp.abs(den[h]), jnp.exp(-m_t[h]))
            hn = hid * lax.rsqrt(jnp.mean(hid * hid, axis=-1, keepdims=True) + ML_NORM_EPS) * mh_ref[:, sl]
            o_ref[rs, sl] = (jax.nn.sigmoid(og_ref[rs, sl].astype(F32)) * hn).astype(BF16)
        return c_next, n_next, m_new

    heads = range(H)
    state = ([c_ref[h] for h in heads], [n_ref[h] for h in heads], [m_ref[h] for h in heads])
    for r0 in range(0, q_ref.shape[0], L):
        state = chunk(r0, *state)
    for h in heads:
        c_ref[h] = state[0][h]
        n_ref[h] = state[1][h]
        m_ref[h] = state[2][h]


def _mlstm_core(proj, gates, gate_b, mh_g, chunks_per_step=4):
    T = proj.shape[0]
    H = ML_HEADS
    rows = chunks_per_step * ML_CHUNK
    v_tot = mh_g.shape[0]
    dv = v_tot // H
    qk_tot = (proj.shape[1] - 2 * v_tot) // 2
    dqk = qk_tot // H
    return pl.pallas_call(
        _mlstm_kernel,
        grid=(T // rows,),
        in_specs=[
            pl.BlockSpec((rows, qk_tot), lambda c: (c, 0)),
            pl.BlockSpec((rows, qk_tot), lambda c: (c, 1)),
            pl.BlockSpec((rows, v_tot), lambda c: (c, (2 * qk_tot) // v_tot)),
            pl.BlockSpec((rows, v_tot), lambda c: (c, (2 * qk_tot) // v_tot + 1)),
            pl.BlockSpec((rows, 2 * H), lambda c: (c, 0)),
            pl.BlockSpec((1, 2 * H), lambda c: (0, 0)),
            pl.BlockSpec((1, v_tot), lambda c: (0, 0)),
        ],
        out_specs=pl.BlockSpec((rows, v_tot), lambda c: (c, 0)),
        out_shape=jax.ShapeDtypeStruct((T, v_tot), BF16),
        scratch_shapes=[pltpu.VMEM((H, dqk, dv), F32), pltpu.VMEM((H, 1, dqk), F32),
                        pltpu.VMEM((H, 1, 1), F32)],
        compiler_params=_params("arbitrary"),
        name="mlstm",
    )(proj, proj, proj, proj, gates, gate_b.reshape(1, 2 * H), mh_g.reshape(1, v_tot))


def _gate_proj_kernel(x_ref, g_ref, w_ref, o_ref):
    o_ref[...] = _dot(_rms(x_ref[...], g_ref[...]).astype(BF16), w_ref[...])


def _gate_proj(h, gain, w, tm=512):
    T, D = h.shape
    n = w.shape[1]
    return pl.pallas_call(
        _gate_proj_kernel,
        grid=(T // tm,),
        in_specs=[
            pl.BlockSpec((tm, D), lambda i: (i, 0)),
            pl.BlockSpec((1, D), lambda i: (0, 0)),
            pl.BlockSpec((D, n), lambda i: (0, 0)),
        ],
        out_specs=pl.BlockSpec((tm, n), lambda i: (i, 0)),
        out_shape=jax.ShapeDtypeStruct((T, n), F32),
        compiler_params=_params("parallel"),
        name="gate_proj",
    )(h, gain.reshape(1, D), w)


def kernel(x, p, positions, norm_g, final_g, ffn_in, ffn_out, ple_proj, ple_gate, rwkv_lerp, rwkv_w0, rwkv_w1, rwkv_w2, rwkv_a0, rwkv_a1, rwkv_a2, rwkv_g1, rwkv_g2, rwkv_kk, rwkv_ka, rwkv_rk, rwkv_w_rkv, rwkv_w_o, rwkv_lnx_g, rwkv_lnx_b, rwkv_v0, rwkv_v1, rwkv_v2, ret_w_in, ret_ln_g, ret_w_o, ml_w_in, ml_gate_b, ml_mh_g, ml_w_o):
    B, T, D = x.shape
    depth = norm_g.shape[0]
    bf = lambda w: w.astype(BF16)
    ffn_in_b, ffn_out_b = bf(ffn_in), bf(ffn_out)
    ple_proj_b, ple_gate_b = bf(ple_proj), bf(ple_gate)
    rwkv_w_rkv_b, rwkv_w_o_b = bf(rwkv_w_rkv), bf(rwkv_w_o)
    ret_w_in_b, ret_w_o_b = bf(ret_w_in), bf(ret_w_o)
    ml_w_in_b, ml_w_o_b = bf(ml_w_in), bf(ml_w_o)
    outs = []
    for b in range(B):
        h = x[b]
        v_first = None
        for i in range(depth):
            kind, j = i % 3, i // 3
            h = _ffn(h, norm_g[i, 0], ffn_in, ffn_out, (i, 0))
            if kind == 0:
                v_res = None
                if j > 0:
                    v_res = (rwkv_v0[j - 1], bf(rwkv_v1[j - 1]), bf(rwkv_v2[j - 1]))
                r, k, v, a, lw, g = _rwkv_proj(
                    h, norm_g[i, 1], rwkv_lerp[j], rwkv_w_rkv_b, (j,), bf(rwkv_w1[j]), bf(rwkv_w2[j]),
                    bf(rwkv_a1[j]), bf(rwkv_a2[j]), bf(rwkv_g1[j]), bf(rwkv_g2[j]),
                    rwkv_w0[j], rwkv_a0[j], v_res, v_first)
                if j == 0:
                    v_first = v
                mix_in = _rwkv_scan(r, k, v, a, lw, g, rwkv_kk[j], rwkv_ka[j], rwkv_rk[j].reshape(D),
                                    rwkv_lnx_g[j], rwkv_lnx_b[j])
                w_o = rwkv_w_o_b
            elif kind == 1:
                proj = _norm_matmul(h, norm_g[i, 1], ret_w_in_b, (j,), ret_w_in.shape[2])
                half = ret_w_in.shape[2] // 6 // RET_HEADS // 2
                cos, sin = _rope_table(positions[b], half)
                mix_in = _retention_core(proj, cos, sin, ret_ln_g[j])
                w_o = ret_w_o_b
            else:
                n_main = ml_w_in.shape[2] - 2 * ML_HEADS
                proj = _norm_matmul(h, norm_g[i, 1], ml_w_in_b, (j,), n_main)
                gates = _gate_proj(h, norm_g[i, 1], ml_w_in_b[j][:, n_main:])
                mix_in = _mlstm_core(proj, gates, ml_gate_b[j], ml_mh_g[j])
                w_o = ml_w_o_b
            h = _matmul_residual(h, mix_in, w_o, (j,))
            h = _ffn(h, norm_g[i, 2], ffn_in, ffn_out, (i, 1))
            h = _ple(h, norm_g[i, 3], p[i, b], ple_proj_b, ple_gate_b, (i,), final_g, i == depth - 1)
        outs.append(h)
    return outs[0][None] if B == 1 else jnp.stack(outs, axis=0)
```

```python
import functools
import math

import jax
import jax.numpy as jnp
from jax import lax
from jax.experimental import pallas as pl
from jax.experimental.pallas import tpu as pltpu

F32 = jnp.float32
BF16 = jnp.bfloat16

NORM_EPS = 1e-6
RWKV_HEAD = 64
RWKV_LNX_EPS = 64e-5
RWKV_CHUNK = 64
RET_HEADS = 8
RET_CHUNK = 128
RET_GN_EPS = 1e-6
ROPE_BASE = 10000.0
ML_HEADS = 8
ML_CHUNK = 64
ML_IGATE_CAP = 15.0
ML_NORM_EPS = 1e-6

LANES = 128
SUBLANES = 8
VMEM_LIMIT_BYTES = 60 * 1024 * 1024
HI = lax.Precision.HIGHEST


def _params(*sem):
    return pltpu.CompilerParams(dimension_semantics=sem, vmem_limit_bytes=VMEM_LIMIT_BYTES)


def _rms(x, gain):
    return x * lax.rsqrt(jnp.mean(x * x, axis=-1, keepdims=True) + NORM_EPS) * gain


def _wspec(block, lead, tail):
    return pl.BlockSpec((None,) * len(lead) + tuple(block), lambda *g: tuple(lead) + tuple(tail(*g)))


def _dot(a, b):
    return jnp.dot(a, b, preferred_element_type=F32)


def _dot_nt(a, b):
    return lax.dot_general(a, b, (((1,), (1,)), ((), ())), preferred_element_type=F32)


def _dot_tn(a, b):
    return lax.dot_general(a, b, (((0,), (0,)), ((), ())), preferred_element_type=F32)


def _ffn_kernel(x_ref, g_ref, wg_ref, wu_ref, wo_ref, o_ref, xn_ref):
    @pl.when(pl.program_id(1) == 0)
    def _():
        x = x_ref[...]
        xn_ref[...] = _rms(x, g_ref[...]).astype(BF16)
        o_ref[...] = x

    xn = xn_ref[...]
    gate = _dot(xn, wg_ref[...].astype(BF16))
    up = _dot(xn, wu_ref[...].astype(BF16))
    act = (0.5 * gate * jax.nn.sigmoid(gate) * up).astype(BF16)
    o_ref[...] += _dot(act, wo_ref[...].astype(BF16))


def _ffn(h, gain, w_in, w_out, lead, tm=1024, tf=256):
    T, D = h.shape
    F = w_out.shape[-2]
    nf = F // tf
    return pl.pallas_call(
        _ffn_kernel,
        grid=(T // tm, nf),
        in_specs=[
            pl.BlockSpec((tm, D), lambda i, f: (i, 0)),
            pl.BlockSpec((1, D), lambda i, f: (0, 0)),
            _wspec((D, tf), lead, lambda i, f: (0, f)),
            _wspec((D, tf), lead, lambda i, f: (0, nf + f)),
            _wspec((tf, D), lead, lambda i, f: (f, 0)),
        ],
        out_specs=pl.BlockSpec((tm, D), lambda i, f: (i, 0)),
        out_shape=jax.ShapeDtypeStruct((T, D), F32),
        scratch_shapes=[pltpu.VMEM((tm, D), BF16)],
        compiler_params=_params("parallel", "arbitrary"),
        name="ffn",
    )(h, gain.reshape(1, D), w_in, w_in, w_out)


def _norm_matmul_kernel(x_ref, g_ref, w_ref, o_ref, xn_ref):
    @pl.when(pl.program_id(1) == 0)
    def _():
        xn_ref[...] = _rms(x_ref[...], g_ref[...]).astype(BF16)

    o_ref[...] = _dot(xn_ref[...], w_ref[...]).astype(o_ref.dtype)


def _norm_matmul(h, gain, w, lead, N, tm=1024, tn=2048):
    T, D = h.shape
    return pl.pallas_call(
        _norm_matmul_kernel,
        grid=(T // tm, N // tn),
        in_specs=[
            pl.BlockSpec((tm, D), lambda i, j: (i, 0)),
            pl.BlockSpec((1, D), lambda i, j: (0, 0)),
            _wspec((D, tn), lead, lambda i, j: (0, j)),
        ],
        out_specs=pl.BlockSpec((tm, tn), lambda i, j: (i, j)),
        out_shape=jax.ShapeDtypeStruct((T, N), BF16),
        scratch_shapes=[pltpu.VMEM((tm, D), BF16)],
        compiler_params=_params("parallel", "arbitrary"),
        name="norm_matmul",
    )(h, gain.reshape(1, D), w)


def _matmul_residual_kernel(h_ref, a_ref, w_ref, o_ref):
    o_ref[...] = h_ref[...] + _dot(a_ref[...], w_ref[...])


def _matmul_residual(h, a, w, lead, tm=1024, tn=1024):
    T, D = h.shape
    K = a.shape[1]
    return pl.pallas_call(
        _matmul_residual_kernel,
        grid=(D // tn, T // tm),
        in_specs=[
            pl.BlockSpec((tm, tn), lambda j, i: (i, j)),
            pl.BlockSpec((tm, K), lambda j, i: (i, 0)),
            _wspec((K, tn), lead, lambda j, i: (0, j)),
        ],
        out_specs=pl.BlockSpec((tm, tn), lambda j, i: (i, j)),
        out_shape=jax.ShapeDtypeStruct((T, D), F32),
        compiler_params=_params("parallel", "parallel"),
        name="matmul_residual",
    )(h, a, w)


def _ple_kernel(final, h_ref, g_ref, p_ref, wp_ref, wg_ref, fg_ref, o_ref):
    h = h_ref[...]
    gate = jax.nn.sigmoid(_dot(_rms(h, g_ref[...]).astype(BF16), wg_ref[...]))
    out = h + _dot(p_ref[...].astype(BF16), wp_ref[...]) * gate
    if final:
        out = _rms(out, fg_ref[...])
    o_ref[...] = out


def _ple(h, gain, p, w_proj, w_gate, lead, final_gain, final, tm=512):
    T, D = h.shape
    P = p.shape[1]
    return pl.pallas_call(
        functools.partial(_ple_kernel, final),
        grid=(T // tm,),
        in_specs=[
            pl.BlockSpec((tm, D), lambda i: (i, 0)),
            pl.BlockSpec((1, D), lambda i: (0, 0)),
            pl.BlockSpec((tm, P), lambda i: (i, 0)),
            _wspec((P, D), lead, lambda i: (0, 0)),
            _wspec((D, D), lead, lambda i: (0, 0)),
            pl.BlockSpec((1, D), lambda i: (0, 0)),
        ],
        out_specs=pl.BlockSpec((tm, D), lambda i: (i, 0)),
        out_shape=jax.ShapeDtypeStruct((T, D), F32),
        compiler_params=_params("parallel"),
        name="ple",
    )(h, gain.reshape(1, D), p, w_proj, w_gate, final_gain.reshape(1, D))


def _rwkv_proj_kernel(has_vres, *refs):
    if has_vres:
        (h_ref, halo_ref, g_ref, lerp_ref, wr_ref, wk_ref, wv_ref, w1_ref, w2_ref, a1_ref, a2_ref,
         g1_ref, g2_ref, w0_ref, a0_ref, v1_ref, v2_ref, v0_ref, vf_ref,
         r_out, k_out, v_out, a_out, lw_out, g_out,
         xr_s, xk_s, xv_s, hw_s, ha_s, hg_s, hv_s) = refs
    else:
        (h_ref, halo_ref, g_ref, lerp_ref, wr_ref, wk_ref, wv_ref, w1_ref, w2_ref, a1_ref, a2_ref,
         g1_ref, g2_ref, w0_ref, a0_ref,
         r_out, k_out, v_out, a_out, lw_out, g_out,
         xr_s, xk_s, xv_s, hw_s, ha_s, hg_s) = refs
    i = pl.program_id(0)

    @pl.when(pl.program_id(1) == 0)
    def _():
        gain = g_ref[...]
        u = _rms(h_ref[...], gain)
        halo = _rms(halo_ref[...], gain)
        first = jnp.where(i > 0, halo[SUBLANES - 1:SUBLANES, :], 0.0)
        row = lax.broadcasted_iota(jnp.int32, u.shape, 0)
        u_prev = jnp.where(row == 0, first, pltpu.roll(u, 1, 0))
        xx = u_prev - u
        lerp = lerp_ref[...]
        xr_s[...] = (u + xx * lerp[0:1]).astype(BF16)
        xw = (u + xx * lerp[1:2]).astype(BF16)
        xk_s[...] = (u + xx * lerp[2:3]).astype(BF16)
        xv = (u + xx * lerp[3:4]).astype(BF16)
        xv_s[...] = xv
        xa = (u + xx * lerp[4:5]).astype(BF16)
        xg = (u + xx * lerp[5:6]).astype(BF16)
        hw_s[...] = jnp.tanh(_dot(xw, w1_ref[...])).astype(BF16)
        ha_s[...] = _dot(xa, a1_ref[...]).astype(BF16)
        hg_s[...] = jax.nn.sigmoid(_dot(xg, g1_ref[...])).astype(BF16)
        if has_vres:
            hv_s[...] = _dot(xv, v1_ref[...]).astype(BF16)

    r_out[...] = _dot(xr_s[...], wr_ref[...])
    k_out[...] = _dot(xk_s[...], wk_ref[...])
    v = _dot(xv_s[...], wv_ref[...])
    if has_vres:
        mix = jax.nn.sigmoid(v0_ref[...] + _dot(hv_s[...], v2_ref[...]))
        v = v + (vf_ref[...] - v) * mix
    v_out[...] = v
    z = w0_ref[...] + _dot(hw_s[...], w2_ref[...])
    lw_out[...] = -math.exp(-0.5) * jax.nn.sigmoid(z)
    a_out[...] = jax.nn.sigmoid(a0_ref[...] + _dot(ha_s[...], a2_ref[...]))
    g_out[...] = _dot(hg_s[...], g2_ref[...])


def _pad_cols(w, n):
    return jnp.pad(w, ((0, 0), (0, n - w.shape[1])))


def _pad_rows(w, n):
    return jnp.pad(w, ((0, n - w.shape[0]), (0, 0)))


def _rwkv_proj(h, gain, lerp, w_rkv, lead, w1, w2, a1, a2, g1, g2, w0, a0, v_res, v_first, tm=512, tn=512):
    T, D = h.shape
    has_vres = v_res is not None
    lo = LANES
    row = lambda x: x.reshape(1, D)
    full = lambda shape: pl.BlockSpec(shape, lambda i, j: (0, 0))
    col = lambda k: pl.BlockSpec((k, tn), lambda i, j: (0, j))
    tile = pl.BlockSpec((tm, tn), lambda i, j: (i, j))
    rowtile = pl.BlockSpec((1, tn), lambda i, j: (0, j))
    hb = tm // SUBLANES
    gd = g1.shape[1]
    args = [h, h, row(gain), _pad_rows(lerp, SUBLANES),
            w_rkv, w_rkv, w_rkv,
            _pad_cols(w1, lo), _pad_rows(w2, lo), _pad_cols(a1, lo), _pad_rows(a2, lo),
            g1, g2, row(w0), row(a0)]
    in_specs = [
        pl.BlockSpec((tm, D), lambda i, j: (i, 0)),
        pl.BlockSpec((SUBLANES, D), lambda i, j: (jnp.maximum(i * hb - 1, 0), 0)),
        full((1, D)), full((SUBLANES, D)),
        _wspec((D, tn), lead + (0,), lambda i, j: (0, j)),
        _wspec((D, tn), lead + (1,), lambda i, j: (0, j)),
        _wspec((D, tn), lead + (2,), lambda i, j: (0, j)),
        full((D, lo)), col(lo), full((D, lo)), col(lo),
        full((D, gd)), col(gd), rowtile, rowtile,
    ]
    scratch = [pltpu.VMEM((tm, D), BF16)] * 3 + [
        pltpu.VMEM((tm, lo), BF16), pltpu.VMEM((tm, lo), BF16), pltpu.VMEM((tm, gd), BF16)]
    if has_vres:
        v0, v1, v2 = v_res
        args += [_pad_cols(v1, lo), _pad_rows(v2, lo), row(v0), v_first]
        in_specs += [full((D, lo)), col(lo), rowtile, tile]
        scratch += [pltpu.VMEM((tm, lo), BF16)]
    return pl.pallas_call(
        functools.partial(_rwkv_proj_kernel, has_vres),
        grid=(T // tm, D // tn),
        in_specs=in_specs,
        out_specs=[tile] * 6,
        out_shape=[jax.ShapeDtypeStruct((T, D), F32)] * 6,
        scratch_shapes=scratch,
        compiler_params=_params("parallel", "arbitrary"),
        name="rwkv_proj",
    )(*args)


def _rwkv_scan_kernel(npairs, r_ref, k_ref, v_ref, a_ref, lw_ref, g_ref,
                      kk_ref, ka_ref, rk_ref, lg_ref, lb_ref, o_ref, s_ref):
    L = RWKV_CHUNK
    N = RWKV_HEAD
    W = 2 * N

    @pl.when(pl.program_id(1) == 0)
    def _():
        s_ref[...] = jnp.zeros_like(s_ref)

    t_i = lax.broadcasted_iota(jnp.int32, (L, W), 0)
    lane = lax.broadcasted_iota(jnp.int32, (L, W), 1)
    s_i = lane % N
    head0 = lane < N
    strict = s_i < t_i
    incl = s_i <= t_i
    same = [(t_i // b) == (s_i // b) for b in (8, 16, 32, L)]
    eye = jnp.where(s_i == t_i, 1.0, 0.0)
    brow = lax.broadcasted_iota(jnp.int32, (W, W), 0)
    blane = lax.broadcasted_iota(jnp.int32, (W, W), 1)
    bdiag = (brow // N) == (blane // N)

    def blk(x):
        return jnp.where(bdiag, jnp.concatenate([x, x], axis=0), 0.0)

    def segsum(x):
        s0 = jnp.sum(jnp.where(head0, x, 0.0), axis=1, keepdims=True)
        s1 = jnp.sum(jnp.where(head0, 0.0, x), axis=1, keepdims=True)
        return jnp.where(head0, s0, s1)

    def blk_b(x):
        return blk(x).astype(BF16)

    def mm(x, y):
        return _dot(x.astype(BF16), blk_b(y))

    tri_r = lax.broadcasted_iota(jnp.int32, (L, L), 0)
    tri_c = lax.broadcasted_iota(jnp.int32, (L, L), 1)
    tri = jnp.where(tri_c <= tri_r, 1.0, 0.0)
    c_all = jnp.dot(tri, lw_ref[...], precision=HI, preferred_element_type=F32)

    pairs = range(npairs)
    sls = [slice(p * W, (p + 1) * W) for p in pairs]

    def prep(sl):
        r = r_ref[:, sl]
        k = k_ref[:, sl]
        a = a_ref[:, sl]
        lw = lw_ref[:, sl]
        c = c_all[:, sl]
        kk = k * kk_ref[:, sl]
        kk = kk / jnp.maximum(jnp.sqrt(segsum(kk * kk)), 1e-12)
        kmod = k * (1.0 + (a - 1.0) * ka_ref[:, sl])
        alpha = -kk
        beta = kk * a
        c_mid = c[L // 2 - 1:L // 2, :]
        c_last = c[L - 1:L, :]
        e = c - c_mid
        ex_m = jnp.exp(-e)
        lhs = jnp.concatenate([alpha * jnp.exp(e - lw), r * jnp.exp(e)], axis=0).astype(BF16)
        rhs = jnp.concatenate([blk_b(beta * ex_m), blk_b(kmod * ex_m)], axis=0)
        to_end = jnp.exp(c_last - c)
        upd = jnp.concatenate([beta * to_end, kmod * to_end], axis=0).astype(BF16)
        return dict(lhs=lhs, rhs=rhs, upd=upd, r_abs=r * jnp.exp(c), al_abs=alpha * jnp.exp(c - lw),
                    p_last=jnp.exp(c_last), bonus=segsum(r * kmod * rk_ref[:, sl]))

    pre = [prep(sl) for sl in sls]
    amat = [_dot_nt(q["lhs"], q["rhs"]) for q in pre]
    n_ab = [jnp.where(strict, m[:L, :W], 0.0) for m in amat]
    a_rb = [jnp.where(incl, m[L:, :W], 0.0).astype(BF16) for m in amat]
    a_k = [jnp.concatenate([jnp.where(strict, m[:L, W:], 0.0), jnp.where(incl, m[L:, W:], 0.0)],
                           axis=0).astype(BF16) for m in amat]
    av = [_dot(x, blk_b(v_ref[:, sl])) for x, sl in zip(a_k, sls)]
    akv = [x[:L] for x in av]

    nd = [jnp.where(same[0], n, 0.0) for n in n_ab]
    tinv = [eye + n for n in nd]
    pw = [mm(n, n) for n in nd]
    both = [mm(jnp.concatenate([t, q], axis=0), q) for t, q in zip(tinv, pw)]
    tinv = [t + b[:L] for t, b in zip(tinv, both)]
    tinv = [t + mm(t, b[L:]) for t, b in zip(tinv, both)]
    for lvl in range(1, len(same)):
        off_mask = same[lvl] & jnp.logical_not(same[lvl - 1])
        inner = [mm(jnp.where(off_mask, n, 0.0), t) for n, t in zip(n_ab, tinv)]
        tinv = [t + mm(t, x) for t, x in zip(tinv, inner)]

    ta = [_dot(t.astype(BF16), jnp.concatenate([blk_b(q["al_abs"]), blk_b(x)], axis=1))
          for t, q, x in zip(tinv, pre, akv)]
    ra = [jnp.concatenate([q["r_abs"], t[:, :W]], axis=0).astype(BF16) for q, t in zip(pre, ta)]

    states = [s_ref[p] for p in pairs]
    su = [_dot(x, s.astype(BF16)) for x, s in zip(ra, states)]
    us = [x[L:] + t[:, W:] for x, t in zip(su, ta)]
    ys = [x[:L] + _dot(a, blk_b(u)) + y[L:] for x, a, u, y in zip(su, a_rb, us, av)]
    uv = [jnp.concatenate([u, v_ref[:, sl]], axis=0).astype(BF16) for u, sl in zip(us, sls)]
    z = [_dot_tn(q["upd"], x) for x, q in zip(uv, pre)]
    c_end = jnp.concatenate([c_all[L - 1:L, sl] for sl in sls]
                            + [jnp.zeros((W - npairs, W), F32)], axis=0).T
    for p in pairs:
        s_ref[p] = states[p] * jnp.exp(c_end[:, p:p + 1]) + jnp.where(bdiag, z[p], 0.0)

    for p in pairs:
        sl = sls[p]
        y = ys[p]
        mu = segsum(y) * (1.0 / N)
        yc = y - mu
        var = segsum(yc * yc) * (1.0 / N)
        out = yc * lax.rsqrt(var + RWKV_LNX_EPS) * lg_ref[:, sl] + lb_ref[:, sl]
        out = out + pre[p]["bonus"] * v_ref[:, sl]
        o_ref[:, sl] = (out * g_ref[:, sl]).astype(BF16)


def _rwkv_scan(r, k, v, a, lw, g, k_k, k_a, r_k, lnx_g, lnx_b, npairs=16):
    T, D = r.shape
    L = RWKV_CHUNK
    wl = npairs * 2 * RWKV_HEAD
    tile = pl.BlockSpec((L, wl), lambda hg, c: (c, hg))
    prow = pl.BlockSpec((1, wl), lambda hg, c: (0, hg))
    row = lambda x: x.reshape(1, D)
    return pl.pallas_call(
        functools.partial(_rwkv_scan_kernel, npairs),
        grid=(D // wl, T // L),
        in_specs=[tile] * 6 + [prow] * 5,
        out_specs=tile,
        out_shape=jax.ShapeDtypeStruct((T, D), BF16),
        scratch_shapes=[pltpu.VMEM((npairs, 2 * RWKV_HEAD, 2 * RWKV_HEAD), F32)],
        compiler_params=_params("parallel", "arbitrary"),
        name="rwkv_scan",
    )(r, k, v, a, lw, g, row(k_k), row(k_a), row(r_k), row(lnx_g), row(lnx_b))


def _rope_table_kernel(pos_ref, cos_ref, sin_ref):
    half = cos_ref.shape[1]
    idx = lax.broadcasted_iota(jnp.int32, (1, half), 1).astype(F32)
    freqs = jnp.exp(idx * (-jnp.log(ROPE_BASE) / half))
    ang = pos_ref[...].astype(F32) * freqs
    cos_ref[...] = jnp.cos(ang)
    sin_ref[...] = jnp.sin(ang)


def _rope_table(positions, half, tm=1024):
    T = positions.shape[0]
    return pl.pallas_call(
        _rope_table_kernel,
        grid=(T // tm,),
        in_specs=[pl.BlockSpec((tm, 1), lambda i: (i, 0))],
        out_specs=[pl.BlockSpec((tm, half), lambda i: (i, 0))] * 2,
        out_shape=[jax.ShapeDtypeStruct((T, half), F32)] * 2,
        compiler_params=_params("parallel"),
        name="rope_table",
    )(positions.reshape(T, 1))


def _retention_kernel(q_ref, k_ref, v_ref, g_ref, cos_ref, sin_ref, lg_ref, ln_ref, o_ref, st_ref):
    L = RET_CHUNK
    H = RET_HEADS
    dk = q_ref.shape[1] // H
    dv = v_ref.shape[1] // H
    half = dk // 2

    @pl.when(pl.program_id(0) == 0)
    def _():
        st_ref[...] = jnp.zeros_like(st_ref)

    cos = cos_ref[...]
    sin = sin_ref[...]

    def rope(ref, h):
        x1 = ref[:, h * dk:h * dk + half].astype(F32)
        x2 = ref[:, h * dk + half:(h + 1) * dk].astype(F32)
        return jnp.concatenate([x1 * cos - x2 * sin, x1 * sin + x2 * cos], axis=1)

    row = lax.broadcasted_iota(jnp.int32, (L, L), 0)
    col = lax.broadcasted_iota(jnp.int32, (L, L), 1)
    rel = (row - col).astype(F32)
    tcol = lax.broadcasted_iota(jnp.int32, (L, 1), 0).astype(F32)

    heads = range(H)
    lgs = [lg_ref[h] for h in heads]
    intra = [jnp.where(rel >= 0, jnp.exp(jnp.maximum(rel, 0.0) * lg), 0.0) for lg in lgs]
    lg1 = [lg[:, 0:1] for lg in lgs]
    qb = [rope(q_ref, h).astype(BF16) for h in heads]
    ks = [rope(k_ref, h) * (dk ** -0.5) for h in heads]
    vb = [v_ref[:, h * dv:(h + 1) * dv].astype(BF16) for h in heads]
    s = [(_dot_nt(q, k.astype(BF16)) * m).astype(BF16) for q, k, m in zip(qb, ks, intra)]
    states = [st_ref[h] for h in heads]
    o = [_dot(x, v) + _dot(q, st.astype(BF16)) * jnp.exp((tcol + 1.0) * lg)
         for x, v, q, st, lg in zip(s, vb, qb, states, lg1)]
    kz = [(k * jnp.exp((L - 1.0 - tcol) * lg)).astype(BF16) for k, lg in zip(ks, lg1)]
    upd = [_dot_tn(k, v) for k, v in zip(kz, vb)]
    for h in heads:
        st_ref[h] = jnp.exp(L * lg1[h]) * states[h] + upd[h]

    for h in heads:
        sl = slice(h * dv, (h + 1) * dv)
        mu = jnp.mean(o[h], axis=-1, keepdims=True)
        oc = o[h] - mu
        on = oc * lax.rsqrt(jnp.mean(oc * oc, axis=-1, keepdims=True) + RET_GN_EPS) * ln_ref[:, sl]
        g = g_ref[:, sl].astype(F32)
        o_ref[:, sl] = (g * jax.nn.sigmoid(g) * on).astype(BF16)


def _retention_core(proj, cos, sin, ln_g):
    T = proj.shape[0]
    H = RET_HEADS
    L = RET_CHUNK
    dk = cos.shape[1] * 2
    v_tot = ln_g.shape[0]
    dv = v_tot // H
    qk_tot = H * dk
    hs = jnp.arange(H, dtype=F32)
    log_gamma = jnp.broadcast_to(jnp.log(1.0 - 2.0 ** (-5.0 - hs))[:, None, None], (H, 1, L))
    return pl.pallas_call(
        _retention_kernel,
        grid=(T // L,),
        in_specs=[
            pl.BlockSpec((L, qk_tot), lambda c: (c, 0)),
            pl.BlockSpec((L, qk_tot), lambda c: (c, 1)),
            pl.BlockSpec((L, v_tot), lambda c: (c, (2 * qk_tot) // v_tot)),
            pl.BlockSpec((L, v_tot), lambda c: (c, (2 * qk_tot) // v_tot + 1)),
            pl.BlockSpec((L, dk // 2), lambda c: (c, 0)),
            pl.BlockSpec((L, dk // 2), lambda c: (c, 0)),
            pl.BlockSpec((H, 1, L), lambda c: (0, 0, 0)),
            pl.BlockSpec((1, v_tot), lambda c: (0, 0)),
        ],
        out_specs=pl.BlockSpec((L, v_tot), lambda c: (c, 0)),
        out_shape=jax.ShapeDtypeStruct((T, v_tot), BF16),
        scratch_shapes=[pltpu.VMEM((H, dk, dv), F32)],
        compiler_params=_params("arbitrary"),
        name="retention",
    )(proj, proj, proj, proj, cos, sin, log_gamma, ln_g.reshape(1, v_tot))


def _mlstm_kernel(q_ref, k_ref, v_ref, og_ref, gates_ref, gb_ref, mh_ref, o_ref, c_ref, n_ref, m_ref):
    L = ML_CHUNK
    H = ML_HEADS
    dqk = q_ref.shape[1] // H
    dv = v_ref.shape[1] // H

    @pl.when(pl.program_id(0) == 0)
    def _():
        c_ref[...] = jnp.zeros_like(c_ref)
        n_ref[...] = jnp.zeros_like(n_ref)
        m_ref[...] = jnp.zeros_like(m_ref)

    row = lax.broadcasted_iota(jnp.int32, (L, L), 0)
    col = lax.broadcasted_iota(jnp.int32, (L, L), 1)
    causal = col <= row

    def chunk(r0, c_state, n_state, m_prev):
        rs = slice(r0, r0 + L)
        gates = gates_ref[rs, :] + gb_ref[...]
        glane = lax.broadcasted_iota(jnp.int32, gates.shape, 1)
        log_i = ML_IGATE_CAP * jnp.tanh(gates / ML_IGATE_CAP)
        log_f = jnp.minimum(gates, 0.0) - jnp.log1p(jnp.exp(-jnp.abs(gates)))
        act = jnp.where(glane < H, log_i, log_f)
        cum_f = jnp.dot(jnp.where(causal, 1.0, 0.0), act[:, H:], precision=HI, preferred_element_type=F32)
        cols = jnp.concatenate([act, cum_f, jnp.zeros((L, LANES - 3 * H), F32)], axis=1)
        rows = cols.T

        ic = [cols[:, h:h + 1] for h in heads]
        b_col = [cols[:, 2 * H + h:2 * H + h + 1] for h in heads]
        i_row = [rows[h:h + 1, :] for h in heads]
        b_row = [rows[2 * H + h:2 * H + h + 1, :] for h in heads]
        b_last = [b[L - 1:L, :] for b in b_col]
        log_inter = [b + m for b, m in zip(b_col, m_prev)]
        log_intra = [jnp.where(causal, bc - br + ir, -jnp.inf) for bc, br, ir in zip(b_col, b_row, i_row)]
        m_t = [jnp.maximum(x, jnp.max(y, axis=1, keepdims=True)) for x, y in zip(log_inter, log_intra)]
        log_s = [bl - bc + i for bl, bc, i in zip(b_last, b_col, ic)]
        m_new = [jnp.maximum(bl + m, jnp.max(s, axis=0, keepdims=True)) for bl, m, s in zip(b_last, m_prev, log_s)]
        w_inter = [jnp.exp(x - mt) for x, mt in zip(log_inter, m_t)]
        w_intra = [jnp.exp(y - mt) for y, mt in zip(log_intra, m_t)]
        ws = [jnp.exp(s - mn) for s, mn in zip(log_s, m_new)]
        carry = [jnp.exp(bl + m - mn) for bl, m, mn in zip(b_last, m_prev, m_new)]
        qs = [q_ref[rs, h * dqk:(h + 1) * dqk].astype(F32) for h in heads]
        ks = [k_ref[rs, h * dqk:(h + 1) * dqk].astype(F32) * (dqk ** -0.5) for h in heads]
        qb = [q.astype(BF16) for q in qs]
        vb = [v_ref[rs, h * dv:(h + 1) * dv].astype(BF16) for h in heads]
        s = [_dot_nt(q, k.astype(BF16)) * w for q, k, w in zip(qb, ks, w_intra)]
        num = [_dot(x.astype(BF16), v) + w * _dot(q, c.astype(BF16))
               for x, v, w, q, c in zip(s, vb, w_inter, qb, c_state)]
        den = [jnp.sum(x, axis=1, keepdims=True) + w * jnp.sum(q * n, axis=1, keepdims=True)
               for x, w, q, n in zip(s, w_inter, qs, n_state)]
        kw = [k * w for k, w in zip(ks, ws)]
        upd = [_dot_tn(k.astype(BF16), v) for k, v in zip(kw, vb)]
        c_next = [cy * c + u for cy, c, u in zip(carry, c_state, upd)]
        n_next = [cy * n + jnp.sum(k, axis=0, keepdims=True) for cy, n, k in zip(carry, n_state, kw)]
        for h in heads:
            sl = slice(h * dv, (h + 1) * dv)
            hid = num[h] / jnp.maximum(jnp.abs(den[h]), jnp.exp(-m_t[h]))
            hn = hid * lax.rsqrt(jnp.mean(hid * hid, axis=-1, keepdims=True) + ML_NORM_EPS) * mh_ref[:, sl]
            o_ref[rs, sl] = (jax.nn.sigmoid(og_ref[rs, sl].astype(F32)) * hn).astype(BF16)
        return c_next, n_next, m_new

    heads = range(H)
    state = ([c_ref[h] for h in heads], [n_ref[h] for h in heads], [m_ref[h] for h in heads])
    for r0 in range(0, q_ref.shape[0], L):
        state = chunk(r0, *state)
    for h in heads:
        c_ref[h] = state[0][h]
        n_ref[h] = state[1][h]
        m_ref[h] = state[2][h]


def _mlstm_core(proj, gates, gate_b, mh_g, chunks_per_step=4):
    T = proj.shape[0]
    H = ML_HEADS
    rows = chunks_per_step * ML_CHUNK
    v_tot = mh_g.shape[0]
    dv = v_tot // H
    qk_tot = (proj.shape[1] - 2 * v_tot) // 2
    dqk = qk_tot // H
    return pl.pallas_call(
        _mlstm_kernel,
        grid=(T // rows,),
        in_specs=[
            pl.BlockSpec((rows, qk_tot), lambda c: (c, 0)),
            pl.BlockSpec((rows, qk_tot), lambda c: (c, 1)),
            pl.BlockSpec((rows, v_tot), lambda c: (c, (2 * qk_tot) // v_tot)),
            pl.BlockSpec((rows, v_tot), lambda c: (c, (2 * qk_tot) // v_tot + 1)),
            pl.BlockSpec((rows, 2 * H), lambda c: (c, 0)),
            pl.BlockSpec((1, 2 * H), lambda c: (0, 0)),
            pl.BlockSpec((1, v_tot), lambda c: (0, 0)),
        ],
        out_specs=pl.BlockSpec((rows, v_tot), lambda c: (c, 0)),
        out_shape=jax.ShapeDtypeStruct((T, v_tot), BF16),
        scratch_shapes=[pltpu.VMEM((H, dqk, dv), F32), pltpu.VMEM((H, 1, dqk), F32),
                        pltpu.VMEM((H, 1, 1), F32)],
        compiler_params=_params("arbitrary"),
        name="mlstm",
    )(proj, proj, proj, proj, gates, gate_b.reshape(1, 2 * H), mh_g.reshape(1, v_tot))


def _gate_proj_kernel(x_ref, g_ref, w_ref, o_ref):
    o_ref[...] = _dot(_rms(x_ref[...], g_ref[...]).astype(BF16), w_ref[...])


def _gate_proj(h, gain, w, tm=512):
    T, D = h.shape
    n = w.shape[1]
    return pl.pallas_call(
        _gate_proj_kernel,
        grid=(T // tm,),
        in_specs=[
            pl.BlockSpec((tm, D), lambda i: (i, 0)),
            pl.BlockSpec((1, D), lambda i: (0, 0)),
            pl.BlockSpec((D, n), lambda i: (0, 0)),
        ],
        out_specs=pl.BlockSpec((tm, n), lambda i: (i, 0)),
        out_shape=jax.ShapeDtypeStruct((T, n), F32),
        compiler_params=_params("parallel"),
        name="gate_proj",
    )(h, gain.reshape(1, D), w)


def kernel(x, p, positions, norm_g, final_g, ffn_in, ffn_out, ple_proj, ple_gate, rwkv_lerp, rwkv_w0, rwkv_w1, rwkv_w2, rwkv_a0, rwkv_a1, rwkv_a2, rwkv_g1, rwkv_g2, rwkv_kk, rwkv_ka, rwkv_rk, rwkv_w_rkv, rwkv_w_o, rwkv_lnx_g, rwkv_lnx_b, rwkv_v0, rwkv_v1, rwkv_v2, ret_w_in, ret_ln_g, ret_w_o, ml_w_in, ml_gate_b, ml_mh_g, ml_w_o):
    B, T, D = x.shape
    depth = norm_g.shape[0]
    bf = lambda w: w.astype(BF16)
    ffn_in_b, ffn_out_b = bf(ffn_in), bf(ffn_out)
    ple_proj_b, ple_gate_b = bf(ple_proj), bf(ple_gate)
    rwkv_w_rkv_b, rwkv_w_o_b = bf(rwkv_w_rkv), bf(rwkv_w_o)
    ret_w_in_b, ret_w_o_b = bf(ret_w_in), bf(ret_w_o)
    ml_w_in_b, ml_w_o_b = bf(ml_w_in), bf(ml_w_o)
    outs = []
    for b in range(B):
        h = x[b]
        v_first = None
        for i in range(depth):
            kind, j = i % 3, i // 3
            h = _ffn(h, norm_g[i, 0], ffn_in, ffn_out, (i, 0))
            if kind == 0:
                v_res = None
                if j > 0:
                    v_res = (rwkv_v0[j - 1], bf(rwkv_v1[j - 1]), bf(rwkv_v2[j - 1]))
                r, k, v, a, lw, g = _rwkv_proj(
                    h, norm_g[i, 1], rwkv_lerp[j], rwkv_w_rkv_b, (j,), bf(rwkv_w1[j]), bf(rwkv_w2[j]),
                    bf(rwkv_a1[j]), bf(rwkv_a2[j]), bf(rwkv_g1[j]), bf(rwkv_g2[j]),
                    rwkv_w0[j], rwkv_a0[j], v_res, v_first)
                if j == 0:
                    v_first = v
                mix_in = _rwkv_scan(r, k, v, a, lw, g, rwkv_kk[j], rwkv_ka[j], rwkv_rk[j].reshape(D),
                                    rwkv_lnx_g[j], rwkv_lnx_b[j])
                w_o = rwkv_w_o_b
            elif kind == 1:
                proj = _norm_matmul(h, norm_g[i, 1], ret_w_in_b, (j,), ret_w_in.shape[2])
                half = ret_w_in.shape[2] // 6 // RET_HEADS // 2
                cos, sin = _rope_table(positions[b], half)
                mix_in = _retention_core(proj, cos, sin, ret_ln_g[j])
                w_o = ret_w_o_b
            else:
                n_main = ml_w_in.shape[2] - 2 * ML_HEADS
                proj = _norm_matmul(h, norm_g[i, 1], ml_w_in_b, (j,), n_main)
                gates = _gate_proj(h, norm_g[i, 1], ml_w_in_b[j][:, n_main:])
                mix_in = _mlstm_core(proj, gates, ml_gate_b[j], ml_mh_g[j])
                w_o = ml_w_o_b
            h = _matmul_residual(h, mix_in, w_o, (j,))
            h = _ffn(h, norm_g[i, 2], ffn_in, ffn_out, (i, 1))
            h = _ple(h, norm_g[i, 3], p[i, b], ple_proj_b, ple_gate_b, (i,), final_g, i == depth - 1)
        outs.append(h)
    return outs[0][None] if B == 1 else jnp.stack(outs, axis=0)
```

```python
import functools
import math

import jax
import jax.numpy as jnp
from jax import lax
from jax.experimental import pallas as pl
from jax.experimental.pallas import tpu as pltpu

F32 = jnp.float32
BF16 = jnp.bfloat16

NORM_EPS = 1e-6
RWKV_HEAD = 64
RWKV_LNX_EPS = 64e-5
RWKV_CHUNK = 64
RET_HEADS = 8
RET_CHUNK = 128
RET_GN_EPS = 1e-6
ROPE_BASE = 10000.0
ML_HEADS = 8
ML_CHUNK = 64
ML_IGATE_CAP = 15.0
ML_NORM_EPS = 1e-6

LANES = 128
SUBLANES = 8
VMEM_LIMIT_BYTES = 60 * 1024 * 1024
HI = lax.Precision.HIGHEST


def _params(*sem):
    return pltpu.CompilerParams(dimension_semantics=sem, vmem_limit_bytes=VMEM_LIMIT_BYTES)


def _rms(x, gain):
    return x * lax.rsqrt(jnp.mean(x * x, axis=-1, keepdims=True) + NORM_EPS) * gain


def _wspec(block, lead, tail):
    return pl.BlockSpec((None,) * len(lead) + tuple(block), lambda *g: tuple(lead) + tuple(tail(*g)))


def _dot(a, b):
    return jnp.dot(a, b, preferred_element_type=F32)


def _dot_nt(a, b):
    return lax.dot_general(a, b, (((1,), (1,)), ((), ())), preferred_element_type=F32)


def _dot_tn(a, b):
    return lax.dot_general(a, b, (((0,), (0,)), ((), ())), preferred_element_type=F32)


def _ffn_kernel(x_ref, g_ref, wg_ref, wu_ref, wo_ref, o_ref, xn_ref):
    @pl.when(pl.program_id(1) == 0)
    def _():
        x = x_ref[...]
        xn_ref[...] = _rms(x, g_ref[...]).astype(BF16)
        o_ref[...] = x

    xn = xn_ref[...]
    gate = _dot(xn, wg_ref[...].astype(BF16))
    up = _dot(xn, wu_ref[...].astype(BF16))
    act = (0.5 * gate * jax.nn.sigmoid(gate) * up).astype(BF16)
    o_ref[...] += _dot(act, wo_ref[...].astype(BF16))


def _ffn(h, gain, w_in, w_out, lead, tm=1024, tf=256):
    T, D = h.shape
    F = w_out.shape[-2]
    nf = F // tf
    return pl.pallas_call(
        _ffn_kernel,
        grid=(T // tm, nf),
        in_specs=[
            pl.BlockSpec((tm, D), lambda i, f: (i, 0)),
            pl.BlockSpec((1, D), lambda i, f: (0, 0)),
            _wspec((D, tf), lead, lambda i, f: (0, f)),
            _wspec((D, tf), lead, lambda i, f: (0, nf + f)),
            _wspec((tf, D), lead, lambda i, f: (f, 0)),
        ],
        out_specs=pl.BlockSpec((tm, D), lambda i, f: (i, 0)),
        out_shape=jax.ShapeDtypeStruct((T, D), F32),
        scratch_shapes=[pltpu.VMEM((tm, D), BF16)],
        compiler_params=_params("parallel", "arbitrary"),
        name="ffn",
    )(h, gain.reshape(1, D), w_in, w_in, w_out)


def _norm_matmul_kernel(x_ref, g_ref, w_ref, o_ref, xn_ref):
    @pl.when(pl.program_id(1) == 0)
    def _():
        xn_ref[...] = _rms(x_ref[...], g_ref[...]).astype(BF16)

    o_ref[...] = _dot(xn_ref[...], w_ref[...]).astype(o_ref.dtype)


def _norm_matmul(h, gain, w, lead, N, tm=1024, tn=2048):
    T, D = h.shape
    return pl.pallas_call(
        _norm_matmul_kernel,
        grid=(T // tm, N // tn),
        in_specs=[
            pl.BlockSpec((tm, D), lambda i, j: (i, 0)),
            pl.BlockSpec((1, D), lambda i, j: (0, 0)),
            _wspec((D, tn), lead, lambda i, j: (0, j)),
        ],
        out_specs=pl.BlockSpec((tm, tn), lambda i, j: (i, j)),
        out_shape=jax.ShapeDtypeStruct((T, N), BF16),
        scratch_shapes=[pltpu.VMEM((tm, D), BF16)],
        compiler_params=_params("parallel", "arbitrary"),
        name="norm_matmul",
    )(h, gain.reshape(1, D), w)


def _matmul_residual_kernel(h_ref, a_ref, w_ref, o_ref):
    o_ref[...] = h_ref[...] + _dot(a_ref[...], w_ref[...])


def _matmul_residual(h, a, w, lead, tm=1024, tn=1024):
    T, D = h.shape
    K = a.shape[1]
    return pl.pallas_call(
        _matmul_residual_kernel,
        grid=(D // tn, T // tm),
        in_specs=[
            pl.BlockSpec((tm, tn), lambda j, i: (i, j)),
            pl.BlockSpec((tm, K), lambda j, i: (i, 0)),
            _wspec((K, tn), lead, lambda j, i: (0, j)),
        ],
        out_specs=pl.BlockSpec((tm, tn), lambda j, i: (i, j)),
        out_shape=jax.ShapeDtypeStruct((T, D), F32),
        compiler_params=_params("parallel", "parallel"),
        name="matmul_residual",
    )(h, a, w)


def _ple_kernel(final, h_ref, g_ref, p_ref, wp_ref, wg_ref, fg_ref, o_ref):
    h = h_ref[...]
    gate = jax.nn.sigmoid(_dot(_rms(h, g_ref[...]).astype(BF16), wg_ref[...]))
    out = h + _dot(p_ref[...].astype(BF16), wp_ref[...]) * gate
    if final:
        out = _rms(out, fg_ref[...])
    o_ref[...] = out


def _ple(h, gain, p, w_proj, w_gate, lead, final_gain, final, tm=512):
    T, D = h.shape
    P = p.shape[1]
    return pl.pallas_call(
        functools.partial(_ple_kernel, final),
        grid=(T // tm,),
        in_specs=[
            pl.BlockSpec((tm, D), lambda i: (i, 0)),
            pl.BlockSpec((1, D), lambda i: (0, 0)),
            pl.BlockSpec((tm, P), lambda i: (i, 0)),
            _wspec((P, D), lead, lambda i: (0, 0)),
            _wspec((D, D), lead, lambda i: (0, 0)),
            pl.BlockSpec((1, D), lambda i: (0, 0)),
        ],
        out_specs=pl.BlockSpec((tm, D), lambda i: (i, 0)),
        out_shape=jax.ShapeDtypeStruct((T, D), F32),
        compiler_params=_params("parallel"),
        name="ple",
    )(h, gain.reshape(1, D), p, w_proj, w_gate, final_gain.reshape(1, D))


def _rwkv_proj_kernel(has_vres, *refs):
    if has_vres:
        (h_ref, halo_ref, g_ref, lerp_ref, wr_ref, wk_ref, wv_ref, w1_ref, w2_ref, a1_ref, a2_ref,
         g1_ref, g2_ref, w0_ref, a0_ref, v1_ref, v2_ref, v0_ref, vf_ref,
         r_out, k_out, v_out, a_out, lw_out, g_out,
         xr_s, xk_s, xv_s, hw_s, ha_s, hg_s, hv_s) = refs
    else:
        (h_ref, halo_ref, g_ref, lerp_ref, wr_ref, wk_ref, wv_ref, w1_ref, w2_ref, a1_ref, a2_ref,
         g1_ref, g2_ref, w0_ref, a0_ref,
         r_out, k_out, v_out, a_out, lw_out, g_out,
         xr_s, xk_s, xv_s, hw_s, ha_s, hg_s) = refs
    i = pl.program_id(0)

    @pl.when(pl.program_id(1) == 0)
    def _():
        gain = g_ref[...]
        u = _rms(h_ref[...], gain)
        halo = _rms(halo_ref[...], gain)
        first = jnp.where(i > 0, halo[SUBLANES - 1:SUBLANES, :], 0.0)
        row = lax.broadcasted_iota(jnp.int32, u.shape, 0)
        u_prev = jnp.where(row == 0, first, pltpu.roll(u, 1, 0))
        xx = u_prev - u
        lerp = lerp_ref[...]
        xr_s[...] = (u + xx * lerp[0:1]).astype(BF16)
        xw = (u + xx * lerp[1:2]).astype(BF16)
        xk_s[...] = (u + xx * lerp[2:3]).astype(BF16)
        xv = (u + xx * lerp[3:4]).astype(BF16)
        xv_s[...] = xv
        xa = (u + xx * lerp[4:5]).astype(BF16)
        xg = (u + xx * lerp[5:6]).astype(BF16)
        hw_s[...] = jnp.tanh(_dot(xw, w1_ref[...])).astype(BF16)
        ha_s[...] = _dot(xa, a1_ref[...]).astype(BF16)
        hg_s[...] = jax.nn.sigmoid(_dot(xg, g1_ref[...])).astype(BF16)
        if has_vres:
            hv_s[...] = _dot(xv, v1_ref[...]).astype(BF16)

    r_out[...] = _dot(xr_s[...], wr_ref[...])
    k_out[...] = _dot(xk_s[...], wk_ref[...])
    v = _dot(xv_s[...], wv_ref[...])
    if has_vres:
        mix = jax.nn.sigmoid(v0_ref[...] + _dot(hv_s[...], v2_ref[...]))
        v = v + (vf_ref[...] - v) * mix
    v_out[...] = v
    z = w0_ref[...] + _dot(hw_s[...], w2_ref[...])
    lw_out[...] = -math.exp(-0.5) * jax.nn.sigmoid(z)
    a_out[...] = jax.nn.sigmoid(a0_ref[...] + _dot(ha_s[...], a2_ref[...]))
    g_out[...] = _dot(hg_s[...], g2_ref[...])


def _pad_cols(w, n):
    return jnp.pad(w, ((0, 0), (0, n - w.shape[1])))


def _pad_rows(w, n):
    return jnp.pad(w, ((0, n - w.shape[0]), (0, 0)))


def _rwkv_proj(h, gain, lerp, w_rkv, lead, w1, w2, a1, a2, g1, g2, w0, a0, v_res, v_first, tm=512, tn=512):
    T, D = h.shape
    has_vres = v_res is not None
    lo = LANES
    row = lambda x: x.reshape(1, D)
    full = lambda shape: pl.BlockSpec(shape, lambda i, j: (0, 0))
    col = lambda k: pl.BlockSpec((k, tn), lambda i, j: (0, j))
    tile = pl.BlockSpec((tm, tn), lambda i, j: (i, j))
    rowtile = pl.BlockSpec((1, tn), lambda i, j: (0, j))
    hb = tm // SUBLANES
    gd = g1.shape[1]
    args = [h, h, row(gain), _pad_rows(lerp, SUBLANES),
            w_rkv, w_rkv, w_rkv,
            _pad_cols(w1, lo), _pad_rows(w2, lo), _pad_cols(a1, lo), _pad_rows(a2, lo),
            g1, g2, row(w0), row(a0)]
    in_specs = [
        pl.BlockSpec((tm, D), lambda i, j: (i, 0)),
        pl.BlockSpec((SUBLANES, D), lambda i, j: (jnp.maximum(i * hb - 1, 0), 0)),
        full((1, D)), full((SUBLANES, D)),
        _wspec((D, tn), lead + (0,), lambda i, j: (0, j)),
        _wspec((D, tn), lead + (1,), lambda i, j: (0, j)),
        _wspec((D, tn), lead + (2,), lambda i, j: (0, j)),
        full((D, lo)), col(lo), full((D, lo)), col(lo),
        full((D, gd)), col(gd), rowtile, rowtile,
    ]
    scratch = [pltpu.VMEM((tm, D), BF16)] * 3 + [
        pltpu.VMEM((tm, lo), BF16), pltpu.VMEM((tm, lo), BF16), pltpu.VMEM((tm, gd), BF16)]
    if has_vres:
        v0, v1, v2 = v_res
        args += [_pad_cols(v1, lo), _pad_rows(v2, lo), row(v0), v_first]
        in_specs += [full((D, lo)), col(lo), rowtile, tile]
        scratch += [pltpu.VMEM((tm, lo), BF16)]
    return pl.pallas_call(
        functools.partial(_rwkv_proj_kernel, has_vres),
        grid=(T // tm, D // tn),
        in_specs=in_specs,
        out_specs=[tile] * 6,
        out_shape=[jax.ShapeDtypeStruct((T, D), F32)] * 6,
        scratch_shapes=scratch,
        compiler_params=_params("parallel", "arbitrary"),
        name="rwkv_proj",
    )(*args)


def _rwkv_scan_kernel(npairs, r_ref, k_ref, v_ref, a_ref, lw_ref, g_ref,
                      kk_ref, ka_ref, rk_ref, lg_ref, lb_ref, o_ref, s_ref):
    L = RWKV_CHUNK
    N = RWKV_HEAD
    W = 2 * N

    @pl.when(pl.program_id(1) == 0)
    def _():
        s_ref[...] = jnp.zeros_like(s_ref)

    t_i = lax.broadcasted_iota(jnp.int32, (L, W), 0)
    lane = lax.broadcasted_iota(jnp.int32, (L, W), 1)
    s_i = lane % N
    head0 = lane < N
    strict = s_i < t_i
    incl = s_i <= t_i
    same = [(t_i // b) == (s_i // b) for b in (8, 16, 32, L)]
    eye = jnp.where(s_i == t_i, 1.0, 0.0)
    brow = lax.broadcasted_iota(jnp.int32, (W, W), 0)
    blane = lax.broadcasted_iota(jnp.int32, (W, W), 1)
    bdiag = (brow // N) == (blane // N)

    def blk(x):
        return jnp.where(bdiag, jnp.concatenate([x, x], axis=0), 0.0)

    def segsum(x):
        s0 = jnp.sum(jnp.where(head0, x, 0.0), axis=1, keepdims=True)
        s1 = jnp.sum(jnp.where(head0, 0.0, x), axis=1, keepdims=True)
        return jnp.where(head0, s0, s1)

    def blk_b(x):
        return blk(x).astype(BF16)

    def mm(x, y):
        return _dot(x.astype(BF16), blk_b(y))

    tri_r = lax.broadcasted_iota(jnp.int32, (L, L), 0)
    tri_c = lax.broadcasted_iota(jnp.int32, (L, L), 1)
    tri = jnp.where(tri_c <= tri_r, 1.0, 0.0)
    c_all = jnp.dot(tri, lw_ref[...], precision=HI, preferred_element_type=F32)

    pairs = range(npairs)
    sls = [slice(p * W, (p + 1) * W) for p in pairs]

    def prep(sl):
        r = r_ref[:, sl]
        k = k_ref[:, sl]
        a = a_ref[:, sl]
        lw = lw_ref[:, sl]
        c = c_all[:, sl]
        kk = k * kk_ref[:, sl]
        kk = kk / jnp.maximum(jnp.sqrt(segsum(kk * kk)), 1e-12)
        kmod = k * (1.0 + (a - 1.0) * ka_ref[:, sl])
        alpha = -kk
        beta = kk * a
        c_mid = c[L // 2 - 1:L // 2, :]
        c_last = c[L - 1:L, :]
        e = c - c_mid
        ex_m = jnp.exp(-e)
        lhs = jnp.concatenate([alpha * jnp.exp(e - lw), r * jnp.exp(e)], axis=0).astype(BF16)
        rhs = jnp.concatenate([blk_b(beta * ex_m), blk_b(kmod * ex_m)], axis=0)
        to_end = jnp.exp(c_last - c)
        upd = jnp.concatenate([beta * to_end, kmod * to_end], axis=0).astype(BF16)
        return dict(lhs=lhs, rhs=rhs, upd=upd, r_abs=r * jnp.exp(c), al_abs=alpha * jnp.exp(c - lw),
                    bonus=segsum(r * kmod * rk_ref[:, sl]))

    pre = [prep(sl) for sl in sls]
    amat = [_dot_nt(q["lhs"], q["rhs"]) for q in pre]
    n_ab = [jnp.where(strict, m[:L, :W], 0.0) for m in amat]
    a_rb = [jnp.where(incl, m[L:, :W], 0.0).astype(BF16) for m in amat]
    a_k = [jnp.concatenate([jnp.where(strict, m[:L, W:], 0.0), jnp.where(incl, m[L:, W:], 0.0)],
                           axis=0).astype(BF16) for m in amat]
    av = [_dot(x, blk_b(v_ref[:, sl])) for x, sl in zip(a_k, sls)]
    akv = [x[:L] for x in av]

    nd = [jnp.where(same[0], n, 0.0) for n in n_ab]
    tinv = [eye + n for n in nd]
    pw = [mm(n, n) for n in nd]
    both = [mm(jnp.concatenate([t, q], axis=0), q) for t, q in zip(tinv, pw)]
    tinv = [t + b[:L] for t, b in zip(tinv, both)]
    tinv = [t + mm(t, b[L:]) for t, b in zip(tinv, both)]
    for lvl in range(1, len(same)):
        off_mask = same[lvl] & jnp.logical_not(same[lvl - 1])
        inner = [mm(jnp.where(off_mask, n, 0.0), t) for n, t in zip(n_ab, tinv)]
        tinv = [t + mm(t, x) for t, x in zip(tinv, inner)]

    ta = [_dot(t.astype(BF16), jnp.concatenate([blk_b(q["al_abs"]), blk_b(x)], axis=1))
          for t, q, x in zip(tinv, pre, akv)]
    ra = [jnp.concatenate([q["r_abs"], t[:, :W]], axis=0).astype(BF16) for q, t in zip(pre, ta)]

    states = [s_ref[p] for p in pairs]
    su = [_dot(x, s.astype(BF16)) for x, s in zip(ra, states)]
    us = [x[L:] + t[:, W:] for x, t in zip(su, ta)]
    ys = [x[:L] + _dot(a, blk_b(u)) + y[L:] for x, a, u, y in zip(su, a_rb, us, av)]
    uv = [jnp.concatenate([u, v_ref[:, sl]], axis=0).astype(BF16) for u, sl in zip(us, sls)]
    z = [_dot_tn(q["upd"], x) for x, q in zip(uv, pre)]
    c_end = jnp.concatenate([c_all[L - 1:L, sl] for sl in sls]
                            + [jnp.zeros((W - npairs, W), F32)], axis=0).T
    for p in pairs:
        s_ref[p] = states[p] * jnp.exp(c_end[:, p:p + 1]) + jnp.where(bdiag, z[p], 0.0)

    for p in pairs:
        sl = sls[p]
        y = ys[p]
        mu = segsum(y) * (1.0 / N)
        yc = y - mu
        var = segsum(yc * yc) * (1.0 / N)
        out = yc * lax.rsqrt(var + RWKV_LNX_EPS) * lg_ref[:, sl] + lb_ref[:, sl]
        out = out + pre[p]["bonus"] * v_ref[:, sl]
        o_ref[:, sl] = (out * g_ref[:, sl]).astype(BF16)


def _rwkv_scan(r, k, v, a, lw, g, k_k, k_a, r_k, lnx_g, lnx_b, npairs=16):
    T, D = r.shape
    L = RWKV_CHUNK
    wl = npairs * 2 * RWKV_HEAD
    tile = pl.BlockSpec((L, wl), lambda hg, c: (c, hg))
    prow = pl.BlockSpec((1, wl), lambda hg, c: (0, hg))
    row = lambda x: x.reshape(1, D)
    return pl.pallas_call(
        functools.partial(_rwkv_scan_kernel, npairs),
        grid=(D // wl, T // L),
        in_specs=[tile] * 6 + [prow] * 5,
        out_specs=tile,
        out_shape=jax.ShapeDtypeStruct((T, D), BF16),
        scratch_shapes=[pltpu.VMEM((npairs, 2 * RWKV_HEAD, 2 * RWKV_HEAD), F32)],
        compiler_params=_params("parallel", "arbitrary"),
        name="rwkv_scan",
    )(r, k, v, a, lw, g, row(k_k), row(k_a), row(r_k), row(lnx_g), row(lnx_b))


def _rope_table_kernel(pos_ref, cos_ref, sin_ref):
    half = cos_ref.shape[1]
    idx = lax.broadcasted_iota(jnp.int32, (1, half), 1).astype(F32)
    freqs = jnp.exp(idx * (-jnp.log(ROPE_BASE) / half))
    ang = pos_ref[...].astype(F32) * freqs
    cos_ref[...] = jnp.cos(ang)
    sin_ref[...] = jnp.sin(ang)


def _rope_table(positions, half, tm=1024):
    T = positions.shape[0]
    return pl.pallas_call(
        _rope_table_kernel,
        grid=(T // tm,),
        in_specs=[pl.BlockSpec((tm, 1), lambda i: (i, 0))],
        out_specs=[pl.BlockSpec((tm, half), lambda i: (i, 0))] * 2,
        out_shape=[jax.ShapeDtypeStruct((T, half), F32)] * 2,
        compiler_params=_params("parallel"),
        name="rope_table",
    )(positions.reshape(T, 1))


def _retention_kernel(q_ref, k_ref, v_ref, g_ref, cos_ref, sin_ref, lg_ref, ln_ref, o_ref, st_ref):
    L = RET_CHUNK
    H = RET_HEADS
    dk = q_ref.shape[1] // H
    dv = v_ref.shape[1] // H
    half = dk // 2

    @pl.when(pl.program_id(0) == 0)
    def _():
        st_ref[...] = jnp.zeros_like(st_ref)

    cos = cos_ref[...]
    sin = sin_ref[...]

    def rope(ref, h):
        x1 = ref[:, h * dk:h * dk + half].astype(F32)
        x2 = ref[:, h * dk + half:(h + 1) * dk].astype(F32)
        return jnp.concatenate([x1 * cos - x2 * sin, x1 * sin + x2 * cos], axis=1)

    row = lax.broadcasted_iota(jnp.int32, (L, L), 0)
    col = lax.broadcasted_iota(jnp.int32, (L, L), 1)
    rel = (row - col).astype(F32)
    tcol = lax.broadcasted_iota(jnp.int32, (L, 1), 0).astype(F32)

    heads = range(H)
    lgs = [lg_ref[h] for h in heads]
    intra = [jnp.where(rel >= 0, jnp.exp(jnp.maximum(rel, 0.0) * lg), 0.0) for lg in lgs]
    lg1 = [lg[:, 0:1] for lg in lgs]
    qb = [rope(q_ref, h).astype(BF16) for h in heads]
    ks = [rope(k_ref, h) * (dk ** -0.5) for h in heads]
    vb = [v_ref[:, h * dv:(h + 1) * dv].astype(BF16) for h in heads]
    s = [(_dot_nt(q, k.astype(BF16)) * m).astype(BF16) for q, k, m in zip(qb, ks, intra)]
    states = [st_ref[h] for h in heads]
    o = [_dot(x, v) + _dot(q, st.astype(BF16)) * jnp.exp((tcol + 1.0) * lg)
         for x, v, q, st, lg in zip(s, vb, qb, states, lg1)]
    kz = [(k * jnp.exp((L - 1.0 - tcol) * lg)).astype(BF16) for k, lg in zip(ks, lg1)]
    upd = [_dot_tn(k, v) for k, v in zip(kz, vb)]
    for h in heads:
        st_ref[h] = jnp.exp(L * lg1[h]) * states[h] + upd[h]

    for h in heads:
        sl = slice(h * dv, (h + 1) * dv)
        mu = jnp.mean(o[h], axis=-1, keepdims=True)
        oc = o[h] - mu
        on = oc * lax.rsqrt(jnp.mean(oc * oc, axis=-1, keepdims=True) + RET_GN_EPS) * ln_ref[:, sl]
        g = g_ref[:, sl].astype(F32)
        o_ref[:, sl] = (g * jax.nn.sigmoid(g) * on).astype(BF16)


def _retention_core(proj, cos, sin, ln_g):
    T = proj.shape[0]
    H = RET_HEADS
    L = RET_CHUNK
    dk = cos.shape[1] * 2
    v_tot = ln_g.shape[0]
    dv = v_tot // H
    qk_tot = H * dk
    hs = jnp.arange(H, dtype=F32)
    log_gamma = jnp.broadcast_to(jnp.log(1.0 - 2.0 ** (-5.0 - hs))[:, None, None], (H, 1, L))
    return pl.pallas_call(
        _retention_kernel,
        grid=(T // L,),
        in_specs=[
            pl.BlockSpec((L, qk_tot), lambda c: (c, 0)),
            pl.BlockSpec((L, qk_tot), lambda c: (c, 1)),
            pl.BlockSpec((L, v_tot), lambda c: (c, (2 * qk_tot) // v_tot)),
            pl.BlockSpec((L, v_tot), lambda c: (c, (2 * qk_tot) // v_tot + 1)),
            pl.BlockSpec((L, dk // 2), lambda c: (c, 0)),
            pl.BlockSpec((L, dk // 2), lambda c: (c, 0)),
            pl.BlockSpec((H, 1, L), lambda c: (0, 0, 0)),
            pl.BlockSpec((1, v_tot), lambda c: (0, 0)),
        ],
        out_specs=pl.BlockSpec((L, v_tot), lambda c: (c, 0)),
        out_shape=jax.ShapeDtypeStruct((T, v_tot), BF16),
        scratch_shapes=[pltpu.VMEM((H, dk, dv), F32)],
        compiler_params=_params("arbitrary"),
        name="retention",
    )(proj, proj, proj, proj, cos, sin, log_gamma, ln_g.reshape(1, v_tot))


def _mlstm_kernel(q_ref, k_ref, v_ref, og_ref, gates_ref, gb_ref, mh_ref, o_ref, c_ref, n_ref, m_ref):
    L = ML_CHUNK
    H = ML_HEADS
    dqk = q_ref.shape[1] // H
    dv = v_ref.shape[1] // H

    @pl.when(pl.program_id(0) == 0)
    def _():
        c_ref[...] = jnp.zeros_like(c_ref)
        n_ref[...] = jnp.zeros_like(n_ref)
        m_ref[...] = jnp.zeros_like(m_ref)

    row = lax.broadcasted_iota(jnp.int32, (L, L), 0)
    col = lax.broadcasted_iota(jnp.int32, (L, L), 1)
    causal = col <= row

    def chunk(r0, c_state, n_state, m_prev):
        rs = slice(r0, r0 + L)
        gates = gates_ref[rs, :] + gb_ref[...]
        glane = lax.broadcasted_iota(jnp.int32, gates.shape, 1)
        log_i = ML_IGATE_CAP * jnp.tanh(gates / ML_IGATE_CAP)
        log_f = jnp.minimum(gates, 0.0) - jnp.log1p(jnp.exp(-jnp.abs(gates)))
        act = jnp.where(glane < H, log_i, log_f)
        cum_f = jnp.dot(jnp.where(causal, 1.0, 0.0), act[:, H:], precision=HI, preferred_element_type=F32)
        cols = jnp.concatenate([act, cum_f, jnp.zeros((L, LANES - 3 * H), F32)], axis=1)
        rows = cols.T

        ic = [cols[:, h:h + 1] for h in heads]
        b_col = [cols[:, 2 * H + h:2 * H + h + 1] for h in heads]
        i_row = [rows[h:h + 1, :] for h in heads]
        b_row = [rows[2 * H + h:2 * H + h + 1, :] for h in heads]
        b_last = [b[L - 1:L, :] for b in b_col]
        log_inter = [b + m for b, m in zip(b_col, m_prev)]
        log_intra = [jnp.where(causal, bc - br + ir, -jnp.inf) for bc, br, ir in zip(b_col, b_row, i_row)]
        m_t = [jnp.maximum(x, jnp.max(y, axis=1, keepdims=True)) for x, y in zip(log_inter, log_intra)]
        log_s = [bl - bc + i for bl, bc, i in zip(b_last, b_col, ic)]
        m_new = [jnp.maximum(bl + m, jnp.max(s, axis=0, keepdims=True)) for bl, m, s in zip(b_last, m_prev, log_s)]
        w_inter = [jnp.exp(x - mt) for x, mt in zip(log_inter, m_t)]
        w_intra = [jnp.exp(y - mt) for y, mt in zip(log_intra, m_t)]
        ws = [jnp.exp(s - mn) for s, mn in zip(log_s, m_new)]
        carry = [jnp.exp(bl + m - mn) for bl, m, mn in zip(b_last, m_prev, m_new)]
        qs = [q_ref[rs, h * dqk:(h + 1) * dqk].astype(F32) for h in heads]
        ks = [k_ref[rs, h * dqk:(h + 1) * dqk].astype(F32) * (dqk ** -0.5) for h in heads]
        qb = [q.astype(BF16) for q in qs]
        vb = [v_ref[rs, h * dv:(h + 1) * dv].astype(BF16) for h in heads]
        s = [_dot_nt(q, k.astype(BF16)) * w for q, k, w in zip(qb, ks, w_intra)]
        num = [_dot(x.astype(BF16), v) + w * _dot(q, c.astype(BF16))
               for x, v, w, q, c in zip(s, vb, w_inter, qb, c_state)]
        den = [jnp.sum(x, axis=1, keepdims=True) + w * jnp.sum(q * n, axis=1, keepdims=True)
               for x, w, q, n in zip(s, w_inter, qs, n_state)]
        kw = [k * w for k, w in zip(ks, ws)]
        upd = [_dot_tn(k.astype(BF16), v) for k, v in zip(kw, vb)]
        c_next = [cy * c + u for cy, c, u in zip(carry, c_state, upd)]
        n_next = [cy * n + jnp.sum(k, axis=0, keepdims=True) for cy, n, k in zip(carry, n_state, kw)]
        for h in heads:
            sl = slice(h * dv, (h + 1) * dv)
            hid = num[h] / jnp.maximum(jnp.abs(den[h]), jnp.exp(-m_t[h]))
            hn = hid * lax.rsqrt(jnp.mean(hid * hid, axis=-1, keepdims=True) + ML_NORM_EPS) * mh_ref[:, sl]
            o_ref[rs, sl] = (jax.nn.sigmoid(og_ref[rs, sl].astype(F32)) * hn).astype(BF16)
        return c_next, n_next, m_new

    heads = range(H)
    state = ([c_ref[h] for h in heads], [n_ref[h] for h in heads], [m_ref[h] for h in heads])
    for r0 in range(0, q_ref.shape[0], L):
        state = chunk(r0, *state)
    for h in heads:
        c_ref[h] = state[0][h]
        n_ref[h] = state[1][h]
        m_ref[h] = state[2][h]


def _mlstm_core(proj, gates, gate_b, mh_g, chunks_per_step=4):
    T = proj.shape[0]
    H = ML_HEADS
    rows = chunks_per_step * ML_CHUNK
    v_tot = mh_g.shape[0]
    dv = v_tot // H
    qk_tot = (proj.shape[1] - 2 * v_tot) // 2
    dqk = qk_tot // H
    return pl.pallas_call(
        _mlstm_kernel,
        grid=(T // rows,),
        in_specs=[
            pl.BlockSpec((rows, qk_tot), lambda c: (c, 0)),
            pl.BlockSpec((rows, qk_tot), lambda c: (c, 1)),
            pl.BlockSpec((rows, v_tot), lambda c: (c, (2 * qk_tot) // v_tot)),
            pl.BlockSpec((rows, v_tot), lambda c: (c, (2 * qk_tot) // v_tot + 1)),
            pl.BlockSpec((rows, 2 * H), lambda c: (c, 0)),
            pl.BlockSpec((1, 2 * H), lambda c: (0, 0)),
            pl.BlockSpec((1, v_tot), lambda c: (0, 0)),
        ],
        out_specs=pl.BlockSpec((rows, v_tot), lambda c: (c, 0)),
        out_shape=jax.ShapeDtypeStruct((T, v_tot), BF16),
        scratch_shapes=[pltpu.VMEM((H, dqk, dv), F32), pltpu.VMEM((H, 1, dqk), F32),
                        pltpu.VMEM((H, 1, 1), F32)],
        compiler_params=_params("arbitrary"),
        name="mlstm",
    )(proj, proj, proj, proj, gates, gate_b.reshape(1, 2 * H), mh_g.reshape(1, v_tot))


def _gate_proj_kernel(x_ref, g_ref, w_ref, o_ref):
    o_ref[...] = _dot(_rms(x_ref[...], g_ref[...]).astype(BF16), w_ref[...])


def _gate_proj(h, gain, w, tm=512):
    T, D = h.shape
    n = w.shape[1]
    return pl.pallas_call(
        _gate_proj_kernel,
        grid=(T // tm,),
        in_specs=[
            pl.BlockSpec((tm, D), lambda i: (i, 0)),
            pl.BlockSpec((1, D), lambda i: (0, 0)),
            pl.BlockSpec((D, n), lambda i: (0, 0)),
        ],
        out_specs=pl.BlockSpec((tm, n), lambda i: (i, 0)),
        out_shape=jax.ShapeDtypeStruct((T, n), F32),
        compiler_params=_params("parallel"),
        name="gate_proj",
    )(h, gain.reshape(1, D), w)


def kernel(x, p, positions, norm_g, final_g, ffn_in, ffn_out, ple_proj, ple_gate, rwkv_lerp, rwkv_w0, rwkv_w1, rwkv_w2, rwkv_a0, rwkv_a1, rwkv_a2, rwkv_g1, rwkv_g2, rwkv_kk, rwkv_ka, rwkv_rk, rwkv_w_rkv, rwkv_w_o, rwkv_lnx_g, rwkv_lnx_b, rwkv_v0, rwkv_v1, rwkv_v2, ret_w_in, ret_ln_g, ret_w_o, ml_w_in, ml_gate_b, ml_mh_g, ml_w_o):
    B, T, D = x.shape
    depth = norm_g.shape[0]
    bf = lambda w: w.astype(BF16)
    ffn_in_b, ffn_out_b = bf(ffn_in), bf(ffn_out)
    ple_proj_b, ple_gate_b = bf(ple_proj), bf(ple_gate)
    rwkv_w_rkv_b, rwkv_w_o_b = bf(rwkv_w_rkv), bf(rwkv_w_o)
    ret_w_in_b, ret_w_o_b = bf(ret_w_in), bf(ret_w_o)
    ml_w_in_b, ml_w_o_b = bf(ml_w_in), bf(ml_w_o)
    outs = []
    for b in range(B):
        h = x[b]
        v_first = None
        for i in range(depth):
            kind, j = i % 3, i // 3
            h = _ffn(h, norm_g[i, 0], ffn_in, ffn_out, (i, 0))
            if kind == 0:
                v_res = None
                if j > 0:
                    v_res = (rwkv_v0[j - 1], bf(rwkv_v1[j - 1]), bf(rwkv_v2[j - 1]))
                r, k, v, a, lw, g = _rwkv_proj(
                    h, norm_g[i, 1], rwkv_lerp[j], rwkv_w_rkv_b, (j,), bf(rwkv_w1[j]), bf(rwkv_w2[j]),
                    bf(rwkv_a1[j]), bf(rwkv_a2[j]), bf(rwkv_g1[j]), bf(rwkv_g2[j]),
                    rwkv_w0[j], rwkv_a0[j], v_res, v_first)
                if j == 0:
                    v_first = v
                mix_in = _rwkv_scan(r, k, v, a, lw, g, rwkv_kk[j], rwkv_ka[j], rwkv_rk[j].reshape(D),
                                    rwkv_lnx_g[j], rwkv_lnx_b[j])
                w_o = rwkv_w_o_b
            elif kind == 1:
                proj = _norm_matmul(h, norm_g[i, 1], ret_w_in_b, (j,), ret_w_in.shape[2])
                half = ret_w_in.shape[2] // 6 // RET_HEADS // 2
                cos, sin = _rope_table(positions[b], half)
                mix_in = _retention_core(proj, cos, sin, ret_ln_g[j])
                w_o = ret_w_o_b
            else:
                n_main = ml_w_in.shape[2] - 2 * ML_HEADS
                proj = _norm_matmul(h, norm_g[i, 1], ml_w_in_b, (j,), n_main)
                gates = _gate_proj(h, norm_g[i, 1], ml_w_in_b[j][:, n_main:])
                mix_in = _mlstm_core(proj, gates, ml_gate_b[j], ml_mh_g[j])
                w_o = ml_w_o_b
            h = _matmul_residual(h, mix_in, w_o, (j,))
            h = _ffn(h, norm_g[i, 2], ffn_in, ffn_out, (i, 1))
            h = _ple(h, norm_g[i, 3], p[i, b], ple_proj_b, ple_gate_b, (i,), final_g, i == depth - 1)
        outs.append(h)
    return outs[0][None] if B == 1 else jnp.stack(outs, axis=0)
```

```python
import functools
import math

import jax
import jax.numpy as jnp
from jax import lax
from jax.experimental import pallas as pl
from jax.experimental.pallas import tpu as pltpu

F32 = jnp.float32
BF16 = jnp.bfloat16

NORM_EPS = 1e-6
RWKV_HEAD = 64
RWKV_LNX_EPS = 64e-5
RWKV_CHUNK = 64
RET_HEADS = 8
RET_CHUNK = 128
RET_GN_EPS = 1e-6
ROPE_BASE = 10000.0
ML_HEADS = 8
ML_CHUNK = 64
ML_IGATE_CAP = 15.0
ML_NORM_EPS = 1e-6

LANES = 128
SUBLANES = 8
VMEM_LIMIT_BYTES = 60 * 1024 * 1024
HI = lax.Precision.HIGHEST


def _params(*sem):
    return pltpu.CompilerParams(dimension_semantics=sem, vmem_limit_bytes=VMEM_LIMIT_BYTES)


def _rms(x, gain):
    return x * lax.rsqrt(jnp.mean(x * x, axis=-1, keepdims=True) + NORM_EPS) * gain


def _wspec(block, lead, tail):
    return pl.BlockSpec((None,) * len(lead) + tuple(block), lambda *g: tuple(lead) + tuple(tail(*g)))


def _dot(a, b):
    return jnp.dot(a, b, preferred_element_type=F32)


def _dot_nt(a, b):
    return lax.dot_general(a, b, (((1,), (1,)), ((), ())), preferred_element_type=F32)


def _dot_tn(a, b):
    return lax.dot_general(a, b, (((0,), (0,)), ((), ())), preferred_element_type=F32)


FFN_WEIGHT_SLOTS = 3


def _ffn_kernel(lead, nf, nsteps, x_ref, g_ref, win_ref, wout_ref, o_ref, xn_ref,
                wg_buf, wu_buf, wo_buf, sems):
    tf = wg_buf.shape[2]
    l0, l1 = lead
    ahead = FFN_WEIGHT_SLOTS - 1
    s = pl.program_id(0) * nf + pl.program_id(1)

    def copies(step):
        step = jnp.asarray(step, jnp.int32)
        f = step % nf
        slot = step % FFN_WEIGHT_SLOTS
        col_g = pl.multiple_of(f * tf, tf)
        col_u = pl.multiple_of((nf + f) * tf, tf)
        return (
            pltpu.make_async_copy(win_ref.at[l0, l1, :, pl.ds(col_g, tf)], wg_buf.at[slot], sems.at[0, slot]),
            pltpu.make_async_copy(win_ref.at[l0, l1, :, pl.ds(col_u, tf)], wu_buf.at[slot], sems.at[1, slot]),
            pltpu.make_async_copy(wout_ref.at[l0, l1, pl.ds(col_g, tf), :], wo_buf.at[slot], sems.at[2, slot]),
        )

    @pl.when(s == 0)
    def _():
        for first in range(ahead):
            for c in copies(first):
                c.start()

    @pl.when(s + ahead < nsteps)
    def _():
        for c in copies(s + ahead):
            c.start()

    @pl.when(pl.program_id(1) == 0)
    def _():
        x = x_ref[...]
        xn_ref[...] = _rms(x, g_ref[...]).astype(BF16)
        o_ref[...] = x

    for c in copies(s):
        c.wait()
    slot = s % FFN_WEIGHT_SLOTS
    xn = xn_ref[...]
    gate = _dot(xn, wg_buf[slot].astype(BF16))
    up = _dot(xn, wu_buf[slot].astype(BF16))
    act = (0.5 * gate * jax.nn.sigmoid(gate) * up).astype(BF16)
    o_ref[...] += _dot(act, wo_buf[slot].astype(BF16))


def _ffn(h, gain, w_in, w_out, lead, tm=1024, tf=256):
    T, D = h.shape
    F = w_out.shape[-2]
    nf = F // tf
    nsteps = (T // tm) * nf
    assert nsteps >= FFN_WEIGHT_SLOTS
    slots = FFN_WEIGHT_SLOTS
    return pl.pallas_call(
        functools.partial(_ffn_kernel, tuple(lead), nf, nsteps),
        grid=(T // tm, nf),
        in_specs=[
            pl.BlockSpec((tm, D), lambda i, f: (i, 0)),
            pl.BlockSpec((1, D), lambda i, f: (0, 0)),
            pl.BlockSpec(memory_space=pl.ANY),
            pl.BlockSpec(memory_space=pl.ANY),
        ],
        out_specs=pl.BlockSpec((tm, D), lambda i, f: (i, 0)),
        out_shape=jax.ShapeDtypeStruct((T, D), F32),
        scratch_shapes=[pltpu.VMEM((tm, D), BF16),
                        pltpu.VMEM((slots, D, tf), w_in.dtype), pltpu.VMEM((slots, D, tf), w_in.dtype),
                        pltpu.VMEM((slots, tf, D), w_out.dtype), pltpu.SemaphoreType.DMA((3, slots))],
        compiler_params=_params("arbitrary", "arbitrary"),
        name="ffn",
    )(h, gain.reshape(1, D), w_in, w_out)


def _norm_matmul_kernel(x_ref, g_ref, w_ref, o_ref, xn_ref):
    @pl.when(pl.program_id(1) == 0)
    def _():
        xn_ref[...] = _rms(x_ref[...], g_ref[...]).astype(BF16)

    o_ref[...] = _dot(xn_ref[...], w_ref[...]).astype(o_ref.dtype)


def _norm_matmul(h, gain, w, lead, N, tm=1024, tn=2048):
    T, D = h.shape
    return pl.pallas_call(
        _norm_matmul_kernel,
        grid=(T // tm, N // tn),
        in_specs=[
            pl.BlockSpec((tm, D), lambda i, j: (i, 0)),
            pl.BlockSpec((1, D), lambda i, j: (0, 0)),
            _wspec((D, tn), lead, lambda i, j: (0, j)),
        ],
        out_specs=pl.BlockSpec((tm, tn), lambda i, j: (i, j)),
        out_shape=jax.ShapeDtypeStruct((T, N), BF16),
        scratch_shapes=[pltpu.VMEM((tm, D), BF16)],
        compiler_params=_params("parallel", "arbitrary"),
        name="norm_matmul",
    )(h, gain.reshape(1, D), w)


def _matmul_residual_kernel(h_ref, a_ref, w_ref, o_ref):
    o_ref[...] = h_ref[...] + _dot(a_ref[...], w_ref[...])


def _matmul_residual(h, a, w, lead, tm=1024, tn=1024):
    T, D = h.shape
    K = a.shape[1]
    return pl.pallas_call(
        _matmul_residual_kernel,
        grid=(D // tn, T // tm),
        in_specs=[
            pl.BlockSpec((tm, tn), lambda j, i: (i, j)),
            pl.BlockSpec((tm, K), lambda j, i: (i, 0)),
            _wspec((K, tn), lead, lambda j, i: (0, j)),
        ],
        out_specs=pl.BlockSpec((tm, tn), lambda j, i: (i, j)),
        out_shape=jax.ShapeDtypeStruct((T, D), F32),
        compiler_params=_params("parallel", "parallel"),
        name="matmul_residual",
    )(h, a, w)


def _ple_kernel(final, h_ref, g_ref, p_ref, wp_ref, wg_ref, fg_ref, o_ref):
    h = h_ref[...]
    gate = jax.nn.sigmoid(_dot(_rms(h, g_ref[...]).astype(BF16), wg_ref[...]))
    out = h + _dot(p_ref[...].astype(BF16), wp_ref[...]) * gate
    if final:
        out = _rms(out, fg_ref[...])
    o_ref[...] = out


def _ple(h, gain, p, w_proj, w_gate, lead, final_gain, final, tm=512):
    T, D = h.shape
    P = p.shape[1]
    return pl.pallas_call(
        functools.partial(_ple_kernel, final),
        grid=(T // tm,),
        in_specs=[
            pl.BlockSpec((tm, D), lambda i: (i, 0)),
            pl.BlockSpec((1, D), lambda i: (0, 0)),
            pl.BlockSpec((tm, P), lambda i: (i, 0)),
            _wspec((P, D), lead, lambda i: (0, 0)),
            _wspec((D, D), lead, lambda i: (0, 0)),
            pl.BlockSpec((1, D), lambda i: (0, 0)),
        ],
        out_specs=pl.BlockSpec((tm, D), lambda i: (i, 0)),
        out_shape=jax.ShapeDtypeStruct((T, D), F32),
        compiler_params=_params("parallel"),
        name="ple",
    )(h, gain.reshape(1, D), p, w_proj, w_gate, final_gain.reshape(1, D))


def _rwkv_proj_kernel(has_vres, *refs):
    if has_vres:
        (h_ref, halo_ref, g_ref, lerp_ref, wr_ref, wk_ref, wv_ref, w1_ref, w2_ref, a1_ref, a2_ref,
         g1_ref, g2_ref, w0_ref, a0_ref, v1_ref, v2_ref, v0_ref, vf_ref,
         r_out, k_out, v_out, a_out, lw_out, g_out,
         xr_s, xk_s, xv_s, hw_s, ha_s, hg_s, hv_s) = refs
    else:
        (h_ref, halo_ref, g_ref, lerp_ref, wr_ref, wk_ref, wv_ref, w1_ref, w2_ref, a1_ref, a2_ref,
         g1_ref, g2_ref, w0_ref, a0_ref,
         r_out, k_out, v_out, a_out, lw_out, g_out,
         xr_s, xk_s, xv_s, hw_s, ha_s, hg_s) = refs
    i = pl.program_id(0)

    @pl.when(pl.program_id(1) == 0)
    def _():
        gain = g_ref[...]
        u = _rms(h_ref[...], gain)
        halo = _rms(halo_ref[...], gain)
        first = jnp.where(i > 0, halo[SUBLANES - 1:SUBLANES, :], 0.0)
        row = lax.broadcasted_iota(jnp.int32, u.shape, 0)
        u_prev = jnp.where(row == 0, first, pltpu.roll(u, 1, 0))
        xx = u_prev - u
        lerp = lerp_ref[...]
        xr_s[...] = (u + xx * lerp[0:1]).astype(BF16)
        xw = (u + xx * lerp[1:2]).astype(BF16)
        xk_s[...] = (u + xx * lerp[2:3]).astype(BF16)
        xv = (u + xx * lerp[3:4]).astype(BF16)
        xv_s[...] = xv
        xa = (u + xx * lerp[4:5]).astype(BF16)
        xg = (u + xx * lerp[5:6]).astype(BF16)
        hw_s[...] = jnp.tanh(_dot(xw, w1_ref[...])).astype(BF16)
        ha_s[...] = _dot(xa, a1_ref[...]).astype(BF16)
        hg_s[...] = jax.nn.sigmoid(_dot(xg, g1_ref[...])).astype(BF16)
        if has_vres:
            hv_s[...] = _dot(xv, v1_ref[...]).astype(BF16)

    r_out[...] = _dot(xr_s[...], wr_ref[...])
    k_out[...] = _dot(xk_s[...], wk_ref[...])
    v = _dot(xv_s[...], wv_ref[...])
    if has_vres:
        mix = jax.nn.sigmoid(v0_ref[...] + _dot(hv_s[...], v2_ref[...]))
        v = v + (vf_ref[...] - v) * mix
    v_out[...] = v
    z = w0_ref[...] + _dot(hw_s[...], w2_ref[...])
    lw_out[...] = -math.exp(-0.5) * jax.nn.sigmoid(z)
    a_out[...] = jax.nn.sigmoid(a0_ref[...] + _dot(ha_s[...], a2_ref[...]))
    g_out[...] = _dot(hg_s[...], g2_ref[...])


def _pad_cols(w, n):
    return jnp.pad(w, ((0, 0), (0, n - w.shape[1])))


def _pad_rows(w, n):
    return jnp.pad(w, ((0, n - w.shape[0]), (0, 0)))


def _rwkv_proj(h, gain, lerp, w_rkv, lead, w1, w2, a1, a2, g1, g2, w0, a0, v_res, v_first, tm=512, tn=512):
    T, D = h.shape
    has_vres = v_res is not None
    lo = LANES
    row = lambda x: x.reshape(1, D)
    full = lambda shape: pl.BlockSpec(shape, lambda i, j: (0, 0))
    col = lambda k: pl.BlockSpec((k, tn), lambda i, j: (0, j))
    tile = pl.BlockSpec((tm, tn), lambda i, j: (i, j))
    rowtile = pl.BlockSpec((1, tn), lambda i, j: (0, j))
    hb = tm // SUBLANES
    gd = g1.shape[1]
    args = [h, h, row(gain), _pad_rows(lerp, SUBLANES),
            w_rkv, w_rkv, w_rkv,
            _pad_cols(w1, lo), _pad_rows(w2, lo), _pad_cols(a1, lo), _pad_rows(a2, lo),
            g1, g2, row(w0), row(a0)]
    in_specs = [
        pl.BlockSpec((tm, D), lambda i, j: (i, 0)),
        pl.BlockSpec((SUBLANES, D), lambda i, j: (jnp.maximum(i * hb - 1, 0), 0)),
        full((1, D)), full((SUBLANES, D)),
        _wspec((D, tn), lead + (0,), lambda i, j: (0, j)),
        _wspec((D, tn), lead + (1,), lambda i, j: (0, j)),
        _wspec((D, tn), lead + (2,), lambda i, j: (0, j)),
        full((D, lo)), col(lo), full((D, lo)), col(lo),
        full((D, gd)), col(gd), rowtile, rowtile,
    ]
    scratch = [pltpu.VMEM((tm, D), BF16)] * 3 + [
        pltpu.VMEM((tm, lo), BF16), pltpu.VMEM((tm, lo), BF16), pltpu.VMEM((tm, gd), BF16)]
    if has_vres:
        v0, v1, v2 = v_res
        args += [_pad_cols(v1, lo), _pad_rows(v2, lo), row(v0), v_first]
        in_specs += [full((D, lo)), col(lo), rowtile, tile]
        scratch += [pltpu.VMEM((tm, lo), BF16)]
    return pl.pallas_call(
        functools.partial(_rwkv_proj_kernel, has_vres),
        grid=(T // tm, D // tn),
        in_specs=in_specs,
        out_specs=[tile] * 6,
        out_shape=[jax.ShapeDtypeStruct((T, D), F32)] * 6,
        scratch_shapes=scratch,
        compiler_params=_params("parallel", "arbitrary"),
        name="rwkv_proj",
    )(*args)


def _rwkv_scan_kernel(npairs, r_ref, k_ref, v_ref, a_ref, lw_ref, g_ref,
                      kk_ref, ka_ref, rk_ref, lg_ref, lb_ref, o_ref, s_ref):
    L = RWKV_CHUNK
    N = RWKV_HEAD
    W = 2 * N

    @pl.when(pl.program_id(1) == 0)
    def _():
        s_ref[...] = jnp.zeros_like(s_ref)

    t_i = lax.broadcasted_iota(jnp.int32, (L, W), 0)
    lane = lax.broadcasted_iota(jnp.int32, (L, W), 1)
    s_i = lane % N
    head0 = lane < N
    strict = s_i < t_i
    incl = s_i <= t_i
    same = [(t_i // b) == (s_i // b) for b in (8, 16, 32, L)]
    eye = jnp.where(s_i == t_i, 1.0, 0.0)
    brow = lax.broadcasted_iota(jnp.int32, (W, W), 0)
    blane = lax.broadcasted_iota(jnp.int32, (W, W), 1)
    bdiag = (brow // N) == (blane // N)

    def blk(x):
        return jnp.where(bdiag, jnp.concatenate([x, x], axis=0), 0.0)

    def segsum(x):
        s0 = jnp.sum(jnp.where(head0, x, 0.0), axis=1, keepdims=True)
        s1 = jnp.sum(jnp.where(head0, 0.0, x), axis=1, keepdims=True)
        return jnp.where(head0, s0, s1)

    def blk_b(x):
        return blk(x).astype(BF16)

    def mm(x, y):
        return _dot(x.astype(BF16), blk_b(y))

    tri_r = lax.broadcasted_iota(jnp.int32, (L, L), 0)
    tri_c = lax.broadcasted_iota(jnp.int32, (L, L), 1)
    tri = jnp.where(tri_c <= tri_r, 1.0, 0.0)
    c_all = jnp.dot(tri, lw_ref[...], precision=HI, preferred_element_type=F32)

    pairs = range(npairs)
    sls = [slice(p * W, (p + 1) * W) for p in pairs]

    def prep(sl):
        r = r_ref[:, sl]
        k = k_ref[:, sl]
        a = a_ref[:, sl]
        lw = lw_ref[:, sl]
        c = c_all[:, sl]
        kk = k * kk_ref[:, sl]
        kk = kk / jnp.maximum(jnp.sqrt(segsum(kk * kk)), 1e-12)
        kmod = k * (1.0 + (a - 1.0) * ka_ref[:, sl])
        alpha = -kk
        beta = kk * a
        c_mid = c[L // 2 - 1:L // 2, :]
        c_last = c[L - 1:L, :]
        e = c - c_mid
        ex_m = jnp.exp(-e)
        lhs = jnp.concatenate([alpha * jnp.exp(e - lw), r * jnp.exp(e)], axis=0).astype(BF16)
        rhs = jnp.concatenate([blk_b(beta * ex_m), blk_b(kmod * ex_m)], axis=0)
        to_end = jnp.exp(c_last - c)
        upd = jnp.concatenate([beta * to_end, kmod * to_end], axis=0).astype(BF16)
        return dict(lhs=lhs, rhs=rhs, upd=upd, r_abs=r * jnp.exp(c), al_abs=alpha * jnp.exp(c - lw),
                    bonus=segsum(r * kmod * rk_ref[:, sl]))

    pre = [prep(sl) for sl in sls]
    amat = [_dot_nt(q["lhs"], q["rhs"]) for q in pre]
    n_ab = [jnp.where(strict, m[:L, :W], 0.0) for m in amat]
    a_rb = [jnp.where(incl, m[L:, :W], 0.0).astype(BF16) for m in amat]
    a_k = [jnp.concatenate([jnp.where(strict, m[:L, W:], 0.0), jnp.where(incl, m[L:, W:], 0.0)],
                           axis=0).astype(BF16) for m in amat]
    av = [_dot(x, blk_b(v_ref[:, sl])) for x, sl in zip(a_k, sls)]
    akv = [x[:L] for x in av]

    nd = [jnp.where(same[0], n, 0.0) for n in n_ab]
    tinv = [eye + n for n in nd]
    pw = [mm(n, n) for n in nd]
    both = [mm(jnp.concatenate([t, q], axis=0), q) for t, q in zip(tinv, pw)]
    tinv = [t + b[:L] for t, b in zip(tinv, both)]
    tinv = [t + mm(t, b[L:]) for t, b in zip(tinv, both)]
    for lvl in range(1, len(same)):
        off_mask = same[lvl] & jnp.logical_not(same[lvl - 1])
        inner = [mm(jnp.where(off_mask, n, 0.0), t) for n, t in zip(n_ab, tinv)]
        tinv = [t + mm(t, x) for t, x in zip(tinv, inner)]

    ta = [_dot(t.astype(BF16), jnp.concatenate([blk_b(q["al_abs"]), blk_b(x)], axis=1))
          for t, q, x in zip(tinv, pre, akv)]
    ra = [jnp.concatenate([q["r_abs"], t[:, :W]], axis=0).astype(BF16) for q, t in zip(pre, ta)]

    states = [s_ref[p] for p in pairs]
    su = [_dot(x, s.astype(BF16)) for x, s in zip(ra, states)]
    us = [x[L:] + t[:, W:] for x, t in zip(su, ta)]
    ys = [x[:L] + _dot(a, blk_b(u)) + y[L:] for x, a, u, y in zip(su, a_rb, us, av)]
    uv = [jnp.concatenate([u, v_ref[:, sl]], axis=0).astype(BF16) for u, sl in zip(us, sls)]
    z = [_dot_tn(q["upd"], x) for x, q in zip(uv, pre)]
    c_end = jnp.concatenate([c_all[L - 1:L, sl] for sl in sls]
                            + [jnp.zeros((W - npairs, W), F32)], axis=0).T
    for p in pairs:
        s_ref[p] = states[p] * jnp.exp(c_end[:, p:p + 1]) + jnp.where(bdiag, z[p], 0.0)

    for p in pairs:
        sl = sls[p]
        y = ys[p]
        mu = segsum(y) * (1.0 / N)
        yc = y - mu
        var = segsum(yc * yc) * (1.0 / N)
        out = yc * lax.rsqrt(var + RWKV_LNX_EPS) * lg_ref[:, sl] + lb_ref[:, sl]
        out = out + pre[p]["bonus"] * v_ref[:, sl]
        o_ref[:, sl] = (out * g_ref[:, sl]).astype(BF16)


def _rwkv_scan(r, k, v, a, lw, g, k_k, k_a, r_k, lnx_g, lnx_b, npairs=16):
    T, D = r.shape
    L = RWKV_CHUNK
    wl = npairs * 2 * RWKV_HEAD
    tile = pl.BlockSpec((L, wl), lambda hg, c: (c, hg))
    prow = pl.BlockSpec((1, wl), lambda hg, c: (0, hg))
    row = lambda x: x.reshape(1, D)
    return pl.pallas_call(
        functools.partial(_rwkv_scan_kernel, npairs),
        grid=(D // wl, T // L),
        in_specs=[tile] * 6 + [prow] * 5,
        out_specs=tile,
        out_shape=jax.ShapeDtypeStruct((T, D), BF16),
        scratch_shapes=[pltpu.VMEM((npairs, 2 * RWKV_HEAD, 2 * RWKV_HEAD), F32)],
        compiler_params=_params("parallel", "arbitrary"),
        name="rwkv_scan",
    )(r, k, v, a, lw, g, row(k_k), row(k_a), row(r_k), row(lnx_g), row(lnx_b))


def _rope_table_kernel(pos_ref, cos_ref, sin_ref):
    half = cos_ref.shape[1]
    idx = lax.broadcasted_iota(jnp.int32, (1, half), 1).astype(F32)
    freqs = jnp.exp(idx * (-jnp.log(ROPE_BASE) / half))
    ang = pos_ref[...].astype(F32) * freqs
    cos_ref[...] = jnp.cos(ang)
    sin_ref[...] = jnp.sin(ang)


def _rope_table(positions, half, tm=1024):
    T = positions.shape[0]
    return pl.pallas_call(
        _rope_table_kernel,
        grid=(T // tm,),
        in_specs=[pl.BlockSpec((tm, 1), lambda i: (i, 0))],
        out_specs=[pl.BlockSpec((tm, half), lambda i: (i, 0))] * 2,
        out_shape=[jax.ShapeDtypeStruct((T, half), F32)] * 2,
        compiler_params=_params("parallel"),
        name="rope_table",
    )(positions.reshape(T, 1))


def _retention_kernel(q_ref, k_ref, v_ref, g_ref, cos_ref, sin_ref, lg_ref, ln_ref, o_ref, st_ref):
    L = RET_CHUNK
    H = RET_HEADS
    dk = q_ref.shape[1] // H
    dv = v_ref.shape[1] // H
    half = dk // 2

    @pl.when(pl.program_id(0) == 0)
    def _():
        st_ref[...] = jnp.zeros_like(st_ref)

    cos = cos_ref[...]
    sin = sin_ref[...]

    def rope(ref, h):
        x1 = ref[:, h * dk:h * dk + half].astype(F32)
        x2 = ref[:, h * dk + half:(h + 1) * dk].astype(F32)
        return jnp.concatenate([x1 * cos - x2 * sin, x1 * sin + x2 * cos], axis=1)

    row = lax.broadcasted_iota(jnp.int32, (L, L), 0)
    col = lax.broadcasted_iota(jnp.int32, (L, L), 1)
    rel = (row - col).astype(F32)
    tcol = lax.broadcasted_iota(jnp.int32, (L, 1), 0).astype(F32)

    heads = range(H)
    lgs = [lg_ref[h] for h in heads]
    intra = [jnp.where(rel >= 0, jnp.exp(jnp.maximum(rel, 0.0) * lg), 0.0) for lg in lgs]
    lg1 = [lg[:, 0:1] for lg in lgs]
    qb = [rope(q_ref, h).astype(BF16) for h in heads]
    ks = [rope(k_ref, h) * (dk ** -0.5) for h in heads]
    vb = [v_ref[:, h * dv:(h + 1) * dv].astype(BF16) for h in heads]
    s = [(_dot_nt(q, k.astype(BF16)) * m).astype(BF16) for q, k, m in zip(qb, ks, intra)]
    states = [st_ref[h] for h in heads]
    o = [_dot(x, v) + _dot(q, st.astype(BF16)) * jnp.exp((tcol + 1.0) * lg)
         for x, v, q, st, lg in zip(s, vb, qb, states, lg1)]
    kz = [(k * jnp.exp((L - 1.0 - tcol) * lg)).astype(BF16) for k, lg in zip(ks, lg1)]
    upd = [_dot_tn(k, v) for k, v in zip(kz, vb)]
    for h in heads:
        st_ref[h] = jnp.exp(L * lg1[h]) * states[h] + upd[h]

    for h in heads:
        sl = slice(h * dv, (h + 1) * dv)
        mu = jnp.mean(o[h], axis=-1, keepdims=True)
        oc = o[h] - mu
        on = oc * lax.rsqrt(jnp.mean(oc * oc, axis=-1, keepdims=True) + RET_GN_EPS) * ln_ref[:, sl]
        g = g_ref[:, sl].astype(F32)
        o_ref[:, sl] = (g * jax.nn.sigmoid(g) * on).astype(BF16)


def _retention_core(proj, cos, sin, ln_g):
    T = proj.shape[0]
    H = RET_HEADS
    L = RET_CHUNK
    dk = cos.shape[1] * 2
    v_tot = ln_g.shape[0]
    dv = v_tot // H
    qk_tot = H * dk
    hs = jnp.arange(H, dtype=F32)
    log_gamma = jnp.broadcast_to(jnp.log(1.0 - 2.0 ** (-5.0 - hs))[:, None, None], (H, 1, L))
    return pl.pallas_call(
        _retention_kernel,
        grid=(T // L,),
        in_specs=[
            pl.BlockSpec((L, qk_tot), lambda c: (c, 0)),
            pl.BlockSpec((L, qk_tot), lambda c: (c, 1)),
            pl.BlockSpec((L, v_tot), lambda c: (c, (2 * qk_tot) // v_tot)),
            pl.BlockSpec((L, v_tot), lambda c: (c, (2 * qk_tot) // v_tot + 1)),
            pl.BlockSpec((L, dk // 2), lambda c: (c, 0)),
            pl.BlockSpec((L, dk // 2), lambda c: (c, 0)),
            pl.BlockSpec((H, 1, L), lambda c: (0, 0, 0)),
            pl.BlockSpec((1, v_tot), lambda c: (0, 0)),
        ],
        out_specs=pl.BlockSpec((L, v_tot), lambda c: (c, 0)),
        out_shape=jax.ShapeDtypeStruct((T, v_tot), BF16),
        scratch_shapes=[pltpu.VMEM((H, dk, dv), F32)],
        compiler_params=_params("arbitrary"),
        name="retention",
    )(proj, proj, proj, proj, cos, sin, log_gamma, ln_g.reshape(1, v_tot))


def _mlstm_kernel(q_ref, k_ref, v_ref, og_ref, gates_ref, gb_ref, mh_ref, o_ref, c_ref, n_ref, m_ref):
    L = ML_CHUNK
    H = ML_HEADS
    dqk = q_ref.shape[1] // H
    dv = v_ref.shape[1] // H

    @pl.when(pl.program_id(0) == 0)
    def _():
        c_ref[...] = jnp.zeros_like(c_ref)
        n_ref[...] = jnp.zeros_like(n_ref)
        m_ref[...] = jnp.zeros_like(m_ref)

    row = lax.broadcasted_iota(jnp.int32, (L, L), 0)
    col = lax.broadcasted_iota(jnp.int32, (L, L), 1)
    causal = col <= row

    def chunk(r0, c_state, n_state, m_prev):
        rs = slice(r0, r0 + L)
        gates = gates_ref[rs, :] + gb_ref[...]
        glane = lax.broadcasted_iota(jnp.int32, gates.shape, 1)
        log_i = ML_IGATE_CAP * jnp.tanh(gates / ML_IGATE_CAP)
        log_f = jnp.minimum(gates, 0.0) - jnp.log1p(jnp.exp(-jnp.abs(gates)))
        act = jnp.where(glane < H, log_i, log_f)
        cum_f = jnp.dot(jnp.where(causal, 1.0, 0.0), act[:, H:], precision=HI, preferred_element_type=F32)
        cols = jnp.concatenate([act, cum_f, jnp.zeros((L, LANES - 3 * H), F32)], axis=1)
        rows = cols.T

        ic = [cols[:, h:h + 1] for h in heads]
        b_col = [cols[:, 2 * H + h:2 * H + h + 1] for h in heads]
        i_row = [rows[h:h + 1, :] for h in heads]
        b_row = [rows[2 * H + h:2 * H + h + 1, :] for h in heads]
        b_last = [b[L - 1:L, :] for b in b_col]
        log_inter = [b + m for b, m in zip(b_col, m_prev)]
        log_intra = [jnp.where(causal, bc - br + ir, -jnp.inf) for bc, br, ir in zip(b_col, b_row, i_row)]
        m_t = [jnp.maximum(x, jnp.max(y, axis=1, keepdims=True)) for x, y in zip(log_inter, log_intra)]
        log_s = [bl - bc + i for bl, bc, i in zip(b_last, b_col, ic)]
        m_new = [jnp.maximum(bl + m, jnp.max(s, axis=0, keepdims=True)) for bl, m, s in zip(b_last, m_prev, log_s)]
        w_inter = [jnp.exp(x - mt) for x, mt in zip(log_inter, m_t)]
        w_intra = [jnp.exp(y - mt) for y, mt in zip(log_intra, m_t)]
        ws = [jnp.exp(s - mn) for s, mn in zip(log_s, m_new)]
        carry = [jnp.exp(bl + m - mn) for bl, m, mn in zip(b_last, m_prev, m_new)]
        qs = [q_ref[rs, h * dqk:(h + 1) * dqk].astype(F32) for h in heads]
        ks = [k_ref[rs, h * dqk:(h + 1) * dqk].astype(F32) * (dqk ** -0.5) for h in heads]
        qb = [q.astype(BF16) for q in qs]
        vb = [v_ref[rs, h * dv:(h + 1) * dv].astype(BF16) for h in heads]
        s = [_dot_nt(q, k.astype(BF16)) * w for q, k, w in zip(qb, ks, w_intra)]
        num = [_dot(x.astype(BF16), v) + w * _dot(q, c.astype(BF16))
               for x, v, w, q, c in zip(s, vb, w_inter, qb, c_state)]
        den = [jnp.sum(x, axis=1, keepdims=True) + w * jnp.sum(q * n, axis=1, keepdims=True)
               for x, w, q, n in zip(s, w_inter, qs, n_state)]
        kw = [k * w for k, w in zip(ks, ws)]
        upd = [_dot_tn(k.astype(BF16), v) for k, v in zip(kw, vb)]
        c_next = [cy * c + u for cy, c, u in zip(carry, c_state, upd)]
        n_next = [cy * n + jnp.sum(k, axis=0, keepdims=True) for cy, n, k in zip(carry, n_state, kw)]
        for h in heads:
            sl = slice(h * dv, (h + 1) * dv)
            hid = num[h] / jnp.maximum(jnp.abs(den[h]), jnp.exp(-m_t[h]))
            hn = hid * lax.rsqrt(jnp.mean(hid * hid, axis=-1, keepdims=True) + ML_NORM_EPS) * mh_ref[:, sl]
            o_ref[rs, sl] = (jax.nn.sigmoid(og_ref[rs, sl].astype(F32)) * hn).astype(BF16)
        return c_next, n_next, m_new

    heads = range(H)
    state = ([c_ref[h] for h in heads], [n_ref[h] for h in heads], [m_ref[h] for h in heads])
    for r0 in range(0, q_ref.shape[0], L):
        state = chunk(r0, *state)
    for h in heads:
        c_ref[h] = state[0][h]
        n_ref[h] = state[1][h]
        m_ref[h] = state[2][h]


def _mlstm_core(proj, gates, gate_b, mh_g, chunks_per_step=4):
    T = proj.shape[0]
    H = ML_HEADS
    rows = chunks_per_step * ML_CHUNK
    v_tot = mh_g.shape[0]
    dv = v_tot // H
    qk_tot = (proj.shape[1] - 2 * v_tot) // 2
    dqk = qk_tot // H
    return pl.pallas_call(
        _mlstm_kernel,
        grid=(T // rows,),
        in_specs=[
            pl.BlockSpec((rows, qk_tot), lambda c: (c, 0)),
            pl.BlockSpec((rows, qk_tot), lambda c: (c, 1)),
            pl.BlockSpec((rows, v_tot), lambda c: (c, (2 * qk_tot) // v_tot)),
            pl.BlockSpec((rows, v_tot), lambda c: (c, (2 * qk_tot) // v_tot + 1)),
            pl.BlockSpec((rows, 2 * H), lambda c: (c, 0)),
            pl.BlockSpec((1, 2 * H), lambda c: (0, 0)),
            pl.BlockSpec((1, v_tot), lambda c: (0, 0)),
        ],
        out_specs=pl.BlockSpec((rows, v_tot), lambda c: (c, 0)),
        out_shape=jax.ShapeDtypeStruct((T, v_tot), BF16),
        scratch_shapes=[pltpu.VMEM((H, dqk, dv), F32), pltpu.VMEM((H, 1, dqk), F32),
                        pltpu.VMEM((H, 1, 1), F32)],
        compiler_params=_params("arbitrary"),
        name="mlstm",
    )(proj, proj, proj, proj, gates, gate_b.reshape(1, 2 * H), mh_g.reshape(1, v_tot))


def _gate_proj_kernel(x_ref, g_ref, w_ref, o_ref):
    o_ref[...] = _dot(_rms(x_ref[...], g_ref[...]).astype(BF16), w_ref[...])


def _gate_proj(h, gain, w, tm=512):
    T, D = h.shape
    n = w.shape[1]
    return pl.pallas_call(
        _gate_proj_kernel,
        grid=(T // tm,),
        in_specs=[
            pl.BlockSpec((tm, D), lambda i: (i, 0)),
            pl.BlockSpec((1, D), lambda i: (0, 0)),
            pl.BlockSpec((D, n), lambda i: (0, 0)),
        ],
        out_specs=pl.BlockSpec((tm, n), lambda i: (i, 0)),
        out_shape=jax.ShapeDtypeStruct((T, n), F32),
        compiler_params=_params("parallel"),
        name="gate_proj",
    )(h, gain.reshape(1, D), w)


def kernel(x, p, positions, norm_g, final_g, ffn_in, ffn_out, ple_proj, ple_gate, rwkv_lerp, rwkv_w0, rwkv_w1, rwkv_w2, rwkv_a0, rwkv_a1, rwkv_a2, rwkv_g1, rwkv_g2, rwkv_kk, rwkv_ka, rwkv_rk, rwkv_w_rkv, rwkv_w_o, rwkv_lnx_g, rwkv_lnx_b, rwkv_v0, rwkv_v1, rwkv_v2, ret_w_in, ret_ln_g, ret_w_o, ml_w_in, ml_gate_b, ml_mh_g, ml_w_o):
    B, T, D = x.shape
    depth = norm_g.shape[0]
    bf = lambda w: w.astype(BF16)
    ffn_in_b, ffn_out_b = bf(ffn_in), bf(ffn_out)
    ple_proj_b, ple_gate_b = bf(ple_proj), bf(ple_gate)
    rwkv_w_rkv_b, rwkv_w_o_b = bf(rwkv_w_rkv), bf(rwkv_w_o)
    ret_w_in_b, ret_w_o_b = bf(ret_w_in), bf(ret_w_o)
    ml_w_in_b, ml_w_o_b = bf(ml_w_in), bf(ml_w_o)
    outs = []
    for b in range(B):
        h = x[b]
        v_first = None
        for i in range(depth):
            kind, j = i % 3, i // 3
            h = _ffn(h, norm_g[i, 0], ffn_in, ffn_out, (i, 0))
            if kind == 0:
                v_res = None
                if j > 0:
                    v_res = (rwkv_v0[j - 1], bf(rwkv_v1[j - 1]), bf(rwkv_v2[j - 1]))
                r, k, v, a, lw, g = _rwkv_proj(
                    h, norm_g[i, 1], rwkv_lerp[j], rwkv_w_rkv_b, (j,), bf(rwkv_w1[j]), bf(rwkv_w2[j]),
                    bf(rwkv_a1[j]), bf(rwkv_a2[j]), bf(rwkv_g1[j]), bf(rwkv_g2[j]),
                    rwkv_w0[j], rwkv_a0[j], v_res, v_first)
                if j == 0:
                    v_first = v
                mix_in = _rwkv_scan(r, k, v, a, lw, g, rwkv_kk[j], rwkv_ka[j], rwkv_rk[j].reshape(D),
                                    rwkv_lnx_g[j], rwkv_lnx_b[j])
                w_o = rwkv_w_o_b
            elif kind == 1:
                proj = _norm_matmul(h, norm_g[i, 1], ret_w_in_b, (j,), ret_w_in.shape[2])
                half = ret_w_in.shape[2] // 6 // RET_HEADS // 2
                cos, sin = _rope_table(positions[b], half)
                mix_in = _retention_core(proj, cos, sin, ret_ln_g[j])
                w_o = ret_w_o_b
            else:
                n_main = ml_w_in.shape[2] - 2 * ML_HEADS
                proj = _norm_matmul(h, norm_g[i, 1], ml_w_in_b, (j,), n_main)
                gates = _gate_proj(h, norm_g[i, 1], ml_w_in_b[j][:, n_main:])
                mix_in = _mlstm_core(proj, gates, ml_gate_b[j], ml_mh_g[j])
                w_o = ml_w_o_b
            h = _matmul_residual(h, mix_in, w_o, (j,))
            h = _ffn(h, norm_g[i, 2], ffn_in, ffn_out, (i, 1))
            h = _ple(h, norm_g[i, 3], p[i, b], ple_proj_b, ple_gate_b, (i,), final_g, i == depth - 1)
        outs.append(h)
    return outs[0][None] if B == 1 else jnp.stack(outs, axis=0)
```
